```python
import math
import jax, jax.numpy as jnp
from jax import lax
import numpy as np

D_MODEL = 2048
BATCH = 1
SEQ = 8192
DEPTH = 4

EPS = 1e-5
RWKV_WIDTH = D_MODEL
RWKV_HEAD = 64
RWKV_HEADS = RWKV_WIDTH // RWKV_HEAD
DECAY_LORA = max(32, int(round(1.8 * D_MODEL ** 0.5 / 32)) * 32)
AAA_LORA = max(32, int(round(1.8 * D_MODEL ** 0.5 / 32)) * 32)
MV_LORA = max(32, int(round(1.3 * D_MODEL ** 0.5 / 32)) * 32)
RWKV_GN_EPS = 64e-5
SSM_WIDTH = D_MODEL
SSM_HEAD = 64
SSM_HEADS = SSM_WIDTH // SSM_HEAD
SSM_GROUPS = 4
SSM_STATE = 128
SSM_CONV = 4
SSM_CHUNK = 128
GLA_HEADS = 4
GLA_KEY = D_MODEL // 2
GLA_VAL = D_MODEL
GLA_HK = GLA_KEY // GLA_HEADS
GLA_HV = GLA_VAL // GLA_HEADS
GLA_GATE_RANK = 16
GLA_GATE_NORM = 16.0
GLA_CHUNK = 64

N_EVEN = (DEPTH + 1) // 2
N_ODD = DEPTH // 2
RWKV_COLS = 4 * RWKV_WIDTH + DECAY_LORA + AAA_LORA
SSM_CONV_DIM = SSM_WIDTH + 2 * SSM_GROUPS * SSM_STATE
SSM_COLS = SSM_WIDTH + SSM_CONV_DIM + SSM_HEADS
EVEN_COLS = RWKV_COLS + SSM_COLS
GLA_COLS = 2 * GLA_KEY + 2 * GLA_VAL + GLA_GATE_RANK

kernel_name = "hybrid_rwkv7_mamba2_gla_trunk"


def split_cols(t, sizes):
    offs = np.cumsum(sizes)[:-1].tolist()
    return jnp.split(t, offs, axis=-1)


def rms_norm(x, w, eps=EPS):
    xf = x.astype(jnp.float32)
    return xf * lax.rsqrt(jnp.mean(xf * xf, axis=-1, keepdims=True) + eps) * w


def token_shift(p):
    return jnp.pad(p, ((0, 0), (1, 0), (0, 0)))[:, :-1]


def rwkv7_scan(r, w, k, v, a, b):
    def step(S, inp):
        r_t, w_t, k_t, v_t, a_t, b_t = inp
        sa = jnp.einsum('bhij,bhj->bhi', S, a_t)
        S = S * w_t[:, :, None, :] + sa[..., None] * b_t[:, :, None, :] + v_t[..., None] * k_t[:, :, None, :]
        return S, jnp.einsum('bhij,bhj->bhi', S, r_t)
    bsz, _, h, n = r.shape
    xs = tuple(jnp.moveaxis(t, 1, 0) for t in (r, w, k, v, a, b))
    _, y = lax.scan(step, jnp.zeros((bsz, h, n, n), jnp.float32), xs)
    return jnp.moveaxis(y, 0, 1)


def rwkv7_branch(p, mu, w0, w2, a0, a2, k_k, k_a, r_k, ln_w, ln_b, v_first, v_mix):
    bsz, l, _ = p.shape
    p = p.astype(jnp.float32)
    p = p + (token_shift(p) - p) * mu
    r, k, v, g, wd, ad = split_cols(p, [RWKV_WIDTH] * 4 + [DECAY_LORA, AAA_LORA])
    w_log = -jax.nn.softplus(-(w0 + jnp.tanh(wd) @ w2)) - 0.5
    a = jax.nn.sigmoid(a0 + ad @ a2)
    if v_mix is not None:
        v0, v1, v2 = v_mix
        v = v + (v_first - v) * jax.nn.sigmoid(v0 + (v @ v1) @ v2)
    heads = lambda t: t.reshape(bsz, l, RWKV_HEADS, RWKV_HEAD)
    kk = heads(k * k_k)
    kk = kk * lax.rsqrt(jnp.maximum(jnp.sum(kk * kk, -1, keepdims=True), 1e-24))
    a_h = heads(a)
    k_h = heads(k * (1.0 + (a - 1.0) * k_a))
    r_h, v_h = heads(r), heads(v)
    decay = jnp.exp(-jnp.exp(heads(w_log)))
    y = rwkv7_scan(r_h, decay, k_h, v_h, -kk, kk * a_h)
    mean = jnp.mean(y, -1, keepdims=True)
    var = jnp.mean(jnp.square(y - mean), -1, keepdims=True)
    y = ((y - mean) * lax.rsqrt(var + RWKV_GN_EPS)).reshape(bsz, l, RWKV_WIDTH) * ln_w + ln_b
    bonus = (jnp.sum(r_h * k_h * r_k, -1, keepdims=True) * v_h).reshape(bsz, l, RWKV_WIDTH)
    return (y + bonus) * jax.nn.silu(g), v


def causal_depthwise_conv(x, w, bias):
    c = x.shape[-1]
    out = lax.conv_general_dilated(x, w.astype(x.dtype)[:, None, :], window_strides=(1,),
                                   padding=[(w.shape[0] - 1, 0)],
                                   dimension_numbers=('NWC', 'WIO', 'NWC'),
                                   feature_group_count=c)
    return out + bias


def ssd_chunked(x, dt, A, Bm, Cm):
    b, l, h, pd = x.shape
    g, n = Bm.shape[2], Bm.shape[3]
    hg, q, c = h // g, SSM_CHUNK, l // SSM_CHUNK
    xc = (x * dt[..., None]).reshape(b, c, q, g, hg, pd)
    acs = jnp.cumsum((dt * A).reshape(b, c, q, g, hg), axis=2)
    Bc = Bm.reshape(b, c, q, g, n)
    Cc = Cm.reshape(b, c, q, g, n)
    causal = jnp.tril(jnp.ones((q, q), bool))[:, :, None, None]
    diff = acs[:, :, :, None] - acs[:, :, None, :]
    L = jnp.exp(jnp.where(causal, diff, -jnp.inf))
    scores = jnp.einsum('bcign,bcjgn->bcijg', Cc, Bc)
    y_diag = jnp.einsum('bcijgh,bcjghp->bcighp', scores[..., None] * L, xc)
    decay_states = jnp.exp(acs[:, :, -1:] - acs)
    states = jnp.einsum('bcjgn,bcjgh,bcjghp->bcghpn', Bc, decay_states, xc)
    chunk_decay = jnp.exp(acs[:, :, -1])

    def step(S, inp):
        st, dec = inp
        return S * dec[..., None, None] + st, S
    _, prev = lax.scan(step, jnp.zeros((b, g, hg, pd, n), jnp.float32),
                       (jnp.moveaxis(states, 1, 0), jnp.moveaxis(chunk_decay, 1, 0)))
    prev = jnp.moveaxis(prev, 0, 1)
    y_off = jnp.einsum('bcign,bcghpn,bcigh->bcighp', Cc, prev, jnp.exp(acs))
    return (y_diag + y_off).reshape(b, l, h, pd)


def ssd_branch(p, conv_w, conv_b, dt_bias, A_log, D_skip, norm_w):
    bsz, l, _ = p.shape
    p = p.astype(jnp.float32)
    z, xbc, dt = split_cols(p, [SSM_WIDTH, SSM_CONV_DIM, SSM_HEADS])
    xbc = jax.nn.silu(causal_depthwise_conv(xbc, conv_w.astype(jnp.float32), conv_b))
    xs, Bm, Cm = split_cols(xbc, [SSM_WIDTH, SSM_GROUPS * SSM_STATE, SSM_GROUPS * SSM_STATE])
    xs = xs.reshape(bsz, l, SSM_HEADS, SSM_HEAD)
    Bm = Bm.reshape(bsz, l, SSM_GROUPS, SSM_STATE)
    Cm = Cm.reshape(bsz, l, SSM_GROUPS, SSM_STATE)
    dt = jax.nn.softplus(dt + dt_bias)
    A = -jnp.exp(A_log.astype(jnp.float32))
    y = ssd_chunked(xs, dt, A, Bm, Cm) + D_skip[:, None] * xs
    y = y.reshape(bsz, l, SSM_WIDTH) * jax.nn.silu(z)
    y = rms_norm(y.reshape(bsz, l, SSM_GROUPS, SSM_WIDTH // SSM_GROUPS), 1.0)
    return y.reshape(bsz, l, SSM_WIDTH) * norm_w


def gla_chunked(q, k, v, gk):
    b, l, h, dk = q.shape
    dv = v.shape[-1]
    qn, c = GLA_CHUNK, l // GLA_CHUNK
    q, k, gk = (t.reshape(b, c, qn, h, dk) for t in (q, k, gk))
    v = v.reshape(b, c, qn, h, dv)
    bc = jnp.cumsum(gk, axis=2)
    mid = bc[:, :, qn // 2][:, :, None]
    attn = jnp.einsum('bcihd,bcjhd->bchij', q * jnp.exp(bc - mid), k * jnp.exp(mid - bc))
    attn = jnp.where(jnp.tril(jnp.ones((qn, qn), bool)), attn, 0.0)
    o_intra = jnp.einsum('bchij,bcjhv->bcihv', attn, v)
    q_in = q * jnp.exp(bc)
    k_st = k * jnp.exp(bc[:, :, -1:] - bc)
    last = jnp.exp(bc[:, :, -1])

    def step(S, inp):
        q_c, k_c, v_c, d_c = inp
        o = jnp.einsum('bihd,bhdv->bihv', q_c, S)
        S = S * d_c[..., None] + jnp.einsum('bjhd,bjhv->bhdv', k_c, v_c)
        return S, o
    xs = tuple(jnp.moveaxis(t, 1, 0) for t in (q_in, k_st, v, last))
    _, o_inter = lax.scan(step, jnp.zeros((b, h, dk, dv), jnp.float32), xs)
    return (o_intra + jnp.moveaxis(o_inter, 0, 1)).reshape(b, l, h, dv)


def gla_branch(p, gk_w, gk_b, norm_w):
    bsz, l, _ = p.shape
    p = p.astype(jnp.float32)
    q, k, v, g, gd = split_cols(p, [GLA_KEY, GLA_KEY, GLA_VAL, GLA_VAL, GLA_GATE_RANK])
    q = q.reshape(bsz, l, GLA_HEADS, GLA_HK) * (GLA_HK ** -0.5)
    k = k.reshape(bsz, l, GLA_HEADS, GLA_HK)
    v = v.reshape(bsz, l, GLA_HEADS, GLA_HV)
    gk = (jax.nn.log_sigmoid(gd @ gk_w + gk_b) / GLA_GATE_NORM).reshape(bsz, l, GLA_HEADS, GLA_HK)
    o = rms_norm(gla_chunked(q, k, v, gk), norm_w)
    return o.reshape(bsz, l, GLA_VAL) * jax.nn.silu(g)


def setup_inputs(seed: int = 0) -> dict:
    key = jax.random.key(seed)
    ks = iter(jax.random.split(key, 40))
    nrm = lambda shape, scale: jax.random.normal(next(ks), shape, jnp.float32) * scale
    ne, no, nv = N_EVEN, N_ODD, max(N_EVEN - 1, 0)
    dt = jnp.exp(jax.random.uniform(next(ks), (ne, SSM_HEADS), jnp.float32,
                                    math.log(1e-3), math.log(1e-1)))
    return {
        "x": nrm((BATCH, SEQ, D_MODEL), 1.0),
        "norm_w": 1.0 + nrm((DEPTH, D_MODEL), 0.02),
        "final_norm_w": 1.0 + nrm((D_MODEL,), 0.02),
        "w_in_even": nrm((ne, D_MODEL, EVEN_COLS), D_MODEL ** -0.5),
        "w_out_even": nrm((ne, RWKV_WIDTH + SSM_WIDTH, D_MODEL), (RWKV_WIDTH + SSM_WIDTH) ** -0.5),
        "rwkv_mu": jax.random.uniform(next(ks), (ne, RWKV_COLS), jnp.float32),
        "rwkv_w0": jax.random.uniform(next(ks), (ne, RWKV_WIDTH), jnp.float32, -6.0, 1.0),
        "rwkv_w2": nrm((ne, DECAY_LORA, RWKV_WIDTH), 0.5 * DECAY_LORA ** -0.5),
        "rwkv_a0": nrm((ne, RWKV_WIDTH), 0.1),
        "rwkv_a2": nrm((ne, AAA_LORA, RWKV_WIDTH), 0.5 * AAA_LORA ** -0.5),
        "rwkv_k_k": 0.85 + nrm((ne, RWKV_WIDTH), 0.02),
        "rwkv_k_a": 1.0 + nrm((ne, RWKV_WIDTH), 0.02),
        "rwkv_r_k": nrm((ne, RWKV_HEADS, RWKV_HEAD), 0.1),
        "rwkv_ln_w": 1.0 + nrm((ne, RWKV_WIDTH), 0.02),
        "rwkv_ln_b": nrm((ne, RWKV_WIDTH), 0.01),
        "rwkv_v0": 1.0 + nrm((nv, RWKV_WIDTH), 0.1),
        "rwkv_v1": nrm((nv, RWKV_WIDTH, MV_LORA), RWKV_WIDTH ** -0.5),
        "rwkv_v2": nrm((nv, MV_LORA, RWKV_WIDTH), 0.5 * MV_LORA ** -0.5),
        "ssm_conv_w": nrm((ne, SSM_CONV, SSM_CONV_DIM), SSM_CONV ** -0.5),
        "ssm_conv_b": nrm((ne, SSM_CONV_DIM), 0.01),
        "ssm_dt_bias": dt + jnp.log(-jnp.expm1(-dt)),
        "ssm_A_log": jnp.log(jax.random.uniform(next(ks), (ne, SSM_HEADS), jnp.float32, 1.0, 16.0)),
        "ssm_D": 1.0 + nrm((ne, SSM_HEADS), 0.1),
        "ssm_norm_w": 1.0 + nrm((ne, SSM_WIDTH), 0.02),
        "w_in_odd": nrm((no, D_MODEL, GLA_COLS), D_MODEL ** -0.5),
        "w_out_odd": nrm((no, GLA_VAL, D_MODEL), GLA_VAL ** -0.5),
        "gla_gk_w": nrm((no, GLA_GATE_RANK, GLA_KEY), GLA_GATE_RANK ** -0.5),
        "gla_gk_b": nrm((no, GLA_KEY), 0.01),
        "gla_norm_w": 1.0 + nrm((no, GLA_HV), 0.02),
    }


def reference(x, norm_w, final_norm_w, w_in_even, w_out_even, rwkv_mu, rwkv_w0, rwkv_w2,
              rwkv_a0, rwkv_a2, rwkv_k_k, rwkv_k_a, rwkv_r_k, rwkv_ln_w, rwkv_ln_b,
              rwkv_v0, rwkv_v1, rwkv_v2, ssm_conv_w, ssm_conv_b, ssm_dt_bias, ssm_A_log,
              ssm_D, ssm_norm_w, w_in_odd, w_out_odd, gla_gk_w, gla_gk_b, gla_norm_w):
    res = x
    v_first = None
    for layer in range(DEPTH):
        h = rms_norm(res, norm_w[layer]).astype(x.dtype)
        i = layer // 2
        if layer % 2 == 0:
            p = h @ w_in_even[i]
            p_a, p_b = p[..., :RWKV_COLS], p[..., RWKV_COLS:]
            v_mix = None if i == 0 else (rwkv_v0[i - 1], rwkv_v1[i - 1], rwkv_v2[i - 1])
            y_a, v_a = rwkv7_branch(p_a, rwkv_mu[i], rwkv_w0[i], rwkv_w2[i], rwkv_a0[i],
                                    rwkv_a2[i], rwkv_k_k[i], rwkv_k_a[i], rwkv_r_k[i],
                                    rwkv_ln_w[i], rwkv_ln_b[i], v_first, v_mix)
            if i == 0:
                v_first = v_a
            y_b = ssd_branch(p_b, ssm_conv_w[i], ssm_conv_b[i], ssm_dt_bias[i], ssm_A_log[i],
                             ssm_D[i], ssm_norm_w[i])
            y = jnp.concatenate([y_a, y_b], axis=-1) @ w_out_even[i]
        else:
            p = h @ w_in_odd[i]
            y = gla_branch(p, gla_gk_w[i], gla_gk_b[i], gla_norm_w[i]) @ w_out_odd[i]
        res = res + y.astype(res.dtype)
    return rms_norm(res, final_norm_w).astype(x.dtype)
```

```python
import functools

import jax
import jax.numpy as jnp
from jax import lax
from jax.experimental import pallas as pl
from jax.experimental.pallas import tpu as pltpu

F32 = jnp.float32
BF16 = jnp.bfloat16

D_MODEL = 2048
EPS = 1e-5
RW = 2048
RH = 64
DECAY_LORA = 96
AAA_LORA = 96
MV_LORA = 64
RWKV_GN_EPS = 64e-5
RCHUNK = 64
SW = 2048
SHEAD = 64
SGROUPS = 4
SSTATE = 128
SCONV = 4
SCHUNK = 128
SGW = SW // SGROUPS
SHG = SGW // SHEAD
GHEADS = 4
GKEY = 1024
GVAL = 2048
GHK = 256
GHV = 512
GRANK = 16
GCHUNK = 64

LANE = 128
SUB = 8
VMEM_LIMIT = 56 * 1024 * 1024

E_R, E_K, E_V, E_G = 0, 2048, 4096, 6144
E_Z, E_X = 8192, 10240
E_B, E_C = 12288, 12800
E_WD, E_AD, E_DT = 13312, 13440, 13568
E_COLS = 13824
O_Q, O_K, O_V, O_G, O_GD = 0, 1024, 2048, 4096, 6144
O_COLS = 6272

NN = (((1,), (0,)), ((), ()))
NT = (((1,), (1,)), ((), ()))
TN = (((0,), (0,)), ((), ()))


def _dg(a, b, dims=NN):
    return lax.dot_general(a, b, dims, preferred_element_type=F32)


def _bdot(a, b, dims=NN):
    return _dg(a.astype(BF16), b.astype(BF16), dims)


def _split2(x):
    hi = x.astype(BF16)
    lo = (x - hi.astype(F32)).astype(BF16)
    return hi, lo


def _split3(x):
    hi = x.astype(BF16)
    r = x - hi.astype(F32)
    mid = r.astype(BF16)
    lo = (r - mid.astype(F32)).astype(BF16)
    return hi, mid, lo


def _dot3(a, b, dims=NN):
    ah, al = _split2(a)
    bh, bl = _split2(b)
    return _dg(ah, bh, dims) + (_dg(ah, bl, dims) + _dg(al, bh, dims))


def _dot_sel(a, sel_bf16, dims=NN):
    ah, am, al = _split3(a)
    return _dg(ah, sel_bf16, dims) + (_dg(am, sel_bf16, dims) + _dg(al, sel_bf16, dims))


def _sigmoid(x):
    return 1.0 / (1.0 + jnp.exp(-x))


def _silu(x):
    return x * _sigmoid(x)


def _log_sigmoid(x):
    return jnp.minimum(x, 0.0) - jnp.log(1.0 + jnp.exp(-jnp.abs(x)))


def _softplus(x):
    return jnp.maximum(x, 0.0) + jnp.log(1.0 + jnp.exp(-jnp.abs(x)))


def _iota(shape, axis):
    return lax.broadcasted_iota(jnp.int32, shape, axis)


def _idiv(x, n):
    return lax.shift_right_logical(x, n.bit_length() - 1)


def _seg_cumsum(x, chunk):
    rin = _iota(x.shape, 0) & (chunk - 1)
    s = 1
    while s < chunk:
        x = x + jnp.where(rin >= s, pltpu.roll(x, s, 0), 0.0)
        s *= 2
    return x


def _chunk_row(x, chunk, r):
    t, w = x.shape
    x3 = x.reshape(t // chunk, chunk, w)
    return jnp.broadcast_to(x3[:, r:r + 1, :], x3.shape).reshape(t, w)


def _head_ones(n, head, scale=1.0):
    r = _idiv(_iota((n, n), 0), head)
    c = _idiv(_iota((n, n), 1), head)
    return jnp.where(r == c, scale, 0.0).astype(BF16)


def _params(**kw):
    return pltpu.CompilerParams(vmem_limit_bytes=VMEM_LIMIT, **kw)


def _rmsnorm_kernel(x_ref, w_ref, o_ref):
    x = x_ref[...]
    ms = jnp.mean(x * x, axis=-1, keepdims=True)
    o_ref[...] = (x * lax.rsqrt(ms + EPS) * w_ref[0:1, :]).astype(BF16)


def _rmsnorm(x, w8, tm=512):
    l, d = x.shape
    return pl.pallas_call(
        _rmsnorm_kernel,
        grid=(l // tm,),
        in_specs=[pl.BlockSpec((tm, d), lambda i: (i, 0)),
                  pl.BlockSpec((SUB, d), lambda i: (0, 0))],
        out_specs=pl.BlockSpec((tm, d), lambda i: (i, 0)),
        out_shape=jax.ShapeDtypeStruct((l, d), BF16),
        compiler_params=_params(dimension_semantics=("parallel",)),
        name="rmsnorm",
    )(x, w8)


def _inproj_kernel(h_ref, w_ref, o_ref):
    o_ref[...] = jnp.dot(h_ref[...], w_ref[...], preferred_element_type=F32)


def _inproj(h, w, tn, tm=1024):
    l, d = h.shape
    n = w.shape[1]
    tm = min(tm, l)
    return pl.pallas_call(
        _inproj_kernel,
        grid=(l // tm, n // tn),
        in_specs=[pl.BlockSpec((tm, d), lambda i, j: (i, 0)),
                  pl.BlockSpec((d, tn), lambda i, j: (0, j))],
        out_specs=pl.BlockSpec((tm, tn), lambda i, j: (i, j)),
        out_shape=jax.ShapeDtypeStruct((l, n), F32),
        compiler_params=_params(dimension_semantics=("parallel", "parallel")),
        name="inproj",
    )(h, w)


def _outproj_kernel(*refs, n_in, kt_per_in, final):
    y_refs = refs[:n_in]
    w_ref, res_ref, nw_ref = refs[n_in:n_in + 3]
    outs = refs[n_in + 3:]
    acc_ref = outs[-1]
    k = pl.program_id(1)
    nk = pl.num_programs(1)

    @pl.when(k == 0)
    def _():
        acc_ref[...] = res_ref[...]

    for s in range(n_in):
        @pl.when((k >= s * kt_per_in) & (k < (s + 1) * kt_per_in))
        def _(s=s):
            acc_ref[...] += jnp.dot(y_refs[s][...], w_ref[...], preferred_element_type=F32)

    @pl.when(k == nk - 1)
    def _():
        r = acc_ref[...]
        ms = jnp.mean(r * r, axis=-1, keepdims=True)
        hn = r * lax.rsqrt(ms + EPS) * nw_ref[0:1, :]
        if final:
            outs[0][...] = hn
        else:
            outs[0][...] = r
            outs[1][...] = hn.astype(BF16)


def _outproj(ys, w, res, nw8, final, tm=512, tk=1024):
    l, d = res.shape
    n_in = len(ys)
    kt_per_in = ys[0].shape[1] // tk
    nk = n_in * kt_per_in
    tm = min(tm, l)

    def y_map(s):
        return lambda i, k: (i, jnp.clip(k - s * kt_per_in, 0, kt_per_in - 1))

    in_specs = [pl.BlockSpec((tm, tk), y_map(s)) for s in range(n_in)]
    in_specs += [pl.BlockSpec((tk, d), lambda i, k: (k, 0)),
                 pl.BlockSpec((tm, d), lambda i, k: (i, 0)),
                 pl.BlockSpec((SUB, d), lambda i, k: (0, 0))]
    row_spec = pl.BlockSpec((tm, d), lambda i, k: (i, 0))
    if final:
        out_specs = [row_spec]
        out_shape = [jax.ShapeDtypeStruct((l, d), F32)]
    else:
        out_specs = [row_spec, row_spec]
        out_shape = [jax.ShapeDtypeStruct((l, d), F32), jax.ShapeDtypeStruct((l, d), BF16)]
    return pl.pallas_call(
        functools.partial(_outproj_kernel, n_in=n_in, kt_per_in=kt_per_in, final=final),
        grid=(l // tm, nk),
        in_specs=in_specs,
        out_specs=out_specs,
        out_shape=out_shape,
        scratch_shapes=[pltpu.VMEM((tm, d), F32)],
        compiler_params=_params(dimension_semantics=("parallel", "arbitrary")),
        name="outproj",
    )(*ys, w, res, nw8)


def _shift_lerp(x, halo_ref, mu, first):
    prev_last = jnp.where(first, 0.0, halo_ref[SUB - 1:SUB, :])
    xs = pltpu.roll(x, 1, 0)
    xprev = jnp.where(_iota(x.shape, 0) == 0, prev_last, xs)
    return x + (xprev - x) * mu


def _vlora_kernel(v_ref, vh_ref, mu_ref, v1_ref, o_ref):
    first = pl.program_id(0) == 0
    v = _shift_lerp(v_ref[...], vh_ref, mu_ref[0:1, :], first)
    o_ref[...] = _dot3(v, v1_ref[...])


def _vlora(p, mu8, v1p, t=256):
    l = p.shape[0]
    t = min(t, l)
    hb = t // SUB
    cb = E_V // RW
    return pl.pallas_call(
        _vlora_kernel,
        grid=(l // t,),
        in_specs=[pl.BlockSpec((t, RW), lambda i: (i, cb)),
                  pl.BlockSpec((SUB, RW), lambda i: (jnp.maximum(i * hb - 1, 0), cb)),
                  pl.BlockSpec((SUB, RW), lambda i: (0, 0)),
                  pl.BlockSpec((RW, LANE), lambda i: (0, 0))],
        out_specs=pl.BlockSpec((t, LANE), lambda i: (i, 0)),
        out_shape=jax.ShapeDtypeStruct((l, LANE), F32),
        compiler_params=_params(dimension_semantics=("parallel",)),
        name="rwkv_vlora",
    )(p, p, mu8, v1p)


RT = 256


def _rwkv_prep_kernel(*refs, has_vmix):
    (r_ref, k_ref, v_ref, g_ref, wd_ref, ad_ref,
     rh_ref, kh_ref, vh_ref, gh_ref, wdh_ref, adh_ref,
     pa_ref, pb_ref, pl_ref, w2_ref, a2_ref) = refs[:17]
    pos = 17
    if has_vmix:
        tl_ref, v2_ref, vf_ref = refs[pos:pos + 3]
        pos += 3
    (atp_ref, rp_ref, bh_ref, kh2_ref, vb_ref,
     u0_ref, y0_ref, bonus_ref, gs_ref, dl_ref, vout_ref) = refs[pos:]

    first = pl.program_id(1) == 0
    t = r_ref.shape[0]
    nchunk = t // RCHUNK

    mu_r, mu_k, mu_v, mu_g = (pa_ref[j:j + 1, :] for j in range(4))
    w0, a0, k_k, k_a = (pa_ref[j:j + 1, :] for j in range(4, 8))
    r_k, v0 = pb_ref[0:1, :], pb_ref[3:4, :]
    mu_wd, mu_ad = pl_ref[0:1, :], pl_ref[1:2, :]

    r = _shift_lerp(r_ref[...], rh_ref, mu_r, first)
    k = _shift_lerp(k_ref[...], kh_ref, mu_k, first)
    v = _shift_lerp(v_ref[...], vh_ref, mu_v, first)
    g = _shift_lerp(g_ref[...], gh_ref, mu_g, first)
    wd = _shift_lerp(wd_ref[...], wdh_ref, mu_wd, first)
    ad = _shift_lerp(ad_ref[...], adh_ref, mu_ad, first)

    w_log = _log_sigmoid(w0 + _dot3(jnp.tanh(wd), w2_ref[...])) - 0.5
    logd = -jnp.exp(w_log)
    a = _sigmoid(a0 + _dot3(ad, a2_ref[...]))
    if has_vmix:
        v = v + (vf_ref[...] - v) * _sigmoid(v0 + _dot3(tl_ref[...], v2_ref[...]))
    vout_ref[...] = v

    hm = _head_ones(LANE, RH)
    kk = k * k_k
    ss = _dot_sel(kk * kk, hm)
    kk = kk * lax.rsqrt(jnp.maximum(ss, 1e-24))
    kh = k * (1.0 + (a - 1.0) * k_a)
    av = -kk
    bv = kk * a

    bonus_ref[...] = _dot_sel(r * kh * r_k, hm) * v
    gs_ref[...] = _silu(g)

    cum = _seg_cumsum(logd, RCHUNK)
    cl = _chunk_row(cum, RCHUNK, RCHUNK - 1)
    e_n = jnp.exp(-cum)
    e_l = jnp.exp(cl - cum)
    at = av * jnp.exp(cum - logd)
    rt = r * jnp.exp(cum)
    bt = bv * e_n
    kt = kh * e_n
    bh_ref[...] = (bv * e_l).astype(BF16)
    kh2_ref[...] = (kh * e_l).astype(BF16)
    vb_ref[...] = v.astype(BF16)
    dl_ref[0] = jnp.exp(cl).reshape(nchunk, RCHUNK, LANE)[:, 0:SUB, :].reshape(nchunk * SUB, LANE)

    ri = _iota((t, t), 0)
    ci = _iota((t, t), 1)
    same = _idiv(ri, RCHUNK) == _idiv(ci, RCHUNK)
    strict = same & (ci < ri)
    incl = same & (ci <= ri)
    eye = jnp.where(ri == ci, 1.0, 0.0)
    lane = _iota((t, LANE), 1)

    rhs = jnp.concatenate([bt, kt], axis=0).astype(BF16)
    atp = jnp.zeros((t, LANE), F32)
    rp = jnp.zeros((t, LANE), F32)
    u0 = jnp.zeros((t, LANE), F32)
    y0 = jnp.zeros((t, LANE), F32)
    for h in range(LANE // RH):
        mh = _idiv(lane, RH) == h
        atm = jnp.where(mh, at, 0.0)
        rtm = jnp.where(mh, rt, 0.0)
        vm = jnp.where(mh, v, 0.0)
        lhs = jnp.concatenate([atm, rtm], axis=0).astype(BF16)
        amat = _dg(lhs, rhs, NT)
        a_ab = jnp.where(strict, amat[:t, :t], 0.0)
        a_ak = jnp.where(strict, amat[:t, t:], 0.0)
        a_rb = jnp.where(incl, amat[t:, :t], 0.0)
        a_rk = jnp.where(incl, amat[t:, t:], 0.0)
        pw = a_ab
        inv = eye + a_ab
        s = 2
        while s < RCHUNK:
            pw = _bdot(pw, pw)
            inv = inv + _bdot(inv, pw)
            s *= 2
        akv = _bdot(a_ak, vm)
        x = _bdot(inv, jnp.concatenate([atm, akv], axis=1))
        z = _bdot(a_rb, x)
        atp = atp + x[:, :LANE]
        u0 = u0 + x[:, LANE:]
        rp = rp + rtm + z[:, :LANE]
        y0 = y0 + z[:, LANE:] + _bdot(a_rk, vm)
    atp_ref[...] = atp.astype(BF16)
    rp_ref[...] = rp.astype(BF16)
    u0_ref[...] = u0
    y0_ref[...] = y0


def _rwkv_prep(p, pa, pb, plora, w2p, a2p, vmix):
    l = p.shape[0]
    t = min(RT, l)
    hb = t // SUB
    npair = RW // LANE
    has_vmix = vmix is not None

    def main(off):
        cb = off // LANE
        return pl.BlockSpec((t, LANE), lambda q, i: (i, cb + q))

    def halo(off):
        cb = off // LANE
        return pl.BlockSpec((SUB, LANE), lambda q, i: (jnp.maximum(i * hb - 1, 0), cb + q))

    def main1(off):
        cb = off // LANE
        return pl.BlockSpec((t, LANE), lambda q, i: (i, cb))

    def halo1(off):
        cb = off // LANE
        return pl.BlockSpec((SUB, LANE), lambda q, i: (jnp.maximum(i * hb - 1, 0), cb))

    par = pl.BlockSpec((SUB, LANE), lambda q, i: (0, q))
    par0 = pl.BlockSpec((SUB, LANE), lambda q, i: (0, 0))
    lora = pl.BlockSpec((LANE, LANE), lambda q, i: (0, q))
    row = pl.BlockSpec((t, LANE), lambda q, i: (i, q))

    in_specs = [main(E_R), main(E_K), main(E_V), main(E_G), main1(E_WD), main1(E_AD),
                halo(E_R), halo(E_K), halo(E_V), halo(E_G), halo1(E_WD), halo1(E_AD),
                par, par, par0, lora, lora]
    args = [p] * 12 + [pa, pb, plora, w2p, a2p]
    if has_vmix:
        tl, v2p, vf = vmix
        in_specs += [pl.BlockSpec((t, LANE), lambda q, i: (i, 0)), lora, row]
        args += [tl, v2p, vf]
    dl_rows = (t // RCHUNK) * SUB
    out_specs = [row] * 9 + [pl.BlockSpec((1, dl_rows, LANE), lambda q, i: (i, 0, q)), row]
    sd = jax.ShapeDtypeStruct
    out_shape = ([sd((l, RW), BF16)] * 5 + [sd((l, RW), F32)] * 4
                 + [sd((l // t, dl_rows, RW), F32), sd((l, RW), F32)])
    return pl.pallas_call(
        functools.partial(_rwkv_prep_kernel, has_vmix=has_vmix),
        grid=(npair, l // t),
        in_specs=in_specs,
        out_specs=out_specs,
        out_shape=out_shape,
        compiler_params=_params(dimension_semantics=("parallel", "parallel")),
        name="rwkv_prep",
    )(*args)


def _rwkv_scan_kernel(atp_ref, rp_ref, bh_ref, kh_ref, vb_ref, u0_ref, y0_ref,
                      bonus_ref, gs_ref, dl_ref, pb_ref, o_ref, s_ref):
    @pl.when(pl.program_id(0) == 0)
    def _():
        s_ref[...] = jnp.zeros_like(s_ref)

    t = atp_ref.shape[0]
    npair = RW // LANE
    bd = _idiv(_iota((LANE, LANE), 0), RH) == _idiv(_iota((LANE, LANE), 1), RH)
    hmean = _head_ones(LANE, RH, 1.0 / RH)

    def chunk(c, carry):
        rows = pl.ds(pl.multiple_of(c * RCHUNK, RCHUNK), RCHUNK)
        for q in range(npair):
            sl = slice(q * LANE, (q + 1) * LANE)
            s_q = s_ref[q]
            x = jnp.concatenate([atp_ref[rows, sl], rp_ref[rows, sl]], axis=0)
            o = _dg(x, s_q.astype(BF16), NT)
            u = o[:RCHUNK] + u0_ref[rows, sl]
            y = o[RCHUNK:] + y0_ref[rows, sl]
            uv = jnp.concatenate([u.astype(BF16), vb_ref[rows, sl]], axis=0)
            bk = jnp.concatenate([bh_ref[rows, sl], kh_ref[rows, sl]], axis=0)
            ds = _dg(uv, bk, TN)
            dl = dl_ref[0, pl.ds(pl.multiple_of(c * SUB, SUB), SUB), sl][0:1, :]
            s_ref[q] = s_q * dl + jnp.where(bd, ds, 0.0)
            mean = _dot_sel(y, hmean)
            d = y - mean
            var = _dot_sel(d * d, hmean)
            yn = d * lax.rsqrt(var + RWKV_GN_EPS) * pb_ref[1:2, sl] + pb_ref[2:3, sl]
            o_ref[rows, sl] = ((yn + bonus_ref[rows, sl]) * gs_ref[rows, sl]).astype(BF16)
        return carry

    lax.fori_loop(0, t // RCHUNK, chunk, 0)


def _rwkv_scan(atp, rp, bh, kh, vb, u0, y0, bonus, gs, dl, pb):
    l = atp.shape[0]
    t = min(RT, l)
    row = pl.BlockSpec((t, RW), lambda i: (i, 0))
    return pl.pallas_call(
        _rwkv_scan_kernel,
        grid=(l // t,),
        in_specs=[row] * 9 + [pl.BlockSpec((1, (t // RCHUNK) * SUB, RW), lambda i: (i, 0, 0)),
                              pl.BlockSpec((SUB, RW), lambda i: (0, 0))],
        out_specs=row,
        out_shape=jax.ShapeDtypeStruct((l, RW), BF16),
        scratch_shapes=[pltpu.VMEM((RW // LANE, LANE, LANE), F32)],
        compiler_params=_params(dimension_semantics=("arbitrary",)),
        name="rwkv_scan",
    )(atp, rp, bh, kh, vb, u0, y0, bonus, gs, dl, pb)


def _ssd_conv(main_ref, halo_ref, wb_ref, buf_ref, first):
    q = main_ref.shape[0]
    buf_ref[0:SUB, :] = jnp.where(first, 0.0, halo_ref[...])
    buf_ref[SUB:SUB + q, :] = main_ref[...]
    acc = wb_ref[SCONV:SCONV + 1, :]
    for j in range(SCONV):
        acc = acc + buf_ref[pl.ds(SUB - (SCONV - 1) + j, q), :] * wb_ref[j:j + 1, :]
    return _silu(acc)


def _ssd_kernel(z_ref, x_ref, b_ref, c_ref, dt_ref, xh_ref, bhalo_ref, chalo_ref,
                cwx_ref, cwb_ref, cwc_ref, hp_ref, dx_ref, nw_ref,
                o_ref, prev_ref, xbuf_ref, bbuf_ref, cbuf_ref, ybuf_ref):
    g = pl.program_id(0)
    first = pl.program_id(1) == 0
    q = z_ref.shape[0]

    @pl.when(first)
    def _():
        prev_ref[...] = jnp.zeros_like(prev_ref)

    xs = _ssd_conv(x_ref, xh_ref, cwx_ref, xbuf_ref, first)
    bm = _ssd_conv(b_ref, bhalo_ref, cwb_ref, bbuf_ref, first)
    cm = _ssd_conv(c_ref, chalo_ref, cwc_ref, cbuf_ref, first)

    sl_r = _iota((LANE, LANE), 0)
    sl_c = _iota((LANE, LANE), 1)
    sel = jnp.where((sl_r == sl_c + g * SHG) & (sl_c < SHG), 1.0, 0.0).astype(BF16)
    dt_raw = _dot_sel(dt_ref[...], sel)
    dt = _softplus(dt_raw + hp_ref[0:1, :])
    a_neg = -jnp.exp(hp_ref[1:2, :])
    acs = _seg_cumsum(dt * a_neg, q)
    acs_t = acs.T
    acs_last = acs[q - 1:q, :]

    expand = jnp.where(_iota((LANE, SGW), 0) == _idiv(_iota((LANE, SGW), 1), SHEAD), 1.0, 0.0).astype(BF16)
    expand_t = jnp.where(_iota((SGW, LANE), 1) == _idiv(_iota((SGW, LANE), 0), SHEAD), 1.0, 0.0).astype(BF16)
    dt_e = _dot_sel(dt, expand)
    eacs_e = _dot_sel(jnp.exp(acs), expand)
    ds_e = _dot_sel(jnp.exp(acs_last - acs), expand)
    xc = xs * dt_e

    cmb = cm.astype(BF16)
    bmb = bm.astype(BF16)
    scores = _dg(cmb, bmb, NT)
    prev = prev_ref[...]
    y_off = _dg(cmb, prev.astype(BF16), NT) * eacs_e
    states = _dg((xc * ds_e).astype(BF16), bmb, TN)
    eh, em, el = _split3(jnp.exp(acs_t))
    cdec = _dg(expand_t, eh) + (_dg(expand_t, em) + _dg(expand_t, el))
    prev_ref[...] = prev * cdec[:, q - 1:q] + states

    causal = _iota((q, q), 1) <= _iota((q, q), 0)
    for h in range(SHG):
        diff = acs[:, h:h + 1] - acs_t[h:h + 1, :]
        lm = jnp.exp(jnp.where(causal, diff, -jnp.inf))
        ybuf_ref[:, h * SHEAD:(h + 1) * SHEAD] = _bdot(scores * lm, xc[:, h * SHEAD:(h + 1) * SHEAD])
    y = ybuf_ref[...] + y_off + dx_ref[0:1, :] * xs
    y = y * _silu(z_ref[...])
    ms = jnp.mean(y * y, axis=-1, keepdims=True)
    o_ref[...] = (y * lax.rsqrt(ms + EPS) * nw_ref[0:1, :]).astype(BF16)


def _ssd(p, cw8, hp, dx8, nw8):
    l = p.shape[0]
    q = SCHUNK
    hb = q // SUB

    def halo_idx(i):
        return jnp.maximum(i * hb - 1, 0)

    zb, xb = E_Z // SGW, E_X // SGW
    bb, cb, db = E_B // LANE, E_C // LANE, E_DT // LANE
    in_specs = [
        pl.BlockSpec((q, SGW), lambda g, i: (i, zb + g)),
        pl.BlockSpec((q, SGW), lambda g, i: (i, xb + g)),
        pl.BlockSpec((q, LANE), lambda g, i: (i, bb + g)),
        pl.BlockSpec((q, LANE), lambda g, i: (i, cb + g)),
        pl.BlockSpec((q, LANE), lambda g, i: (i, db)),
        pl.BlockSpec((SUB, SGW), lambda g, i: (halo_idx(i), xb + g)),
        pl.BlockSpec((SUB, LANE), lambda g, i: (halo_idx(i), bb + g)),
        pl.BlockSpec((SUB, LANE), lambda g, i: (halo_idx(i), cb + g)),
        pl.BlockSpec((SUB, SGW), lambda g, i: (0, g)),
        pl.BlockSpec((SUB, LANE), lambda g, i: (0, SW // LANE + g)),
        pl.BlockSpec((SUB, LANE), lambda g, i: (0, SW // LANE + SGROUPS + g)),
        pl.BlockSpec((SUB, LANE), lambda g, i: (0, g)),
        pl.BlockSpec((SUB, SGW), lambda g, i: (0, g)),
        pl.BlockSpec((SUB, SGW), lambda g, i: (0, g)),
    ]
    return pl.pallas_call(
        _ssd_kernel,
        grid=(SGROUPS, l // q),
        in_specs=in_specs,
        out_specs=pl.BlockSpec((q, SGW), lambda g, i: (i, g)),
        out_shape=jax.ShapeDtypeStruct((l, SW), BF16),
        scratch_shapes=[pltpu.VMEM((SGW, SSTATE), F32),
                        pltpu.VMEM((q + SUB, SGW), F32),
                        pltpu.VMEM((q + SUB, LANE), F32),
                        pltpu.VMEM((q + SUB, LANE), F32),
                        pltpu.VMEM((q, SGW), F32)],
        compiler_params=_params(dimension_semantics=("parallel", "arbitrary")),
        name="ssd",
    )(p, p, p, p, p, p, p, p, cw8, cw8, cw8, hp, dx8, nw8)


GT = 256


def _gla_kernel(q_ref, k_ref, v_ref, g_ref, gd_ref, gkw_ref, gkb_ref, nw_ref, o_ref, st_ref, oi_ref):
    @pl.when(pl.program_id(1) == 0)
    def _():
        st_ref[...] = jnp.zeros_like(st_ref)

    t = q_ref.shape[0]
    nchunk = t // GCHUNK
    gk = _log_sigmoid(_dot3(gd_ref[...], gkw_ref[...]) + gkb_ref[0:1, :]) * (1.0 / 16.0)
    bc = _seg_cumsum(gk, GCHUNK)
    mid = _chunk_row(bc, GCHUNK, GCHUNK // 2)
    last = _chunk_row(bc, GCHUNK, GCHUNK - 1)
    qs = q_ref[...] * (GHK ** -0.5)
    k = k_ref[...]
    vb = v_ref[...].astype(BF16)

    ri = _iota((t, t), 0)
    ci = _iota((t, t), 1)
    incl = (_idiv(ri, GCHUNK) == _idiv(ci, GCHUNK)) & (ci <= ri)
    attn = _bdot(qs * jnp.exp(bc - mid), k * jnp.exp(mid - bc), NT)
    attn = jnp.where(incl, attn, 0.0)
    o_intra = _bdot(attn, vb)

    q_in = (qs * jnp.exp(bc)).astype(BF16)
    k_st = (k * jnp.exp(last - bc)).astype(BF16)
    dlast = jnp.exp(last)
    for c in range(nchunk):
        rows = slice(c * GCHUNK, (c + 1) * GCHUNK)
        st = st_ref[...]
        oi_ref[rows, :] = _dg(q_in[rows], st.astype(BF16), NT)
        st_ref[...] = st * dlast[c * GCHUNK:c * GCHUNK + 1, :] + _dg(vb[rows], k_st[rows], TN)
    o = o_intra + oi_ref[...]
    ms = jnp.mean(o * o, axis=-1, keepdims=True)
    o = o * lax.rsqrt(ms + EPS) * nw_ref[0:1, :]
    o_ref[...] = (o * _silu(g_ref[...])).astype(BF16)


def _gla(p, gkwp, gkb8, nw8):
    l = p.shape[0]
    t = min(GT, l)
    return pl.pallas_call(
        _gla_kernel,
        grid=(GHEADS, l // t),
        in_specs=[pl.BlockSpec((t, GHK), lambda h, i: (i, O_Q // GHK + h)),
                  pl.BlockSpec((t, GHK), lambda h, i: (i, O_K // GHK + h)),
                  pl.BlockSpec((t, GHV), lambda h, i: (i, O_V // GHV + h)),
                  pl.BlockSpec((t, GHV), lambda h, i: (i, O_G // GHV + h)),
                  pl.BlockSpec((t, LANE), lambda h, i: (i, O_GD // LANE)),
                  pl.BlockSpec((LANE, GHK), lambda h, i: (0, h)),
                  pl.BlockSpec((SUB, GHK), lambda h, i: (0, h)),
                  pl.BlockSpec((SUB, GHV), lambda h, i: (0, 0))],
        out_specs=pl.BlockSpec((t, GHV), lambda h, i: (i, h)),
        out_shape=jax.ShapeDtypeStruct((l, GVAL), BF16),
        scratch_shapes=[pltpu.VMEM((GHV, GHK), F32), pltpu.VMEM((t, GHV), F32)],
        compiler_params=_params(dimension_semantics=("parallel", "arbitrary")),
        name="gla",
    )(p, p, p, p, p, gkwp, gkb8, nw8)


def _rows8(*rows):
    n = rows[0].shape[0]
    m = jnp.stack([r.astype(F32) for r in rows])
    return jnp.concatenate([m, jnp.zeros((SUB - len(rows), n), F32)], axis=0)


def _pad_cols(w, n):
    return jnp.concatenate([w, jnp.zeros((w.shape[0], n - w.shape[1]), w.dtype)], axis=1)


def _pad_rows(w, n):
    return jnp.concatenate([w, jnp.zeros((n - w.shape[0], w.shape[1]), w.dtype)], axis=0)


def _even_weight(w):
    rwkv_main = w[:, :4 * RW]
    wd = w[:, 4 * RW:4 * RW + DECAY_LORA]
    ad = w[:, 4 * RW + DECAY_LORA:4 * RW + DECAY_LORA + AAA_LORA]
    s0 = 4 * RW + DECAY_LORA + AAA_LORA
    z = w[:, s0:s0 + SW]
    xbc = w[:, s0 + SW:s0 + SW + SW + 2 * SGROUPS * SSTATE]
    dt = w[:, s0 + 2 * SW + 2 * SGROUPS * SSTATE:]
    parts = [rwkv_main, z, xbc, _pad_cols(wd, LANE), _pad_cols(ad, LANE), _pad_cols(dt, LANE),
             jnp.zeros((w.shape[0], LANE), w.dtype)]
    return jnp.concatenate(parts, axis=1).astype(BF16)


def _odd_weight(w):
    main = w[:, :2 * GKEY + 2 * GVAL]
    gd = w[:, 2 * GKEY + 2 * GVAL:]
    return jnp.concatenate([main, _pad_cols(gd, LANE)], axis=1).astype(BF16)


def kernel(x, norm_w, final_norm_w, w_in_even, w_out_even, rwkv_mu, rwkv_w0, rwkv_w2, rwkv_a0, rwkv_a2, rwkv_k_k, rwkv_k_a, rwkv_r_k, rwkv_ln_w, rwkv_ln_b, rwkv_v0, rwkv_v1, rwkv_v2, ssm_conv_w, ssm_conv_b, ssm_dt_bias, ssm_A_log, ssm_D, ssm_norm_w, w_in_odd, w_out_odd, gla_gk_w, gla_gk_b, gla_norm_w):
    bsz, l, d = x.shape
    depth = norm_w.shape[0]
    outs = []
    for b in range(bsz):
        res = x[b]
        h = _rmsnorm(res, _rows8(norm_w[0]))
        v_first = None
        for layer in range(depth):
            i = layer // 2
            last = layer == depth - 1
            nw_next = _rows8(final_norm_w if last else norm_w[layer + 1])
            if layer % 2 == 0:
                p = _inproj(h, _even_weight(w_in_even[i]), tn=512)
                mu = rwkv_mu[i]
                pa = _rows8(mu[0:RW], mu[RW:2 * RW], mu[2 * RW:3 * RW], mu[3 * RW:4 * RW],
                            rwkv_w0[i], rwkv_a0[i], rwkv_k_k[i], rwkv_k_a[i])
                v0 = rwkv_v0[i - 1] if i > 0 else jnp.zeros((RW,), F32)
                pb = _rows8(rwkv_r_k[i].reshape(RW), rwkv_ln_w[i], rwkv_ln_b[i], v0)
                zpad = jnp.zeros((LANE - DECAY_LORA,), F32)
                plora = _rows8(jnp.concatenate([mu[4 * RW:4 * RW + DECAY_LORA], zpad]),
                               jnp.concatenate([mu[4 * RW + DECAY_LORA:], zpad]))
                w2p = _pad_rows(rwkv_w2[i], LANE)
                a2p = _pad_rows(rwkv_a2[i], LANE)
                if i == 0:
                    vmix = None
                else:
                    tl = _vlora(p, _rows8(mu[2 * RW:3 * RW]), _pad_cols(rwkv_v1[i - 1], LANE))
                    vmix = (tl, _pad_rows(rwkv_v2[i - 1], LANE), v_first)
                (atp, rp, bh, kh, vb, u0, y0, bonus, gs, dl, v_out) = _rwkv_prep(
                    p, pa, pb, plora, w2p, a2p, vmix)
                if i == 0:
                    v_first = v_out
                y_a = _rwkv_scan(atp, rp, bh, kh, vb, u0, y0, bonus, gs, dl, pb)

                cw8 = jnp.concatenate([ssm_conv_w[i], ssm_conv_b[i][None, :],
                                       jnp.zeros((SUB - SCONV - 1, ssm_conv_w.shape[2]), F32)], axis=0)
                pad_g = lambda a: _pad_cols(a.reshape(SGROUPS, SHG), LANE).reshape(SGROUPS * LANE)
                hp = _rows8(pad_g(ssm_dt_bias[i]), pad_g(ssm_A_log[i]))
                dx8 = _rows8(jnp.repeat(ssm_D[i], SHEAD))
                y_b = _ssd(p, cw8, hp, dx8, _rows8(ssm_norm_w[i]))
                w_out = w_out_even[i].astype(BF16)
                o = _outproj([y_a, y_b], w_out, res, nw_next, last)
            else:
                p = _inproj(h, _odd_weight(w_in_odd[i]), tn=896)
                y = _gla(p, _pad_rows(gla_gk_w[i], LANE), _rows8(gla_gk_b[i]), _rows8(gla_norm_w[i]))
                o = _outproj([y], w_out_odd[i].astype(BF16), res, nw_next, last)
            if last:
                res = o[0]
            else:
                res, h = o
        outs.append(res)
    return jnp.stack(outs).astype(x.dtype)
```

```python
import functools

import jax
import jax.numpy as jnp
from jax import lax
from jax.experimental import pallas as pl
from jax.experimental.pallas import tpu as pltpu

F32 = jnp.float32
BF16 = jnp.bfloat16

D_MODEL = 2048
EPS = 1e-5
RW = 2048
RH = 64
DECAY_LORA = 96
AAA_LORA = 96
MV_LORA = 64
RWKV_GN_EPS = 64e-5
RCHUNK = 64
SW = 2048
SHEAD = 64
SGROUPS = 4
SSTATE = 128
SCONV = 4
SCHUNK = 128
SGW = SW // SGROUPS
SHG = SGW // SHEAD
GHEADS = 4
GKEY = 1024
GVAL = 2048
GHK = 256
GHV = 512
GRANK = 16
GCHUNK = 64

LANE = 128
SUB = 8
VMEM_LIMIT = 56 * 1024 * 1024

E_R, E_K, E_V, E_G = 0, 2048, 4096, 6144
E_Z, E_X = 8192, 10240
E_B, E_C = 12288, 12800
E_WD, E_AD, E_DT = 13312, 13440, 13568
E_COLS = 13824
O_Q, O_K, O_V, O_G, O_GD = 0, 1024, 2048, 4096, 6144
O_COLS = 6272

NN = (((1,), (0,)), ((), ()))
NT = (((1,), (1,)), ((), ()))
TN = (((0,), (0,)), ((), ()))


def _dg(a, b, dims=NN):
    return lax.dot_general(a, b, dims, preferred_element_type=F32)


def _bdot(a, b, dims=NN):
    return _dg(a.astype(BF16), b.astype(BF16), dims)


def _split2(x):
    hi = x.astype(BF16)
    lo = (x - hi.astype(F32)).astype(BF16)
    return hi, lo


def _split3(x):
    hi = x.astype(BF16)
    r = x - hi.astype(F32)
    mid = r.astype(BF16)
    lo = (r - mid.astype(F32)).astype(BF16)
    return hi, mid, lo


def _dot3(a, b, dims=NN):
    ah, al = _split2(a)
    bh, bl = _split2(b)
    return _dg(ah, bh, dims) + (_dg(ah, bl, dims) + _dg(al, bh, dims))


def _dot_sel(a, sel_bf16, dims=NN):
    ah, am, al = _split3(a)
    return _dg(ah, sel_bf16, dims) + (_dg(am, sel_bf16, dims) + _dg(al, sel_bf16, dims))


def _sigmoid(x):
    return 1.0 / (1.0 + jnp.exp(-x))


def _silu(x):
    return x * _sigmoid(x)


def _log_sigmoid(x):
    return jnp.minimum(x, 0.0) - jnp.log(1.0 + jnp.exp(-jnp.abs(x)))


def _softplus(x):
    return jnp.maximum(x, 0.0) + jnp.log(1.0 + jnp.exp(-jnp.abs(x)))


def _iota(shape, axis):
    return lax.broadcasted_iota(jnp.int32, shape, axis)


def _idiv(x, n):
    return lax.shift_right_logical(x, n.bit_length() - 1)


def _seg_cumsum(x, chunk):
    rin = _iota(x.shape, 0) & (chunk - 1)
    s = 1
    while s < chunk:
        x = x + jnp.where(rin >= s, pltpu.roll(x, s, 0), 0.0)
        s *= 2
    return x


def _chunk_row(x, chunk, r):
    t, w = x.shape
    x3 = x.reshape(t // chunk, chunk, w)
    return jnp.broadcast_to(x3[:, r:r + 1, :], x3.shape).reshape(t, w)


def _head_ones(n, head, scale=1.0):
    r = _idiv(_iota((n, n), 0), head)
    c = _idiv(_iota((n, n), 1), head)
    return jnp.where(r == c, scale, 0.0).astype(BF16)


def _params(**kw):
    return pltpu.CompilerParams(vmem_limit_bytes=VMEM_LIMIT, **kw)


def _rmsnorm_kernel(x_ref, w_ref, o_ref):
    x = x_ref[...]
    ms = jnp.mean(x * x, axis=-1, keepdims=True)
    o_ref[...] = (x * lax.rsqrt(ms + EPS) * w_ref[0:1, :]).astype(BF16)


def _rmsnorm(x, w8, tm=512):
    l, d = x.shape
    return pl.pallas_call(
        _rmsnorm_kernel,
        grid=(l // tm,),
        in_specs=[pl.BlockSpec((tm, d), lambda i: (i, 0)),
                  pl.BlockSpec((SUB, d), lambda i: (0, 0))],
        out_specs=pl.BlockSpec((tm, d), lambda i: (i, 0)),
        out_shape=jax.ShapeDtypeStruct((l, d), BF16),
        compiler_params=_params(dimension_semantics=("parallel",)),
        name="rmsnorm",
    )(x, w8)


def _inproj_kernel(h_ref, w_ref, o_ref):
    o_ref[...] = jnp.dot(h_ref[...], w_ref[...], preferred_element_type=F32)


def _inproj(h, w, tn, tm=1024):
    l, d = h.shape
    n = w.shape[1]
    tm = min(tm, l)
    return pl.pallas_call(
        _inproj_kernel,
        grid=(l // tm, n // tn),
        in_specs=[pl.BlockSpec((tm, d), lambda i, j: (i, 0)),
                  pl.BlockSpec((d, tn), lambda i, j: (0, j))],
        out_specs=pl.BlockSpec((tm, tn), lambda i, j: (i, j)),
        out_shape=jax.ShapeDtypeStruct((l, n), F32),
        compiler_params=_params(dimension_semantics=("parallel", "parallel")),
        name="inproj",
    )(h, w)


def _outproj_kernel(*refs, n_in, kt_per_in, final):
    y_refs = refs[:n_in]
    w_ref, res_ref, nw_ref = refs[n_in:n_in + 3]
    outs = refs[n_in + 3:]
    acc_ref = outs[-1]
    k = pl.program_id(1)
    nk = pl.num_programs(1)

    @pl.when(k == 0)
    def _():
        acc_ref[...] = res_ref[...]

    for s in range(n_in):
        @pl.when((k >= s * kt_per_in) & (k < (s + 1) * kt_per_in))
        def _(s=s):
            acc_ref[...] += jnp.dot(y_refs[s][...], w_ref[...], preferred_element_type=F32)

    @pl.when(k == nk - 1)
    def _():
        r = acc_ref[...]
        ms = jnp.mean(r * r, axis=-1, keepdims=True)
        hn = r * lax.rsqrt(ms + EPS) * nw_ref[0:1, :]
        if final:
            outs[0][...] = hn
        else:
            outs[0][...] = r
            outs[1][...] = hn.astype(BF16)


def _outproj(ys, w, res, nw8, final, tm=512, tk=1024):
    l, d = res.shape
    n_in = len(ys)
    kt_per_in = ys[0].shape[1] // tk
    nk = n_in * kt_per_in
    tm = min(tm, l)

    def y_map(s):
        return lambda i, k: (i, jnp.clip(k - s * kt_per_in, 0, kt_per_in - 1))

    in_specs = [pl.BlockSpec((tm, tk), y_map(s)) for s in range(n_in)]
    in_specs += [pl.BlockSpec((tk, d), lambda i, k: (k, 0)),
                 pl.BlockSpec((tm, d), lambda i, k: (i, 0)),
                 pl.BlockSpec((SUB, d), lambda i, k: (0, 0))]
    row_spec = pl.BlockSpec((tm, d), lambda i, k: (i, 0))
    if final:
        out_specs = [row_spec]
        out_shape = [jax.ShapeDtypeStruct((l, d), F32)]
    else:
        out_specs = [row_spec, row_spec]
        out_shape = [jax.ShapeDtypeStruct((l, d), F32), jax.ShapeDtypeStruct((l, d), BF16)]
    return pl.pallas_call(
        functools.partial(_outproj_kernel, n_in=n_in, kt_per_in=kt_per_in, final=final),
        grid=(l // tm, nk),
        in_specs=in_specs,
        out_specs=out_specs,
        out_shape=out_shape,
        scratch_shapes=[pltpu.VMEM((tm, d), F32)],
        compiler_params=_params(dimension_semantics=("parallel", "arbitrary")),
        name="outproj",
    )(*ys, w, res, nw8)


def _shift_lerp(x, halo_ref, mu, first):
    prev_last = jnp.where(first, 0.0, halo_ref[SUB - 1:SUB, :])
    xs = pltpu.roll(x, 1, 0)
    xprev = jnp.where(_iota(x.shape, 0) == 0, prev_last, xs)
    return x + (xprev - x) * mu


def _vlora_kernel(v_ref, vh_ref, mu_ref, v1_ref, o_ref):
    first = pl.program_id(0) == 0
    v = _shift_lerp(v_ref[...], vh_ref, mu_ref[0:1, :], first)
    o_ref[...] = _dot3(v, v1_ref[...])


def _vlora(p, mu8, v1p, t=256):
    l = p.shape[0]
    t = min(t, l)
    hb = t // SUB
    cb = E_V // RW
    return pl.pallas_call(
        _vlora_kernel,
        grid=(l // t,),
        in_specs=[pl.BlockSpec((t, RW), lambda i: (i, cb)),
                  pl.BlockSpec((SUB, RW), lambda i: (jnp.maximum(i * hb - 1, 0), cb)),
                  pl.BlockSpec((SUB, RW), lambda i: (0, 0)),
                  pl.BlockSpec((RW, LANE), lambda i: (0, 0))],
        out_specs=pl.BlockSpec((t, LANE), lambda i: (i, 0)),
        out_shape=jax.ShapeDtypeStruct((l, LANE), F32),
        compiler_params=_params(dimension_semantics=("parallel",)),
        name="rwkv_vlora",
    )(p, p, mu8, v1p)


RT = 512
RST = 256


def _rwkv_prep_kernel(*refs, has_vmix):
    (r_ref, k_ref, v_ref, g_ref, wd_ref, ad_ref,
     rh_ref, kh_ref, vh_ref, gh_ref, wdh_ref, adh_ref,
     pa_ref, pb_ref, pl_ref, w2_ref, a2_ref) = refs[:17]
    pos = 17
    if has_vmix:
        tl_ref, v2_ref, vf_ref = refs[pos:pos + 3]
        pos += 3
    (atp_ref, rp_ref, bh_ref, kh2_ref, vb_ref,
     u0_ref, y0_ref, bonus_ref, gs_ref, dl_ref, vout_ref) = refs[pos:]

    first = pl.program_id(1) == 0
    t = r_ref.shape[0]
    nchunk = t // RCHUNK

    mu_r, mu_k, mu_v, mu_g = (pa_ref[j:j + 1, :] for j in range(4))
    w0, a0, k_k, k_a = (pa_ref[j:j + 1, :] for j in range(4, 8))
    r_k, v0 = pb_ref[0:1, :], pb_ref[3:4, :]
    mu_wd, mu_ad = pl_ref[0:1, :], pl_ref[1:2, :]

    r = _shift_lerp(r_ref[...], rh_ref, mu_r, first)
    k = _shift_lerp(k_ref[...], kh_ref, mu_k, first)
    v = _shift_lerp(v_ref[...], vh_ref, mu_v, first)
    g = _shift_lerp(g_ref[...], gh_ref, mu_g, first)
    wd = _shift_lerp(wd_ref[...], wdh_ref, mu_wd, first)
    ad = _shift_lerp(ad_ref[...], adh_ref, mu_ad, first)

    w_log = _log_sigmoid(w0 + _dot3(jnp.tanh(wd), w2_ref[...])) - 0.5
    logd = -jnp.exp(w_log)
    a = _sigmoid(a0 + _dot3(ad, a2_ref[...]))
    if has_vmix:
        v = v + (vf_ref[...] - v) * _sigmoid(v0 + _dot3(tl_ref[...], v2_ref[...]))
    vout_ref[...] = v

    hm = _head_ones(LANE, RH)
    kk = k * k_k
    ss = _dot_sel(kk * kk, hm)
    kk = kk * lax.rsqrt(jnp.maximum(ss, 1e-24))
    kh = k * (1.0 + (a - 1.0) * k_a)
    av = -kk
    bv = kk * a

    bonus_ref[...] = _dot_sel(r * kh * r_k, hm) * v
    gs_ref[...] = _silu(g)

    cum = _seg_cumsum(logd, RCHUNK)
    cl = _chunk_row(cum, RCHUNK, RCHUNK - 1)
    e_n = jnp.exp(-cum)
    e_l = jnp.exp(cl - cum)
    at = av * jnp.exp(cum - logd)
    rt = r * jnp.exp(cum)
    bt = bv * e_n
    kt = kh * e_n
    bh_ref[...] = (bv * e_l).astype(BF16)
    kh2_ref[...] = (kh * e_l).astype(BF16)
    vb_ref[...] = v.astype(BF16)
    dl_ref[...] = jnp.exp(cl).reshape(nchunk, RCHUNK, LANE)[:, 0:SUB, :].reshape(nchunk * SUB, LANE)

    c2 = 2 * RCHUNK
    ri = _iota((c2, c2), 0)
    ci = _iota((c2, c2), 1)
    same = _idiv(ri, RCHUNK) == _idiv(ci, RCHUNK)
    strict = same & (ci < ri)
    incl = same & (ci <= ri)
    eye = jnp.where(ri == ci, 1.0, 0.0)
    head0 = _iota((RCHUNK, LANE), 1) < RH

    def stack(xc):
        return jnp.concatenate([jnp.where(head0, xc, 0.0), jnp.where(head0, 0.0, xc)], axis=0)

    def unstack(xs):
        return xs[:RCHUNK] + xs[RCHUNK:]

    chunks = range(nchunk)
    rows = [slice(c * RCHUNK, (c + 1) * RCHUNK) for c in chunks]
    atm = [stack(at[rw]).astype(BF16) for rw in rows]
    rtm = [stack(rt[rw]) for rw in rows]
    vm = [stack(v[rw]).astype(BF16) for rw in rows]
    amat = []
    for c in chunks:
        btc = bt[rows[c]].astype(BF16)
        ktc = kt[rows[c]].astype(BF16)
        lhs = jnp.concatenate([atm[c], rtm[c].astype(BF16)], axis=0)
        rhs = jnp.concatenate([btc, btc, ktc, ktc], axis=0)
        amat.append(_dg(lhs, rhs, NT))
    a_ab = [jnp.where(strict, m[:c2, :c2], 0.0) for m in amat]
    a_ak = [jnp.where(strict, m[:c2, c2:], 0.0).astype(BF16) for m in amat]
    a_rb = [jnp.where(incl, m[c2:, :c2], 0.0).astype(BF16) for m in amat]
    a_rk = [jnp.where(incl, m[c2:, c2:], 0.0).astype(BF16) for m in amat]
    akv = [_dg(a_ak[c], vm[c]) for c in chunks]
    arkv = [_dg(a_rk[c], vm[c]) for c in chunks]
    pw = [m.astype(BF16) for m in a_ab]
    inv = [eye + m for m in a_ab]
    s = 2
    while s < RCHUNK:
        pw = [_dg(m, m) for m in pw]
        inv = [inv[c] + _bdot(inv[c], pw[c]) for c in chunks]
        pw = [m.astype(BF16) for m in pw]
        s *= 2
    x = [_bdot(inv[c], jnp.concatenate([atm[c], akv[c].astype(BF16)], axis=1)) for c in chunks]
    z = [_dg(a_rb[c], x[c].astype(BF16)) for c in chunks]
    for c in chunks:
        atp_ref[rows[c], :] = unstack(x[c][:, :LANE]).astype(BF16)
        u0_ref[rows[c], :] = unstack(x[c][:, LANE:])
        rp_ref[rows[c], :] = unstack(rtm[c] + z[c][:, :LANE]).astype(BF16)
        y0_ref[rows[c], :] = unstack(z[c][:, LANE:] + arkv[c])


def _rwkv_prep(p, pa, pb, plora, w2p, a2p, vmix):
    l = p.shape[0]
    t = min(RT, l)
    hb = t // SUB
    npair = RW // LANE
    has_vmix = vmix is not None

    def main(off):
        cb = off // LANE
        return pl.BlockSpec((t, LANE), lambda q, i: (i, cb + q))

    def halo(off):
        cb = off // LANE
        return pl.BlockSpec((SUB, LANE), lambda q, i: (jnp.maximum(i * hb - 1, 0), cb + q))

    def main1(off):
        cb = off // LANE
        return pl.BlockSpec((t, LANE), lambda q, i: (i, cb))

    def halo1(off):
        cb = off // LANE
        return pl.BlockSpec((SUB, LANE), lambda q, i: (jnp.maximum(i * hb - 1, 0), cb))

    par = pl.BlockSpec((SUB, LANE), lambda q, i: (0, q))
    par0 = pl.BlockSpec((SUB, LANE), lambda q, i: (0, 0))
    lora = pl.BlockSpec((LANE, LANE), lambda q, i: (0, q))
    row = pl.BlockSpec((t, LANE), lambda q, i: (i, q))

    in_specs = [main(E_R), main(E_K), main(E_V), main(E_G), main1(E_WD), main1(E_AD),
                halo(E_R), halo(E_K), halo(E_V), halo(E_G), halo1(E_WD), halo1(E_AD),
                par, par, par0, lora, lora]
    args = [p] * 12 + [pa, pb, plora, w2p, a2p]
    if has_vmix:
        tl, v2p, vf = vmix
        in_specs += [pl.BlockSpec((t, LANE), lambda q, i: (i, 0)), lora, row]
        args += [tl, v2p, vf]
    dl_rows = (t // RCHUNK) * SUB
    out_specs = [row] * 9 + [pl.BlockSpec((dl_rows, LANE), lambda q, i: (i, q)), row]
    sd = jax.ShapeDtypeStruct
    out_shape = ([sd((l, RW), BF16)] * 5 + [sd((l, RW), F32)] * 4
                 + [sd((l // RCHUNK * SUB, RW), F32), sd((l, RW), F32)])
    return pl.pallas_call(
        functools.partial(_rwkv_prep_kernel, has_vmix=has_vmix),
        grid=(npair, l // t),
        in_specs=in_specs,
        out_specs=out_specs,
        out_shape=out_shape,
        compiler_params=_params(dimension_semantics=("parallel", "parallel")),
        name="rwkv_prep",
    )(*args)


def _rwkv_scan_kernel(atp_ref, rp_ref, bh_ref, kh_ref, vb_ref, u0_ref, y0_ref,
                      bonus_ref, gs_ref, dl_ref, pb_ref, o_ref, s_ref, y_ref):
    @pl.when(pl.program_id(0) == 0)
    def _():
        s_ref[...] = jnp.zeros_like(s_ref)

    t = atp_ref.shape[0]
    npair = RW // LANE
    bd = _idiv(_iota((LANE, LANE), 0), RH) == _idiv(_iota((LANE, LANE), 1), RH)
    hmean = _head_ones(LANE, RH, 1.0 / RH)

    pairs = range(npair)
    lanes = [slice(q * LANE, (q + 1) * LANE) for q in pairs]
    for c in range(t // RCHUNK):
        rows = slice(c * RCHUNK, (c + 1) * RCHUNK)
        s_old = [s_ref[q] for q in pairs]
        o = [_dg(jnp.concatenate([atp_ref[rows, lanes[q]], rp_ref[rows, lanes[q]]], axis=0),
                 s_old[q].astype(BF16), NT) for q in pairs]
        ds = []
        for q in pairs:
            u = o[q][:RCHUNK] + u0_ref[rows, lanes[q]]
            y_ref[rows, lanes[q]] = o[q][RCHUNK:] + y0_ref[rows, lanes[q]]
            uv = jnp.concatenate([u.astype(BF16), vb_ref[rows, lanes[q]]], axis=0)
            bk = jnp.concatenate([bh_ref[rows, lanes[q]], kh_ref[rows, lanes[q]]], axis=0)
            ds.append(_dg(uv, bk, TN))
        for q in pairs:
            dl = dl_ref[c * SUB:c * SUB + 1, lanes[q]]
            s_ref[q] = s_old[q] * dl + jnp.where(bd, ds[q], 0.0)

    for q in range(npair):
        sl = slice(q * LANE, (q + 1) * LANE)
        y = y_ref[:, sl]
        yh, yl = _split2(y)
        mean = _dg(yh, hmean) + _dg(yl, hmean)
        d = y - mean
        var = _bdot(d * d, hmean)
        yn = d * lax.rsqrt(var + RWKV_GN_EPS) * pb_ref[1:2, sl] + pb_ref[2:3, sl]
        o_ref[:, sl] = ((yn + bonus_ref[:, sl]) * gs_ref[:, sl]).astype(BF16)


def _rwkv_scan(atp, rp, bh, kh, vb, u0, y0, bonus, gs, dl, pb):
    l = atp.shape[0]
    t = min(RST, l)
    row = pl.BlockSpec((t, RW), lambda i: (i, 0))
    return pl.pallas_call(
        _rwkv_scan_kernel,
        grid=(l // t,),
        in_specs=[row] * 9 + [pl.BlockSpec(((t // RCHUNK) * SUB, RW), lambda i: (i, 0)),
                              pl.BlockSpec((SUB, RW), lambda i: (0, 0))],
        out_specs=row,
        out_shape=jax.ShapeDtypeStruct((l, RW), BF16),
        scratch_shapes=[pltpu.VMEM((RW // LANE, LANE, LANE), F32), pltpu.VMEM((t, RW), F32)],
        compiler_params=_params(dimension_semantics=("arbitrary",)),
        name="rwkv_scan",
    )(atp, rp, bh, kh, vb, u0, y0, bonus, gs, dl, pb)


def _ssd_conv(main_ref, halo_ref, wb_ref, buf_ref, first):
    q = main_ref.shape[0]
    buf_ref[0:SUB, :] = jnp.where(first, 0.0, halo_ref[...])
    buf_ref[SUB:SUB + q, :] = main_ref[...]
    acc = wb_ref[SCONV:SCONV + 1, :]
    for j in range(SCONV):
        acc = acc + buf_ref[pl.ds(SUB - (SCONV - 1) + j, q), :] * wb_ref[j:j + 1, :]
    return _silu(acc)


def _ssd_kernel(z_ref, x_ref, b_ref, c_ref, dt_ref, xh_ref, bhalo_ref, chalo_ref,
                cwx_ref, cwb_ref, cwc_ref, hp_ref, dx_ref, nw_ref,
                o_ref, prev_ref, xbuf_ref, bbuf_ref, cbuf_ref, ybuf_ref):
    g = pl.program_id(0)
    first = pl.program_id(1) == 0
    q = z_ref.shape[0]

    @pl.when(first)
    def _():
        prev_ref[...] = jnp.zeros_like(prev_ref)

    xs = _ssd_conv(x_ref, xh_ref, cwx_ref, xbuf_ref, first)
    bm = _ssd_conv(b_ref, bhalo_ref, cwb_ref, bbuf_ref, first)
    cm = _ssd_conv(c_ref, chalo_ref, cwc_ref, cbuf_ref, first)

    sl_r = _iota((LANE, LANE), 0)
    sl_c = _iota((LANE, LANE), 1)
    sel = jnp.where((sl_r == sl_c + g * SHG) & (sl_c < SHG), 1.0, 0.0).astype(BF16)
    dt_raw = _dot_sel(dt_ref[...], sel)
    dt = _softplus(dt_raw + hp_ref[0:1, :])
    a_neg = -jnp.exp(hp_ref[1:2, :])
    acs = _seg_cumsum(dt * a_neg, q)
    acs_t = acs.T
    acs_last = acs[q - 1:q, :]

    expand = jnp.where(_iota((LANE, SGW), 0) == _idiv(_iota((LANE, SGW), 1), SHEAD), 1.0, 0.0).astype(BF16)
    expand_t = jnp.where(_iota((SGW, LANE), 1) == _idiv(_iota((SGW, LANE), 0), SHEAD), 1.0, 0.0).astype(BF16)
    dt_e = _dot_sel(dt, expand)
    eacs_e = _dot_sel(jnp.exp(acs), expand)
    ds_e = _dot_sel(jnp.exp(acs_last - acs), expand)
    xc = xs * dt_e

    cmb = cm.astype(BF16)
    bmb = bm.astype(BF16)
    scores = _dg(cmb, bmb, NT)
    prev = prev_ref[...]
    y_off = _dg(cmb, prev.astype(BF16), NT) * eacs_e
    states = _dg((xc * ds_e).astype(BF16), bmb, TN)
    eh, em, el = _split3(jnp.exp(acs_t))
    cdec = _dg(expand_t, eh) + (_dg(expand_t, em) + _dg(expand_t, el))
    prev_ref[...] = prev * cdec[:, q - 1:q] + states

    causal = _iota((q, q), 1) <= _iota((q, q), 0)
    for h in range(SHG):
        diff = acs[:, h:h + 1] - acs_t[h:h + 1, :]
        lm = jnp.exp(jnp.where(causal, diff, -jnp.inf))
        ybuf_ref[:, h * SHEAD:(h + 1) * SHEAD] = _bdot(scores * lm, xc[:, h * SHEAD:(h + 1) * SHEAD])
    y = ybuf_ref[...] + y_off + dx_ref[0:1, :] * xs
    y = y * _silu(z_ref[...])
    ms = jnp.mean(y * y, axis=-1, keepdims=True)
    o_ref[...] = (y * lax.rsqrt(ms + EPS) * nw_ref[0:1, :]).astype(BF16)


def _ssd(p, cw8, hp, dx8, nw8):
    l = p.shape[0]
    q = SCHUNK
    hb = q // SUB

    def halo_idx(i):
        return jnp.maximum(i * hb - 1, 0)

    zb, xb = E_Z // SGW, E_X // SGW
    bb, cb, db = E_B // LANE, E_C // LANE, E_DT // LANE
    in_specs = [
        pl.BlockSpec((q, SGW), lambda g, i: (i, zb + g)),
        pl.BlockSpec((q, SGW), lambda g, i: (i, xb + g)),
        pl.BlockSpec((q, LANE), lambda g, i: (i, bb + g)),
        pl.BlockSpec((q, LANE), lambda g, i: (i, cb + g)),
        pl.BlockSpec((q, LANE), lambda g, i: (i, db)),
        pl.BlockSpec((SUB, SGW), lambda g, i: (halo_idx(i), xb + g)),
        pl.BlockSpec((SUB, LANE), lambda g, i: (halo_idx(i), bb + g)),
        pl.BlockSpec((SUB, LANE), lambda g, i: (halo_idx(i), cb + g)),
        pl.BlockSpec((SUB, SGW), lambda g, i: (0, g)),
        pl.BlockSpec((SUB, LANE), lambda g, i: (0, SW // LANE + g)),
        pl.BlockSpec((SUB, LANE), lambda g, i: (0, SW // LANE + SGROUPS + g)),
        pl.BlockSpec((SUB, LANE), lambda g, i: (0, g)),
        pl.BlockSpec((SUB, SGW), lambda g, i: (0, g)),
        pl.BlockSpec((SUB, SGW), lambda g, i: (0, g)),
    ]
    return pl.pallas_call(
        _ssd_kernel,
        grid=(SGROUPS, l // q),
        in_specs=in_specs,
        out_specs=pl.BlockSpec((q, SGW), lambda g, i: (i, g)),
        out_shape=jax.ShapeDtypeStruct((l, SW), BF16),
        scratch_shapes=[pltpu.VMEM((SGW, SSTATE), F32),
                        pltpu.VMEM((q + SUB, SGW), F32),
                        pltpu.VMEM((q + SUB, LANE), F32),
                        pltpu.VMEM((q + SUB, LANE), F32),
                        pltpu.VMEM((q, SGW), F32)],
        compiler_params=_params(dimension_semantics=("parallel", "arbitrary")),
        name="ssd",
    )(p, p, p, p, p, p, p, p, cw8, cw8, cw8, hp, dx8, nw8)


GT = 256


def _gla_kernel(q_ref, k_ref, v_ref, g_ref, gd_ref, gkw_ref, gkb_ref, nw_ref, o_ref, st_ref, oi_ref):
    @pl.when(pl.program_id(1) == 0)
    def _():
        st_ref[...] = jnp.zeros_like(st_ref)

    t = q_ref.shape[0]
    nchunk = t // GCHUNK
    gk = _log_sigmoid(_dot3(gd_ref[...], gkw_ref[...]) + gkb_ref[0:1, :]) * (1.0 / 16.0)
    bc = _seg_cumsum(gk, GCHUNK)
    mid = _chunk_row(bc, GCHUNK, GCHUNK // 2)
    last = _chunk_row(bc, GCHUNK, GCHUNK - 1)
    qs = q_ref[...] * (GHK ** -0.5)
    k = k_ref[...]
    vb = v_ref[...].astype(BF16)

    ri = _iota((t, t), 0)
    ci = _iota((t, t), 1)
    incl = (_idiv(ri, GCHUNK) == _idiv(ci, GCHUNK)) & (ci <= ri)
    attn = _bdot(qs * jnp.exp(bc - mid), k * jnp.exp(mid - bc), NT)
    attn = jnp.where(incl, attn, 0.0)
    o_intra = _bdot(attn, vb)

    q_in = (qs * jnp.exp(bc)).astype(BF16)
    k_st = (k * jnp.exp(last - bc)).astype(BF16)
    dlast = jnp.exp(last)
    for c in range(nchunk):
        rows = slice(c * GCHUNK, (c + 1) * GCHUNK)
        st = st_ref[...]
        oi_ref[rows, :] = _dg(q_in[rows], st.astype(BF16), NT)
        st_ref[...] = st * dlast[c * GCHUNK:c * GCHUNK + 1, :] + _dg(vb[rows], k_st[rows], TN)
    o = o_intra + oi_ref[...]
    ms = jnp.mean(o * o, axis=-1, keepdims=True)
    o = o * lax.rsqrt(ms + EPS) * nw_ref[0:1, :]
    o_ref[...] = (o * _silu(g_ref[...])).astype(BF16)


def _gla(p, gkwp, gkb8, nw8):
    l = p.shape[0]
    t = min(GT, l)
    return pl.pallas_call(
        _gla_kernel,
        grid=(GHEADS, l // t),
        in_specs=[pl.BlockSpec((t, GHK), lambda h, i: (i, O_Q // GHK + h)),
                  pl.BlockSpec((t, GHK), lambda h, i: (i, O_K // GHK + h)),
                  pl.BlockSpec((t, GHV), lambda h, i: (i, O_V // GHV + h)),
                  pl.BlockSpec((t, GHV), lambda h, i: (i, O_G // GHV + h)),
                  pl.BlockSpec((t, LANE), lambda h, i: (i, O_GD // LANE)),
                  pl.BlockSpec((LANE, GHK), lambda h, i: (0, h)),
                  pl.BlockSpec((SUB, GHK), lambda h, i: (0, h)),
                  pl.BlockSpec((SUB, GHV), lambda h, i: (0, 0))],
        out_specs=pl.BlockSpec((t, GHV), lambda h, i: (i, h)),
        out_shape=jax.ShapeDtypeStruct((l, GVAL), BF16),
        scratch_shapes=[pltpu.VMEM((GHV, GHK), F32), pltpu.VMEM((t, GHV), F32)],
        compiler_params=_params(dimension_semantics=("parallel", "arbitrary")),
        name="gla",
    )(p, p, p, p, p, gkwp, gkb8, nw8)


def _rows8(*rows):
    n = rows[0].shape[0]
    m = jnp.stack([r.astype(F32) for r in rows])
    return jnp.concatenate([m, jnp.zeros((SUB - len(rows), n), F32)], axis=0)


def _pad_cols(w, n):
    return jnp.concatenate([w, jnp.zeros((w.shape[0], n - w.shape[1]), w.dtype)], axis=1)


def _pad_rows(w, n):
    return jnp.concatenate([w, jnp.zeros((n - w.shape[0], w.shape[1]), w.dtype)], axis=0)


def _even_weight(w):
    rwkv_main = w[:, :4 * RW]
    wd = w[:, 4 * RW:4 * RW + DECAY_LORA]
    ad = w[:, 4 * RW + DECAY_LORA:4 * RW + DECAY_LORA + AAA_LORA]
    s0 = 4 * RW + DECAY_LORA + AAA_LORA
    z = w[:, s0:s0 + SW]
    xbc = w[:, s0 + SW:s0 + SW + SW + 2 * SGROUPS * SSTATE]
    dt = w[:, s0 + 2 * SW + 2 * SGROUPS * SSTATE:]
    parts = [rwkv_main, z, xbc, _pad_cols(wd, LANE), _pad_cols(ad, LANE), _pad_cols(dt, LANE),
             jnp.zeros((w.shape[0], LANE), w.dtype)]
    return jnp.concatenate(parts, axis=1).astype(BF16)


def _odd_weight(w):
    main = w[:, :2 * GKEY + 2 * GVAL]
    gd = w[:, 2 * GKEY + 2 * GVAL:]
    return jnp.concatenate([main, _pad_cols(gd, LANE)], axis=1).astype(BF16)


def kernel(x, norm_w, final_norm_w, w_in_even, w_out_even, rwkv_mu, rwkv_w0, rwkv_w2, rwkv_a0, rwkv_a2, rwkv_k_k, rwkv_k_a, rwkv_r_k, rwkv_ln_w, rwkv_ln_b, rwkv_v0, rwkv_v1, rwkv_v2, ssm_conv_w, ssm_conv_b, ssm_dt_bias, ssm_A_log, ssm_D, ssm_norm_w, w_in_odd, w_out_odd, gla_gk_w, gla_gk_b, gla_norm_w):
    bsz, l, d = x.shape
    depth = norm_w.shape[0]
    outs = []
    for b in range(bsz):
        res = x[b]
        h = _rmsnorm(res, _rows8(norm_w[0]))
        v_first = None
        for layer in range(depth):
            i = layer // 2
            last = layer == depth - 1
            nw_next = _rows8(final_norm_w if last else norm_w[layer + 1])
            if layer % 2 == 0:
                p = _inproj(h, _even_weight(w_in_even[i]), tn=512)
                mu = rwkv_mu[i]
                pa = _rows8(mu[0:RW], mu[RW:2 * RW], mu[2 * RW:3 * RW], mu[3 * RW:4 * RW],
                            rwkv_w0[i], rwkv_a0[i], rwkv_k_k[i], rwkv_k_a[i])
                v0 = rwkv_v0[i - 1] if i > 0 else jnp.zeros((RW,), F32)
                pb = _rows8(rwkv_r_k[i].reshape(RW), rwkv_ln_w[i], rwkv_ln_b[i], v0)
                zpad = jnp.zeros((LANE - DECAY_LORA,), F32)
                plora = _rows8(jnp.concatenate([mu[4 * RW:4 * RW + DECAY_LORA], zpad]),
                               jnp.concatenate([mu[4 * RW + DECAY_LORA:], zpad]))
                w2p = _pad_rows(rwkv_w2[i], LANE)
                a2p = _pad_rows(rwkv_a2[i], LANE)
                if i == 0:
                    vmix = None
                else:
                    tl = _vlora(p, _rows8(mu[2 * RW:3 * RW]), _pad_cols(rwkv_v1[i - 1], LANE))
                    vmix = (tl, _pad_rows(rwkv_v2[i - 1], LANE), v_first)
                (atp, rp, bh, kh, vb, u0, y0, bonus, gs, dl, v_out) = _rwkv_prep(
                    p, pa, pb, plora, w2p, a2p, vmix)
                if i == 0:
                    v_first = v_out
                y_a = _rwkv_scan(atp, rp, bh, kh, vb, u0, y0, bonus, gs, dl, pb)

                cw8 = jnp.concatenate([ssm_conv_w[i], ssm_conv_b[i][None, :],
                                       jnp.zeros((SUB - SCONV - 1, ssm_conv_w.shape[2]), F32)], axis=0)
                pad_g = lambda a: _pad_cols(a.reshape(SGROUPS, SHG), LANE).reshape(SGROUPS * LANE)
                hp = _rows8(pad_g(ssm_dt_bias[i]), pad_g(ssm_A_log[i]))
                dx8 = _rows8(jnp.repeat(ssm_D[i], SHEAD))
                y_b = _ssd(p, cw8, hp, dx8, _rows8(ssm_norm_w[i]))
                w_out = w_out_even[i].astype(BF16)
                o = _outproj([y_a, y_b], w_out, res, nw_next, last)
            else:
                p = _inproj(h, _odd_weight(w_in_odd[i]), tn=896)
                y = _gla(p, _pad_rows(gla_gk_w[i], LANE), _rows8(gla_gk_b[i]), _rows8(gla_norm_w[i]))
                o = _outproj([y], w_out_odd[i].astype(BF16), res, nw_next, last)
            if last:
                res = o[0]
            else:
                res, h = o
        outs.append(res)
    return jnp.stack(outs).astype(x.dtype)
```

```python
import functools

import jax
import jax.numpy as jnp
from jax import lax
from jax.experimental import pallas as pl
from jax.experimental.pallas import tpu as pltpu

F32 = jnp.float32
BF16 = jnp.bfloat16

D_MODEL = 2048
EPS = 1e-5
RW = 2048
RH = 64
DECAY_LORA = 96
AAA_LORA = 96
MV_LORA = 64
RWKV_GN_EPS = 64e-5
RCHUNK = 64
SW = 2048
SHEAD = 64
SGROUPS = 4
SSTATE = 128
SCONV = 4
SCHUNK = 128
SGW = SW // SGROUPS
SHG = SGW // SHEAD
GHEADS = 4
GKEY = 1024
GVAL = 2048
GHK = 256
GHV = 512
GRANK = 16
GCHUNK = 64

LANE = 128
SUB = 8
VMEM_LIMIT = 56 * 1024 * 1024

E_R, E_K, E_V, E_G = 0, 2048, 4096, 6144
E_MAIN = 8192
T_Z, T_X = 0, 2048
T_B, T_C = 4096, 4608
T_WD, T_AD, T_DT = 5120, 5248, 5376
T_COLS = 5632
O_Q, O_K, O_V, O_G = 0, 1024, 2048, 4096
O_MAIN = 6144

NN = (((1,), (0,)), ((), ()))
NT = (((1,), (1,)), ((), ()))
TN = (((0,), (0,)), ((), ()))


def _dg(a, b, dims=NN):
    return lax.dot_general(a, b, dims, preferred_element_type=F32)


def _bdot(a, b, dims=NN):
    return _dg(a.astype(BF16), b.astype(BF16), dims)


def _split2(x):
    hi = x.astype(BF16)
    lo = (x - hi.astype(F32)).astype(BF16)
    return hi, lo


def _split3(x):
    hi = x.astype(BF16)
    r = x - hi.astype(F32)
    mid = r.astype(BF16)
    lo = (r - mid.astype(F32)).astype(BF16)
    return hi, mid, lo


def _dot3(a, b, dims=NN):
    ah, al = _split2(a)
    bh, bl = _split2(b)
    return _dg(ah, bh, dims) + (_dg(ah, bl, dims) + _dg(al, bh, dims))


def _dot_sel(a, sel_bf16, dims=NN):
    ah, am, al = _split3(a)
    return _dg(ah, sel_bf16, dims) + (_dg(am, sel_bf16, dims) + _dg(al, sel_bf16, dims))


def _sigmoid(x):
    return 1.0 / (1.0 + jnp.exp(-x))


def _silu(x):
    return x * _sigmoid(x)


def _log_sigmoid(x):
    return jnp.minimum(x, 0.0) - jnp.log(1.0 + jnp.exp(-jnp.abs(x)))


def _softplus(x):
    return jnp.maximum(x, 0.0) + jnp.log(1.0 + jnp.exp(-jnp.abs(x)))


def _iota(shape, axis):
    return lax.broadcasted_iota(jnp.int32, shape, axis)


def _idiv(x, n):
    return lax.shift_right_logical(x, n.bit_length() - 1)


def _seg_cumsum(x, chunk):
    rin = _iota(x.shape, 0) & (chunk - 1)
    s = 1
    while s < chunk:
        x = x + jnp.where(rin >= s, pltpu.roll(x, s, 0), 0.0)
        s *= 2
    return x


def _chunk_row(x, chunk, r):
    t, w = x.shape
    x3 = x.reshape(t // chunk, chunk, w)
    return jnp.broadcast_to(x3[:, r:r + 1, :], x3.shape).reshape(t, w)


def _head_ones(n, head, scale=1.0):
    r = _idiv(_iota((n, n), 0), head)
    c = _idiv(_iota((n, n), 1), head)
    return jnp.where(r == c, scale, 0.0).astype(BF16)


def _params(**kw):
    return pltpu.CompilerParams(vmem_limit_bytes=VMEM_LIMIT, **kw)


def _rmsnorm_kernel(x_ref, w_ref, o_ref):
    x = x_ref[...]
    ms = jnp.mean(x * x, axis=-1, keepdims=True)
    o_ref[...] = (x * lax.rsqrt(ms + EPS) * w_ref[0:1, :]).astype(BF16)


def _rmsnorm(x, w8, tm=512):
    l, d = x.shape
    return pl.pallas_call(
        _rmsnorm_kernel,
        grid=(l // tm,),
        in_specs=[pl.BlockSpec((tm, d), lambda i: (i, 0)),
                  pl.BlockSpec((SUB, d), lambda i: (0, 0))],
        out_specs=pl.BlockSpec((tm, d), lambda i: (i, 0)),
        out_shape=jax.ShapeDtypeStruct((l, d), BF16),
        compiler_params=_params(dimension_semantics=("parallel",)),
        name="rmsnorm",
    )(x, w8)


def _inproj_kernel(h_ref, w_ref, o_ref):
    o_ref[...] = jnp.dot(h_ref[...], w_ref[...].astype(BF16), preferred_element_type=F32)


def _inproj(h, w, n, tn, tm=2048):
    l, d = h.shape
    tm = min(tm, l)
    return pl.pallas_call(
        _inproj_kernel,
        grid=(l // tm, n // tn),
        in_specs=[pl.BlockSpec((tm, d), lambda i, j: (i, 0)),
                  pl.BlockSpec((d, tn), lambda i, j: (0, j))],
        out_specs=pl.BlockSpec((tm, tn), lambda i, j: (i, j)),
        out_shape=jax.ShapeDtypeStruct((l, n), F32),
        compiler_params=_params(dimension_semantics=("parallel", "parallel")),
        name="inproj",
    )(h, w)


def _outproj_kernel(*refs, n_in, kt_per_in, final):
    y_refs = refs[:n_in]
    w_ref, res_ref, nw_ref = refs[n_in:n_in + 3]
    outs = refs[n_in + 3:]
    acc_ref = outs[-1]
    k = pl.program_id(1)
    nk = pl.num_programs(1)

    @pl.when(k == 0)
    def _():
        acc_ref[...] = res_ref[...]

    for s in range(n_in):
        @pl.when((k >= s * kt_per_in) & (k < (s + 1) * kt_per_in))
        def _(s=s):
            acc_ref[...] += jnp.dot(y_refs[s][...], w_ref[...], preferred_element_type=F32)

    @pl.when(k == nk - 1)
    def _():
        r = acc_ref[...]
        ms = jnp.mean(r * r, axis=-1, keepdims=True)
        hn = r * lax.rsqrt(ms + EPS) * nw_ref[0:1, :]
        if final:
            outs[0][...] = hn
        else:
            outs[0][...] = r
            outs[1][...] = hn.astype(BF16)


def _outproj(ys, w, res, nw8, final, tm=512, tk=1024):
    l, d = res.shape
    n_in = len(ys)
    kt_per_in = ys[0].shape[1] // tk
    nk = n_in * kt_per_in
    tm = min(tm, l)

    def y_map(s):
        return lambda i, k: (i, jnp.clip(k - s * kt_per_in, 0, kt_per_in - 1))

    in_specs = [pl.BlockSpec((tm, tk), y_map(s)) for s in range(n_in)]
    in_specs += [pl.BlockSpec((tk, d), lambda i, k: (k, 0)),
                 pl.BlockSpec((tm, d), lambda i, k: (i, 0)),
                 pl.BlockSpec((SUB, d), lambda i, k: (0, 0))]
    row_spec = pl.BlockSpec((tm, d), lambda i, k: (i, 0))
    if final:
        out_specs = [row_spec]
        out_shape = [jax.ShapeDtypeStruct((l, d), F32)]
    else:
        out_specs = [row_spec, row_spec]
        out_shape = [jax.ShapeDtypeStruct((l, d), F32), jax.ShapeDtypeStruct((l, d), BF16)]
    return pl.pallas_call(
        functools.partial(_outproj_kernel, n_in=n_in, kt_per_in=kt_per_in, final=final),
        grid=(l // tm, nk),
        in_specs=in_specs,
        out_specs=out_specs,
        out_shape=out_shape,
        scratch_shapes=[pltpu.VMEM((tm, d), F32)],
        compiler_params=_params(dimension_semantics=("parallel", "arbitrary")),
        name="outproj",
    )(*ys, w, res, nw8)


def _shift_lerp(x, halo_ref, mu, first):
    prev_last = jnp.where(first, 0.0, halo_ref[SUB - 1:SUB, :])
    xs = pltpu.roll(x, 1, 0)
    xprev = jnp.where(_iota(x.shape, 0) == 0, prev_last, xs)
    return x + (xprev - x) * mu


def _vlora_kernel(v_ref, vh_ref, mu_ref, v1_ref, o_ref):
    first = pl.program_id(0) == 0
    v = _shift_lerp(v_ref[...], vh_ref, mu_ref[0:1, :], first)
    o_ref[...] = _dot3(v, v1_ref[...])


def _vlora(p, mu8, v1p, t=256):
    l = p.shape[0]
    t = min(t, l)
    hb = t // SUB
    cb = E_V // RW
    return pl.pallas_call(
        _vlora_kernel,
        grid=(l // t,),
        in_specs=[pl.BlockSpec((t, RW), lambda i: (i, cb)),
                  pl.BlockSpec((SUB, RW), lambda i: (jnp.maximum(i * hb - 1, 0), cb)),
                  pl.BlockSpec((SUB, RW), lambda i: (0, 0)),
                  pl.BlockSpec((RW, LANE), lambda i: (0, 0))],
        out_specs=pl.BlockSpec((t, LANE), lambda i: (i, 0)),
        out_shape=jax.ShapeDtypeStruct((l, LANE), F32),
        compiler_params=_params(dimension_semantics=("parallel",)),
        name="rwkv_vlora",
    )(p, p, mu8, v1p)


RT = 512
RST = 256


def _rwkv_prep_kernel(*refs, has_vmix):
    (r_ref, k_ref, v_ref, g_ref, wd_ref, ad_ref,
     rh_ref, kh_ref, vh_ref, gh_ref, wdh_ref, adh_ref,
     pa_ref, pb_ref, pl_ref, w2_ref, a2_ref) = refs[:17]
    pos = 17
    if has_vmix:
        tl_ref, v2_ref, vf_ref = refs[pos:pos + 3]
        pos += 3
    (atp_ref, rp_ref, bh_ref, kh2_ref, vb_ref,
     u0_ref, y0_ref, bonus_ref, gs_ref, dl_ref, vout_ref) = refs[pos:]

    first = pl.program_id(1) == 0
    t = r_ref.shape[0]
    nchunk = t // RCHUNK

    mu_r, mu_k, mu_v, mu_g = (pa_ref[j:j + 1, :] for j in range(4))
    w0, a0, k_k, k_a = (pa_ref[j:j + 1, :] for j in range(4, 8))
    r_k, v0 = pb_ref[0:1, :], pb_ref[3:4, :]
    mu_wd, mu_ad = pl_ref[0:1, :], pl_ref[1:2, :]

    r = _shift_lerp(r_ref[...], rh_ref, mu_r, first)
    k = _shift_lerp(k_ref[...], kh_ref, mu_k, first)
    v = _shift_lerp(v_ref[...], vh_ref, mu_v, first)
    g = _shift_lerp(g_ref[...], gh_ref, mu_g, first)
    wd = _shift_lerp(wd_ref[...], wdh_ref, mu_wd, first)
    ad = _shift_lerp(ad_ref[...], adh_ref, mu_ad, first)

    w_log = _log_sigmoid(w0 + _dot3(jnp.tanh(wd), w2_ref[...])) - 0.5
    logd = -jnp.exp(w_log)
    a = _sigmoid(a0 + _dot3(ad, a2_ref[...]))
    if has_vmix:
        v = v + (vf_ref[...] - v) * _sigmoid(v0 + _dot3(tl_ref[...], v2_ref[...]))
    vout_ref[...] = v

    hm = _head_ones(LANE, RH)
    kk = k * k_k
    ss = _dot_sel(kk * kk, hm)
    kk = kk * lax.rsqrt(jnp.maximum(ss, 1e-24))
    kh = k * (1.0 + (a - 1.0) * k_a)
    av = -kk
    bv = kk * a

    bonus_ref[...] = _dot_sel(r * kh * r_k, hm) * v
    gs_ref[...] = _silu(g)

    cum = _seg_cumsum(logd, RCHUNK)
    cl = _chunk_row(cum, RCHUNK, RCHUNK - 1)
    e_n = jnp.exp(-cum)
    e_l = jnp.exp(cl - cum)
    at = av * jnp.exp(cum - logd)
    rt = r * jnp.exp(cum)
    bt = bv * e_n
    kt = kh * e_n
    bh_ref[...] = (bv * e_l).astype(BF16)
    kh2_ref[...] = (kh * e_l).astype(BF16)
    vb_ref[...] = v.astype(BF16)
    dl_ref[...] = jnp.exp(cl).reshape(nchunk, RCHUNK, LANE)[:, 0:SUB, :].reshape(nchunk * SUB, LANE)

    c2 = 2 * RCHUNK
    ri = _iota((c2, c2), 0)
    ci = _iota((c2, c2), 1)
    same = _idiv(ri, RCHUNK) == _idiv(ci, RCHUNK)
    strict = same & (ci < ri)
    incl = same & (ci <= ri)
    eye = jnp.where(ri == ci, 1.0, 0.0)
    head0 = _iota((RCHUNK, LANE), 1) < RH

    def stack(xc):
        return jnp.concatenate([jnp.where(head0, xc, 0.0), jnp.where(head0, 0.0, xc)], axis=0)

    def unstack(xs):
        return xs[:RCHUNK] + xs[RCHUNK:]

    chunks = range(nchunk)
    rows = [slice(c * RCHUNK, (c + 1) * RCHUNK) for c in chunks]
    atm = [stack(at[rw]).astype(BF16) for rw in rows]
    rtm = [stack(rt[rw]) for rw in rows]
    vm = [stack(v[rw]).astype(BF16) for rw in rows]
    amat = []
    for c in chunks:
        btc = bt[rows[c]].astype(BF16)
        ktc = kt[rows[c]].astype(BF16)
        lhs = jnp.concatenate([atm[c], rtm[c].astype(BF16)], axis=0)
        rhs = jnp.concatenate([btc, btc, ktc, ktc], axis=0)
        amat.append(_dg(lhs, rhs, NT))
    a_ab = [jnp.where(strict, m[:c2, :c2], 0.0) for m in amat]
    a_ak = [jnp.where(strict, m[:c2, c2:], 0.0).astype(BF16) for m in amat]
    a_rb = [jnp.where(incl, m[c2:, :c2], 0.0).astype(BF16) for m in amat]
    a_rk = [jnp.where(incl, m[c2:, c2:], 0.0).astype(BF16) for m in amat]
    akv2 = [_dg(jnp.concatenate([a_ak[c], a_rk[c]], axis=0), vm[c]) for c in chunks]
    akv = [m[:c2] for m in akv2]
    arkv = [m[c2:] for m in akv2]
    inv = [eye + m for m in a_ab]
    pw = [_bdot(m, m) for m in a_ab]
    s = 2
    while s < RCHUNK // 2:
        pwb = [m.astype(BF16) for m in pw]
        both = [_dg(pwb[c], jnp.concatenate([pwb[c], inv[c].astype(BF16)], axis=1)) for c in chunks]
        pw = [m[:, :c2] for m in both]
        inv = [inv[c] + both[c][:, c2:] for c in chunks]
        s *= 2
    inv = [inv[c] + _bdot(pw[c], inv[c]) for c in chunks]
    x = [_bdot(inv[c], jnp.concatenate([atm[c], akv[c].astype(BF16)], axis=1)) for c in chunks]
    z = [_dg(a_rb[c], x[c].astype(BF16)) for c in chunks]
    for c in chunks:
        atp_ref[rows[c], :] = unstack(x[c][:, :LANE]).astype(BF16)
        u0_ref[rows[c], :] = unstack(x[c][:, LANE:])
        rp_ref[rows[c], :] = unstack(rtm[c] + z[c][:, :LANE]).astype(BF16)
        y0_ref[rows[c], :] = unstack(z[c][:, LANE:] + arkv[c])


def _rwkv_prep(p, pt, pa, pb, plora, w2p, a2p, vmix):
    l = p.shape[0]
    t = min(RT, l)
    hb = t // SUB
    npair = RW // LANE
    has_vmix = vmix is not None

    def main(off):
        cb = off // LANE
        return pl.BlockSpec((t, LANE), lambda q, i: (i, cb + q))

    def halo(off):
        cb = off // LANE
        return pl.BlockSpec((SUB, LANE), lambda q, i: (jnp.maximum(i * hb - 1, 0), cb + q))

    def main1(off):
        cb = off // LANE
        return pl.BlockSpec((t, LANE), lambda q, i: (i, cb))

    def halo1(off):
        cb = off // LANE
        return pl.BlockSpec((SUB, LANE), lambda q, i: (jnp.maximum(i * hb - 1, 0), cb))

    par = pl.BlockSpec((SUB, LANE), lambda q, i: (0, q))
    par0 = pl.BlockSpec((SUB, LANE), lambda q, i: (0, 0))
    lora = pl.BlockSpec((LANE, LANE), lambda q, i: (0, q))
    row = pl.BlockSpec((t, LANE), lambda q, i: (i, q))

    in_specs = [main(E_R), main(E_K), main(E_V), main(E_G), main1(T_WD), main1(T_AD),
                halo(E_R), halo(E_K), halo(E_V), halo(E_G), halo1(T_WD), halo1(T_AD),
                par, par, par0, lora, lora]
    args = [p] * 4 + [pt] * 2 + [p] * 4 + [pt] * 2 + [pa, pb, plora, w2p, a2p]
    if has_vmix:
        tl, v2p, vf = vmix
        in_specs += [pl.BlockSpec((t, LANE), lambda q, i: (i, 0)), lora, row]
        args += [tl, v2p, vf]
    dl_rows = (t // RCHUNK) * SUB
    out_specs = [row] * 9 + [pl.BlockSpec((dl_rows, LANE), lambda q, i: (i, q)), row]
    sd = jax.ShapeDtypeStruct
    out_shape = ([sd((l, RW), BF16)] * 5 + [sd((l, RW), F32)] * 4
                 + [sd((l // RCHUNK * SUB, RW), F32), sd((l, RW), F32)])
    return pl.pallas_call(
        functools.partial(_rwkv_prep_kernel, has_vmix=has_vmix),
        grid=(npair, l // t),
        in_specs=in_specs,
        out_specs=out_specs,
        out_shape=out_shape,
        compiler_params=_params(dimension_semantics=("parallel", "parallel")),
        name="rwkv_prep",
    )(*args)


def _rwkv_scan_kernel(atp_ref, rp_ref, bh_ref, kh_ref, vb_ref, u0_ref, y0_ref,
                      bonus_ref, gs_ref, dl_ref, pb_ref, o_ref, s_ref, y_ref):
    @pl.when(pl.program_id(0) == 0)
    def _():
        s_ref[...] = jnp.zeros_like(s_ref)

    t = atp_ref.shape[0]
    npair = RW // LANE
    bd = _idiv(_iota((LANE, LANE), 0), RH) == _idiv(_iota((LANE, LANE), 1), RH)
    hmean = _head_ones(LANE, RH, 1.0 / RH)

    pairs = range(npair)
    lanes = [slice(q * LANE, (q + 1) * LANE) for q in pairs]
    for c in range(t // RCHUNK):
        rows = slice(c * RCHUNK, (c + 1) * RCHUNK)
        s_old = [s_ref[q] for q in pairs]
        o = [_dg(jnp.concatenate([atp_ref[rows, lanes[q]], rp_ref[rows, lanes[q]]], axis=0),
                 s_old[q].astype(BF16), NT) for q in pairs]
        ds = []
        for q in pairs:
            u = o[q][:RCHUNK] + u0_ref[rows, lanes[q]]
            y_ref[rows, lanes[q]] = o[q][RCHUNK:] + y0_ref[rows, lanes[q]]
            uv = jnp.concatenate([u.astype(BF16), vb_ref[rows, lanes[q]]], axis=0)
            bk = jnp.concatenate([bh_ref[rows, lanes[q]], kh_ref[rows, lanes[q]]], axis=0)
            ds.append(_dg(uv, bk, TN))
        for q in pairs:
            dl = dl_ref[c * SUB:c * SUB + 1, lanes[q]]
            s_ref[q] = s_old[q] * dl + jnp.where(bd, ds[q], 0.0)

    for q in range(npair):
        sl = slice(q * LANE, (q + 1) * LANE)
        y = y_ref[:, sl]
        yh, yl = _split2(y)
        mean = _dg(yh, hmean) + _dg(yl, hmean)
        d = y - mean
        var = _bdot(d * d, hmean)
        yn = d * lax.rsqrt(var + RWKV_GN_EPS) * pb_ref[1:2, sl] + pb_ref[2:3, sl]
        o_ref[:, sl] = ((yn + bonus_ref[:, sl]) * gs_ref[:, sl]).astype(BF16)


def _rwkv_scan(atp, rp, bh, kh, vb, u0, y0, bonus, gs, dl, pb):
    l = atp.shape[0]
    t = min(RST, l)
    row = pl.BlockSpec((t, RW), lambda i: (i, 0))
    return pl.pallas_call(
        _rwkv_scan_kernel,
        grid=(l // t,),
        in_specs=[row] * 9 + [pl.BlockSpec(((t // RCHUNK) * SUB, RW), lambda i: (i, 0)),
                              pl.BlockSpec((SUB, RW), lambda i: (0, 0))],
        out_specs=row,
        out_shape=jax.ShapeDtypeStruct((l, RW), BF16),
        scratch_shapes=[pltpu.VMEM((RW // LANE, LANE, LANE), F32), pltpu.VMEM((t, RW), F32)],
        compiler_params=_params(dimension_semantics=("arbitrary",)),
        name="rwkv_scan",
    )(atp, rp, bh, kh, vb, u0, y0, bonus, gs, dl, pb)


def _ssd_conv(main_ref, halo_ref, wb_ref, buf_ref, first):
    q = main_ref.shape[0]
    buf_ref[0:SUB, :] = jnp.where(first, 0.0, halo_ref[...])
    buf_ref[SUB:SUB + q, :] = main_ref[...]
    acc = wb_ref[SCONV:SCONV + 1, :]
    for j in range(SCONV):
        acc = acc + buf_ref[pl.ds(SUB - (SCONV - 1) + j, q), :] * wb_ref[j:j + 1, :]
    return _silu(acc)


def _ssd_kernel(z_ref, x_ref, b_ref, c_ref, dt_ref, xh_ref, bhalo_ref, chalo_ref,
                cwx_ref, cwb_ref, cwc_ref, hp_ref, dx_ref, nw_ref,
                o_ref, prev_ref, xbuf_ref, bbuf_ref, cbuf_ref, ybuf_ref):
    g = pl.program_id(0)
    first = pl.program_id(1) == 0
    q = z_ref.shape[0]

    @pl.when(first)
    def _():
        prev_ref[...] = jnp.zeros_like(prev_ref)

    xs = _ssd_conv(x_ref, xh_ref, cwx_ref, xbuf_ref, first)
    bm = _ssd_conv(b_ref, bhalo_ref, cwb_ref, bbuf_ref, first)
    cm = _ssd_conv(c_ref, chalo_ref, cwc_ref, cbuf_ref, first)

    sl_r = _iota((LANE, LANE), 0)
    sl_c = _iota((LANE, LANE), 1)
    sel = jnp.where((sl_r == sl_c + g * SHG) & (sl_c < SHG), 1.0, 0.0).astype(BF16)
    dt_raw = _dot_sel(dt_ref[...], sel)
    dt = _softplus(dt_raw + hp_ref[0:1, :])
    a_neg = -jnp.exp(hp_ref[1:2, :])
    acs = _seg_cumsum(dt * a_neg, q)
    acs_t = acs.T
    acs_last = acs[q - 1:q, :]

    expand = jnp.where(_iota((LANE, SGW), 0) == _idiv(_iota((LANE, SGW), 1), SHEAD), 1.0, 0.0).astype(BF16)
    expand_t = jnp.where(_iota((SGW, LANE), 1) == _idiv(_iota((SGW, LANE), 0), SHEAD), 1.0, 0.0).astype(BF16)
    dt_e = _dot_sel(dt, expand)
    eacs_e = _dot_sel(jnp.exp(acs), expand)
    ds_e = _dot_sel(jnp.exp(acs_last - acs), expand)
    xc = xs * dt_e

    cmb = cm.astype(BF16)
    bmb = bm.astype(BF16)
    scores = _dg(cmb, bmb, NT)
    prev = prev_ref[...]
    y_off = _dg(cmb, prev.astype(BF16), NT) * eacs_e
    states = _dg((xc * ds_e).astype(BF16), bmb, TN)
    eh, em, el = _split3(jnp.exp(acs_t))
    cdec = _dg(expand_t, eh) + (_dg(expand_t, em) + _dg(expand_t, el))
    prev_ref[...] = prev * cdec[:, q - 1:q] + states

    causal = _iota((q, q), 1) <= _iota((q, q), 0)
    for h in range(SHG):
        diff = acs[:, h:h + 1] - acs_t[h:h + 1, :]
        lm = jnp.exp(jnp.where(causal, diff, -jnp.inf))
        ybuf_ref[:, h * SHEAD:(h + 1) * SHEAD] = _bdot(scores * lm, xc[:, h * SHEAD:(h + 1) * SHEAD])
    y = ybuf_ref[...] + y_off + dx_ref[0:1, :] * xs
    y = y * _silu(z_ref[...])
    ms = jnp.mean(y * y, axis=-1, keepdims=True)
    o_ref[...] = (y * lax.rsqrt(ms + EPS) * nw_ref[0:1, :]).astype(BF16)


def _ssd(p, cw8, hp, dx8, nw8):
    l = p.shape[0]
    q = SCHUNK
    hb = q // SUB

    def halo_idx(i):
        return jnp.maximum(i * hb - 1, 0)

    zb, xb = T_Z // SGW, T_X // SGW
    bb, cb, db = T_B // LANE, T_C // LANE, T_DT // LANE
    in_specs = [
        pl.BlockSpec((q, SGW), lambda g, i: (i, zb + g)),
        pl.BlockSpec((q, SGW), lambda g, i: (i, xb + g)),
        pl.BlockSpec((q, LANE), lambda g, i: (i, bb + g)),
        pl.BlockSpec((q, LANE), lambda g, i: (i, cb + g)),
        pl.BlockSpec((q, LANE), lambda g, i: (i, db)),
        pl.BlockSpec((SUB, SGW), lambda g, i: (halo_idx(i), xb + g)),
        pl.BlockSpec((SUB, LANE), lambda g, i: (halo_idx(i), bb + g)),
        pl.BlockSpec((SUB, LANE), lambda g, i: (halo_idx(i), cb + g)),
        pl.BlockSpec((SUB, SGW), lambda g, i: (0, g)),
        pl.BlockSpec((SUB, LANE), lambda g, i: (0, SW // LANE + g)),
        pl.BlockSpec((SUB, LANE), lambda g, i: (0, SW // LANE + SGROUPS + g)),
        pl.BlockSpec((SUB, LANE), lambda g, i: (0, g)),
        pl.BlockSpec((SUB, SGW), lambda g, i: (0, g)),
        pl.BlockSpec((SUB, SGW), lambda g, i: (0, g)),
    ]
    return pl.pallas_call(
        _ssd_kernel,
        grid=(SGROUPS, l // q),
        in_specs=in_specs,
        out_specs=pl.BlockSpec((q, SGW), lambda g, i: (i, g)),
        out_shape=jax.ShapeDtypeStruct((l, SW), BF16),
        scratch_shapes=[pltpu.VMEM((SGW, SSTATE), F32),
                        pltpu.VMEM((q + SUB, SGW), F32),
                        pltpu.VMEM((q + SUB, LANE), F32),
                        pltpu.VMEM((q + SUB, LANE), F32),
                        pltpu.VMEM((q, SGW), F32)],
        compiler_params=_params(dimension_semantics=("parallel", "arbitrary")),
        name="ssd",
    )(p, p, p, p, p, p, p, p, cw8, cw8, cw8, hp, dx8, nw8)


GT = 256


def _gla_kernel(q_ref, k_ref, v_ref, g_ref, gd_ref, gkw_ref, gkb_ref, nw_ref, o_ref, st_ref, oi_ref):
    @pl.when(pl.program_id(1) == 0)
    def _():
        st_ref[...] = jnp.zeros_like(st_ref)

    t = q_ref.shape[0]
    nchunk = t // GCHUNK
    gk = _log_sigmoid(_dot3(gd_ref[...], gkw_ref[...]) + gkb_ref[0:1, :]) * (1.0 / 16.0)
    bc = _seg_cumsum(gk, GCHUNK)
    mid = _chunk_row(bc, GCHUNK, GCHUNK // 2)
    last = _chunk_row(bc, GCHUNK, GCHUNK - 1)
    qs = q_ref[...] * (GHK ** -0.5)
    k = k_ref[...]
    vb = v_ref[...].astype(BF16)

    ri = _iota((t, t), 0)
    ci = _iota((t, t), 1)
    incl = (_idiv(ri, GCHUNK) == _idiv(ci, GCHUNK)) & (ci <= ri)
    attn = _bdot(qs * jnp.exp(bc - mid), k * jnp.exp(mid - bc), NT)
    attn = jnp.where(incl, attn, 0.0)
    o_intra = _bdot(attn, vb)

    q_in = (qs * jnp.exp(bc)).astype(BF16)
    k_st = (k * jnp.exp(last - bc)).astype(BF16)
    dlast = jnp.exp(last)
    for c in range(nchunk):
        rows = slice(c * GCHUNK, (c + 1) * GCHUNK)
        st = st_ref[...]
        oi_ref[rows, :] = _dg(q_in[rows], st.astype(BF16), NT)
        st_ref[...] = st * dlast[c * GCHUNK:c * GCHUNK + 1, :] + _dg(vb[rows], k_st[rows], TN)
    o = o_intra + oi_ref[...]
    ms = jnp.mean(o * o, axis=-1, keepdims=True)
    o = o * lax.rsqrt(ms + EPS) * nw_ref[0:1, :]
    o_ref[...] = (o * _silu(g_ref[...])).astype(BF16)


def _gla(p, pgd, gkwp, gkb8, nw8):
    l = p.shape[0]
    t = min(GT, l)
    return pl.pallas_call(
        _gla_kernel,
        grid=(GHEADS, l // t),
        in_specs=[pl.BlockSpec((t, GHK), lambda h, i: (i, O_Q // GHK + h)),
                  pl.BlockSpec((t, GHK), lambda h, i: (i, O_K // GHK + h)),
                  pl.BlockSpec((t, GHV), lambda h, i: (i, O_V // GHV + h)),
                  pl.BlockSpec((t, GHV), lambda h, i: (i, O_G // GHV + h)),
                  pl.BlockSpec((t, LANE), lambda h, i: (i, 0)),
                  pl.BlockSpec((LANE, GHK), lambda h, i: (0, h)),
                  pl.BlockSpec((SUB, GHK), lambda h, i: (0, h)),
                  pl.BlockSpec((SUB, GHV), lambda h, i: (0, 0))],
        out_specs=pl.BlockSpec((t, GHV), lambda h, i: (i, h)),
        out_shape=jax.ShapeDtypeStruct((l, GVAL), BF16),
        scratch_shapes=[pltpu.VMEM((GHV, GHK), F32), pltpu.VMEM((t, GHV), F32)],
        compiler_params=_params(dimension_semantics=("parallel", "arbitrary")),
        name="gla",
    )(p, p, p, p, pgd, gkwp, gkb8, nw8)


def _rows8(*rows):
    n = rows[0].shape[0]
    m = jnp.stack([r.astype(F32) for r in rows])
    return jnp.concatenate([m, jnp.zeros((SUB - len(rows), n), F32)], axis=0)


def _pad_cols(w, n):
    return jnp.concatenate([w, jnp.zeros((w.shape[0], n - w.shape[1]), w.dtype)], axis=1)


def _pad_rows(w, n):
    return jnp.concatenate([w, jnp.zeros((n - w.shape[0], w.shape[1]), w.dtype)], axis=0)


def _even_tail_weight(w):
    wd = w[:, E_MAIN:E_MAIN + DECAY_LORA]
    ad = w[:, E_MAIN + DECAY_LORA:E_MAIN + DECAY_LORA + AAA_LORA]
    s0 = E_MAIN + DECAY_LORA + AAA_LORA
    z_xbc = w[:, s0:s0 + 2 * SW + 2 * SGROUPS * SSTATE]
    dt = w[:, s0 + 2 * SW + 2 * SGROUPS * SSTATE:]
    parts = [z_xbc.astype(BF16), _pad_cols(wd, LANE).astype(BF16), _pad_cols(ad, LANE).astype(BF16),
             _pad_cols(dt, 2 * LANE).astype(BF16)]
    return jnp.concatenate(parts, axis=1)


def kernel(x, norm_w, final_norm_w, w_in_even, w_out_even, rwkv_mu, rwkv_w0, rwkv_w2, rwkv_a0, rwkv_a2, rwkv_k_k, rwkv_k_a, rwkv_r_k, rwkv_ln_w, rwkv_ln_b, rwkv_v0, rwkv_v1, rwkv_v2, ssm_conv_w, ssm_conv_b, ssm_dt_bias, ssm_A_log, ssm_D, ssm_norm_w, w_in_odd, w_out_odd, gla_gk_w, gla_gk_b, gla_norm_w):
    bsz, l, d = x.shape
    depth = norm_w.shape[0]
    outs = []
    for b in range(bsz):
        res = x[b]
        h = _rmsnorm(res, _rows8(norm_w[0]))
        v_first = None
        for layer in range(depth):
            i = layer // 2
            last = layer == depth - 1
            nw_next = _rows8(final_norm_w if last else norm_w[layer + 1])
            if layer % 2 == 0:
                p = _inproj(h, w_in_even[i], E_MAIN, tn=512)
                pt = _inproj(h, _even_tail_weight(w_in_even[i]), T_COLS, tn=512)
                mu = rwkv_mu[i]
                pa = _rows8(mu[0:RW], mu[RW:2 * RW], mu[2 * RW:3 * RW], mu[3 * RW:4 * RW],
                            rwkv_w0[i], rwkv_a0[i], rwkv_k_k[i], rwkv_k_a[i])
                v0 = rwkv_v0[i - 1] if i > 0 else jnp.zeros((RW,), F32)
                pb = _rows8(rwkv_r_k[i].reshape(RW), rwkv_ln_w[i], rwkv_ln_b[i], v0)
                zpad = jnp.zeros((LANE - DECAY_LORA,), F32)
                plora = _rows8(jnp.concatenate([mu[4 * RW:4 * RW + DECAY_LORA], zpad]),
                               jnp.concatenate([mu[4 * RW + DECAY_LORA:], zpad]))
                w2p = _pad_rows(rwkv_w2[i], LANE)
                a2p = _pad_rows(rwkv_a2[i], LANE)
                if i == 0:
                    vmix = None
                else:
                    tl = _vlora(p, _rows8(mu[2 * RW:3 * RW]), _pad_cols(rwkv_v1[i - 1], LANE))
                    vmix = (tl, _pad_rows(rwkv_v2[i - 1], LANE), v_first)
                (atp, rp, bh, kh, vb, u0, y0, bonus, gs, dl, v_out) = _rwkv_prep(
                    p, pt, pa, pb, plora, w2p, a2p, vmix)
                if i == 0:
                    v_first = v_out
                y_a = _rwkv_scan(atp, rp, bh, kh, vb, u0, y0, bonus, gs, dl, pb)

                cw8 = jnp.concatenate([ssm_conv_w[i], ssm_conv_b[i][None, :],
                                       jnp.zeros((SUB - SCONV - 1, ssm_conv_w.shape[2]), F32)], axis=0)
                pad_g = lambda a: _pad_cols(a.reshape(SGROUPS, SHG), LANE).reshape(SGROUPS * LANE)
                hp = _rows8(pad_g(ssm_dt_bias[i]), pad_g(ssm_A_log[i]))
                dx8 = _rows8(jnp.repeat(ssm_D[i], SHEAD))
                y_b = _ssd(pt, cw8, hp, dx8, _rows8(ssm_norm_w[i]))
                w_out = w_out_even[i].astype(BF16)
                o = _outproj([y_a, y_b], w_out, res, nw_next, last)
            else:
                p = _inproj(h, w_in_odd[i], O_MAIN, tn=512)
                pgd = _inproj(h, _pad_cols(w_in_odd[i][:, O_MAIN:], LANE), LANE, tn=LANE)
                y = _gla(p, pgd, _pad_rows(gla_gk_w[i], LANE), _rows8(gla_gk_b[i]), _rows8(gla_norm_w[i]))
                o = _outproj([y], w_out_odd[i].astype(BF16), res, nw_next, last)
            if last:
                res = o[0]
            else:
                res, h = o
        outs.append(res)
    return jnp.stack(outs).astype(x.dtype)
```

```python
import functools

import jax
import jax.numpy as jnp
from jax import lax
from jax.experimental import pallas as pl
from jax.experimental.pallas import tpu as pltpu

F32 = jnp.float32
BF16 = jnp.bfloat16

D_MODEL = 2048
EPS = 1e-5
RW = 2048
RH = 64
DECAY_LORA = 96
AAA_LORA = 96
MV_LORA = 64
RWKV_GN_EPS = 64e-5
RCHUNK = 64
SW = 2048
SHEAD = 64
SGROUPS = 4
SSTATE = 128
SCONV = 4
SCHUNK = 128
SGW = SW // SGROUPS
SHG = SGW // SHEAD
GHEADS = 4
GKEY = 1024
GVAL = 2048
GHK = 256
GHV = 512
GRANK = 16
GCHUNK = 64

LANE = 128
SUB = 8
VMEM_LIMIT = 56 * 1024 * 1024

E_R, E_K, E_V, E_G = 0, 2048, 4096, 6144
E_MAIN = 8192
E_ZXBC = 8384
T_Z, T_X = 0, 2048
T_B, T_C = 4096, 4608
T_COLS = 5120
S_WD, S_AD, S_DT = 0, 128, 256
S_COLS = 512
O_Q, O_K, O_V, O_G = 0, 1024, 2048, 4096
O_MAIN = 6144

NN = (((1,), (0,)), ((), ()))
NT = (((1,), (1,)), ((), ()))
TN = (((0,), (0,)), ((), ()))


def _dg(a, b, dims=NN):
    return lax.dot_general(a, b, dims, preferred_element_type=F32)


def _bdot(a, b, dims=NN):
    return _dg(a.astype(BF16), b.astype(BF16), dims)


def _split2(x):
    hi = x.astype(BF16)
    lo = (x - hi.astype(F32)).astype(BF16)
    return hi, lo


def _split3(x):
    hi = x.astype(BF16)
    r = x - hi.astype(F32)
    mid = r.astype(BF16)
    lo = (r - mid.astype(F32)).astype(BF16)
    return hi, mid, lo


def _dot3(a, b, dims=NN):
    ah, al = _split2(a)
    bh, bl = _split2(b)
    return _dg(ah, bh, dims) + (_dg(ah, bl, dims) + _dg(al, bh, dims))


def _dot_sel(a, sel_bf16, dims=NN):
    ah, am, al = _split3(a)
    return _dg(ah, sel_bf16, dims) + (_dg(am, sel_bf16, dims) + _dg(al, sel_bf16, dims))


def _sigmoid(x):
    return 1.0 / (1.0 + jnp.exp(-x))


def _silu(x):
    return x * _sigmoid(x)


def _log_sigmoid(x):
    return jnp.minimum(x, 0.0) - jnp.log(1.0 + jnp.exp(-jnp.abs(x)))


def _softplus(x):
    return jnp.maximum(x, 0.0) + jnp.log(1.0 + jnp.exp(-jnp.abs(x)))


def _iota(shape, axis):
    return lax.broadcasted_iota(jnp.int32, shape, axis)


def _idiv(x, n):
    return lax.shift_right_logical(x, n.bit_length() - 1)


def _seg_cumsum(x, chunk):
    rin = _iota(x.shape, 0) & (chunk - 1)
    s = 1
    while s < chunk:
        x = x + jnp.where(rin >= s, pltpu.roll(x, s, 0), 0.0)
        s *= 2
    return x


def _chunk_row(x, chunk, r):
    t, w = x.shape
    x3 = x.reshape(t // chunk, chunk, w)
    return jnp.broadcast_to(x3[:, r:r + 1, :], x3.shape).reshape(t, w)


def _head_ones(n, head, scale=1.0):
    r = _idiv(_iota((n, n), 0), head)
    c = _idiv(_iota((n, n), 1), head)
    return jnp.where(r == c, scale, 0.0).astype(BF16)


def _params(**kw):
    return pltpu.CompilerParams(vmem_limit_bytes=VMEM_LIMIT, **kw)


def _rmsnorm_kernel(x_ref, w_ref, o_ref):
    x = x_ref[...]
    ms = jnp.mean(x * x, axis=-1, keepdims=True)
    o_ref[...] = (x * lax.rsqrt(ms + EPS) * w_ref[0:1, :]).astype(BF16)


def _rmsnorm(x, w8, tm=512):
    l, d = x.shape
    return pl.pallas_call(
        _rmsnorm_kernel,
        grid=(l // tm,),
        in_specs=[pl.BlockSpec((tm, d), lambda i: (i, 0)),
                  pl.BlockSpec((SUB, d), lambda i: (0, 0))],
        out_specs=pl.BlockSpec((tm, d), lambda i: (i, 0)),
        out_shape=jax.ShapeDtypeStruct((l, d), BF16),
        compiler_params=_params(dimension_semantics=("parallel",)),
        name="rmsnorm",
    )(x, w8)


HALO = 16


def _inproj_kernel(*refs, shift):
    if shift:
        h_ref, hh_ref, w_ref, mu_ref, o_ref = refs
    else:
        h_ref, w_ref, o_ref = refs
    w = w_ref[...].astype(BF16)
    p = jnp.dot(h_ref[...], w, preferred_element_type=F32)
    if shift:
        prev = jnp.dot(hh_ref[...], w, preferred_element_type=F32)[HALO - 1:HALO, :]
        prev = jnp.where(pl.program_id(0) == 0, 0.0, prev)
        pprev = jnp.where(_iota(p.shape, 0) == 0, prev, pltpu.roll(p, 1, 0))
        p = p + (pprev - p) * mu_ref[0:1, :]
    o_ref[...] = p


def _inproj(h, w, n, tn, mu8=None, tm=2048):
    l, d = h.shape
    tm = min(tm, l)
    shift = mu8 is not None
    in_specs = [pl.BlockSpec((tm, d), lambda i, j: (i, 0))]
    args = [h]
    if shift:
        hb = tm // HALO
        in_specs.append(pl.BlockSpec((HALO, d), lambda i, j: (jnp.maximum(i * hb - 1, 0), 0)))
        args.append(h)
    in_specs.append(pl.BlockSpec((d, tn), lambda i, j: (0, j)))
    args.append(w)
    if shift:
        in_specs.append(pl.BlockSpec((SUB, tn), lambda i, j: (0, j)))
        args.append(mu8)
    return pl.pallas_call(
        functools.partial(_inproj_kernel, shift=shift),
        grid=(l // tm, n // tn),
        in_specs=in_specs,
        out_specs=pl.BlockSpec((tm, tn), lambda i, j: (i, j)),
        out_shape=jax.ShapeDtypeStruct((l, n), F32),
        compiler_params=_params(dimension_semantics=("parallel", "parallel")),
        name="inproj",
    )(*args)


def _outproj_kernel(*refs, n_in, kt_per_in, final):
    y_refs = refs[:n_in]
    w_ref, res_ref, nw_ref = refs[n_in:n_in + 3]
    outs = refs[n_in + 3:]
    acc_ref = outs[-1]
    k = pl.program_id(1)
    nk = pl.num_programs(1)

    @pl.when(k == 0)
    def _():
        acc_ref[...] = res_ref[...]

    for s in range(n_in):
        @pl.when((k >= s * kt_per_in) & (k < (s + 1) * kt_per_in))
        def _(s=s):
            acc_ref[...] += jnp.dot(y_refs[s][...], w_ref[...], preferred_element_type=F32)

    @pl.when(k == nk - 1)
    def _():
        r = acc_ref[...]
        ms = jnp.mean(r * r, axis=-1, keepdims=True)
        hn = r * lax.rsqrt(ms + EPS) * nw_ref[0:1, :]
        if final:
            outs[0][...] = hn
        else:
            outs[0][...] = r
            outs[1][...] = hn.astype(BF16)


def _outproj(ys, w, res, nw8, final, tm=512, tk=1024):
    l, d = res.shape
    n_in = len(ys)
    kt_per_in = ys[0].shape[1] // tk
    nk = n_in * kt_per_in
    tm = min(tm, l)

    def y_map(s):
        return lambda i, k: (i, jnp.clip(k - s * kt_per_in, 0, kt_per_in - 1))

    in_specs = [pl.BlockSpec((tm, tk), y_map(s)) for s in range(n_in)]
    in_specs += [pl.BlockSpec((tk, d), lambda i, k: (k, 0)),
                 pl.BlockSpec((tm, d), lambda i, k: (i, 0)),
                 pl.BlockSpec((SUB, d), lambda i, k: (0, 0))]
    row_spec = pl.BlockSpec((tm, d), lambda i, k: (i, 0))
    if final:
        out_specs = [row_spec]
        out_shape = [jax.ShapeDtypeStruct((l, d), F32)]
    else:
        out_specs = [row_spec, row_spec]
        out_shape = [jax.ShapeDtypeStruct((l, d), F32), jax.ShapeDtypeStruct((l, d), BF16)]
    return pl.pallas_call(
        functools.partial(_outproj_kernel, n_in=n_in, kt_per_in=kt_per_in, final=final),
        grid=(l // tm, nk),
        in_specs=in_specs,
        out_specs=out_specs,
        out_shape=out_shape,
        scratch_shapes=[pltpu.VMEM((tm, d), F32)],
        compiler_params=_params(dimension_semantics=("parallel", "arbitrary")),
        name="outproj",
    )(*ys, w, res, nw8)


def _vlora_kernel(v_ref, v1_ref, o_ref):
    o_ref[...] = _dot3(v_ref[...], v1_ref[...])


def _vlora(p, v1p, t=512):
    l = p.shape[0]
    t = min(t, l)
    cb = E_V // RW
    return pl.pallas_call(
        _vlora_kernel,
        grid=(l // t,),
        in_specs=[pl.BlockSpec((t, RW), lambda i: (i, cb)),
                  pl.BlockSpec((RW, LANE), lambda i: (0, 0))],
        out_specs=pl.BlockSpec((t, LANE), lambda i: (i, 0)),
        out_shape=jax.ShapeDtypeStruct((l, LANE), F32),
        compiler_params=_params(dimension_semantics=("parallel",)),
        name="rwkv_vlora",
    )(p, v1p)


RT = 1024
RSUB = 1024
RST = 256


def _rwkv_prep_kernel(*refs, has_vmix, nsub):
    (r_ref, k_ref, v_ref, g_ref, wd_ref, ad_ref, pa_ref, w2_ref, a2_ref) = refs[:9]
    pos = 9
    if has_vmix:
        tl_ref, v2_ref, vf_ref = refs[pos:pos + 3]
        pos += 3
    orefs = refs[pos:]
    st = r_ref.shape[0] // nsub
    for j in range(nsub):
        rs = slice(j * st, (j + 1) * st)
        vmix_j = (tl_ref[rs, :], v2_ref[...], vf_ref[rs, :]) if has_vmix else None
        _rwkv_prep_subtile(r_ref[rs, :], k_ref[rs, :], v_ref[rs, :], g_ref[rs, :], wd_ref[rs, :],
                           ad_ref[rs, :], pa_ref, w2_ref[...], a2_ref[...], vmix_j, orefs, j, st)


def _rwkv_prep_subtile(r, k, v, g, wd, ad, pa_ref, w2, a2, vmix, orefs, j, st):
    (atp_ref, rp_ref, bh_ref, kh2_ref, vb_ref, u0_ref, y0_ref, bonus_ref, gs_ref, dl_ref) = orefs
    rs = slice(j * st, (j + 1) * st)
    nchunk = st // RCHUNK
    w0, a0, k_k, k_a = (pa_ref[i:i + 1, :] for i in range(4))
    r_k, v0 = pa_ref[4:5, :], pa_ref[7:8, :]

    w_log = _log_sigmoid(w0 + _dot3(jnp.tanh(wd), w2)) - 0.5
    logd = -jnp.exp(w_log)
    a = _sigmoid(a0 + _dot3(ad, a2))
    if vmix is not None:
        tl, v2, vf = vmix
        v = v + (vf - v) * _sigmoid(v0 + _dot3(tl, v2))

    hm = _head_ones(LANE, RH)
    kk = k * k_k
    kk2 = kk * kk
    k2h, k2l = _split2(kk2)
    ss = _dg(k2h, hm) + _dg(k2l, hm)
    kk = kk * lax.rsqrt(jnp.maximum(ss, 1e-24))
    kh = k * (1.0 + (a - 1.0) * k_a)
    av = -kk
    bv = kk * a

    rkh, rkl = _split2(r * kh * r_k)
    bonus_ref[rs, :] = (_dg(rkh, hm) + _dg(rkl, hm)) * v
    gs_ref[rs, :] = _silu(g)

    cum = _seg_cumsum(logd, RCHUNK)
    cl = _chunk_row(cum, RCHUNK, RCHUNK - 1)
    e_n = jnp.exp(-cum)
    e_l = jnp.exp(cl - cum)
    at = av * jnp.exp(cum - logd)
    rt = r * jnp.exp(cum)
    bt = bv * e_n
    kt = kh * e_n
    bh_ref[rs, :] = (bv * e_l).astype(BF16)
    kh2_ref[rs, :] = (kh * e_l).astype(BF16)
    vb_ref[rs, :] = v.astype(BF16)
    dl_ref[j * nchunk * SUB:(j + 1) * nchunk * SUB, :] = (
        jnp.exp(cl).reshape(nchunk, RCHUNK, LANE)[:, 0:SUB, :].reshape(nchunk * SUB, LANE))

    lane = _iota((RCHUNK, LANE), 1)
    si = lane & (RH - 1)
    ti = _iota((RCHUNK, LANE), 0)
    strict = si < ti
    incl = si <= ti
    eye = jnp.where(si == ti, 1.0, 0.0)
    head0 = lane < RH

    def stack(xc):
        xc = xc.astype(BF16)
        zero = jnp.zeros_like(xc)
        return jnp.concatenate([jnp.where(head0, xc, zero), jnp.where(head0, zero, xc)], axis=0)

    chunks = range(nchunk)
    rows = [slice(c * RCHUNK, (c + 1) * RCHUNK) for c in chunks]
    orows = [slice(j * st + c * RCHUNK, j * st + (c + 1) * RCHUNK) for c in chunks]
    ats = [stack(at[rw]) for rw in rows]
    vs = [stack(v[rw]) for rw in rows]
    amat = []
    for c in chunks:
        lhs = jnp.concatenate([at[rows[c]], rt[rows[c]]], axis=0).astype(BF16)
        rhs = jnp.concatenate([stack(bt[rows[c]]), stack(kt[rows[c]])], axis=0)
        amat.append(_dg(lhs, rhs, NT))
    a_ab = [jnp.where(strict, m[:RCHUNK, :LANE], 0.0) for m in amat]
    a_ak = [jnp.where(strict, m[:RCHUNK, LANE:], 0.0) for m in amat]
    a_rb = [jnp.where(incl, m[RCHUNK:, :LANE], 0.0).astype(BF16) for m in amat]
    a_rk = [jnp.where(incl, m[RCHUNK:, LANE:], 0.0) for m in amat]
    akv2 = [_dg(jnp.concatenate([a_ak[c], a_rk[c]], axis=0).astype(BF16), vs[c]) for c in chunks]
    akv = [m[:RCHUNK] for m in akv2]
    arkv = [m[RCHUNK:] for m in akv2]
    inv = [eye + m for m in a_ab]
    pw = [_dg(m.astype(BF16), stack(m)) for m in a_ab]
    s = 2
    while s < RCHUNK // 2:
        both = [_dg(pw[c].astype(BF16), jnp.concatenate([stack(pw[c]), stack(inv[c])], axis=1))
                for c in chunks]
        pw = [m[:, :LANE] for m in both]
        inv = [inv[c] + both[c][:, LANE:] for c in chunks]
        s *= 2
    inv = [inv[c] + _dg(pw[c].astype(BF16), stack(inv[c])) for c in chunks]
    x = [_dg(inv[c].astype(BF16), jnp.concatenate([ats[c], stack(akv[c])], axis=1)) for c in chunks]
    z = [_dg(a_rb[c], jnp.concatenate([stack(x[c][:, :LANE]), stack(x[c][:, LANE:])], axis=1))
         for c in chunks]
    for c in chunks:
        atp_ref[orows[c], :] = x[c][:, :LANE].astype(BF16)
        u0_ref[orows[c], :] = x[c][:, LANE:]
        rp_ref[orows[c], :] = (rt[rows[c]] + z[c][:, :LANE]).astype(BF16)
        y0_ref[orows[c], :] = z[c][:, LANE:] + arkv[c]


def _rwkv_prep(p, ps, pa, w2p, a2p, vmix):
    l = p.shape[0]
    t = min(RT, l)
    nsub = max(t // RSUB, 1)
    npair = RW // LANE
    has_vmix = vmix is not None

    def main(off):
        cb = off // LANE
        return pl.BlockSpec((t, LANE), lambda q, i: (i, cb + q))

    def main1(off):
        cb = off // LANE
        return pl.BlockSpec((t, LANE), lambda q, i: (i, cb))

    par = pl.BlockSpec((SUB, LANE), lambda q, i: (0, q))
    lora = pl.BlockSpec((LANE, LANE), lambda q, i: (0, q))
    row = pl.BlockSpec((t, LANE), lambda q, i: (i, q))

    in_specs = [main(E_R), main(E_K), main(E_V), main(E_G), main1(S_WD), main1(S_AD), par, lora, lora]
    args = [p] * 4 + [ps] * 2 + [pa, w2p, a2p]
    if has_vmix:
        tl, v2p, p_first = vmix
        in_specs += [pl.BlockSpec((t, LANE), lambda q, i: (i, 0)), lora, main(E_V)]
        args += [tl, v2p, p_first]
    dl_rows = (t // RCHUNK) * SUB
    out_specs = [row] * 9 + [pl.BlockSpec((dl_rows, LANE), lambda q, i: (i, q))]
    sd = jax.ShapeDtypeStruct
    out_shape = ([sd((l, RW), BF16)] * 5 + [sd((l, RW), F32)] * 4
                 + [sd((l // RCHUNK * SUB, RW), F32)])
    return pl.pallas_call(
        functools.partial(_rwkv_prep_kernel, has_vmix=has_vmix, nsub=nsub),
        grid=(npair, l // t),
        in_specs=in_specs,
        out_specs=out_specs,
        out_shape=out_shape,
        compiler_params=_params(dimension_semantics=("parallel", "parallel")),
        name="rwkv_prep",
    )(*args)


def _rwkv_scan_kernel(atp_ref, rp_ref, bh_ref, kh_ref, vb_ref, u0_ref, y0_ref,
                      bonus_ref, gs_ref, dl_ref, pa_ref, o_ref, s_ref, y_ref):
    @pl.when(pl.program_id(0) == 0)
    def _():
        s_ref[...] = jnp.zeros_like(s_ref)

    t = atp_ref.shape[0]
    npair = RW // LANE
    bd = _idiv(_iota((LANE, LANE), 0), RH) == _idiv(_iota((LANE, LANE), 1), RH)
    hmean = _head_ones(LANE, RH, 1.0 / RH)

    pairs = range(npair)
    lanes = [slice(q * LANE, (q + 1) * LANE) for q in pairs]
    for c in range(t // RCHUNK):
        rows = slice(c * RCHUNK, (c + 1) * RCHUNK)
        s_old = [s_ref[q] for q in pairs]
        o = [_dg(jnp.concatenate([atp_ref[rows, lanes[q]], rp_ref[rows, lanes[q]]], axis=0),
                 s_old[q].astype(BF16), NT) for q in pairs]
        ds = []
        for q in pairs:
            u = o[q][:RCHUNK] + u0_ref[rows, lanes[q]]
            y_ref[rows, lanes[q]] = o[q][RCHUNK:] + y0_ref[rows, lanes[q]]
            uv = jnp.concatenate([u.astype(BF16), vb_ref[rows, lanes[q]]], axis=0)
            bk = jnp.concatenate([bh_ref[rows, lanes[q]], kh_ref[rows, lanes[q]]], axis=0)
            ds.append(_dg(uv, bk, TN))
        for q in pairs:
            dl = dl_ref[c * SUB:c * SUB + 1, lanes[q]]
            s_ref[q] = s_old[q] * dl + jnp.where(bd, ds[q], 0.0)

    for q in range(npair):
        sl = slice(q * LANE, (q + 1) * LANE)
        y = y_ref[:, sl]
        yh, yl = _split2(y)
        mean = _dg(yh, hmean) + _dg(yl, hmean)
        d = y - mean
        var = _bdot(d * d, hmean)
        yn = d * lax.rsqrt(var + RWKV_GN_EPS) * pa_ref[5:6, sl] + pa_ref[6:7, sl]
        o_ref[:, sl] = ((yn + bonus_ref[:, sl]) * gs_ref[:, sl]).astype(BF16)


def _rwkv_scan(atp, rp, bh, kh, vb, u0, y0, bonus, gs, dl, pa):
    l = atp.shape[0]
    t = min(RST, l)
    row = pl.BlockSpec((t, RW), lambda i: (i, 0))
    return pl.pallas_call(
        _rwkv_scan_kernel,
        grid=(l // t,),
        in_specs=[row] * 9 + [pl.BlockSpec(((t // RCHUNK) * SUB, RW), lambda i: (i, 0)),
                              pl.BlockSpec((SUB, RW), lambda i: (0, 0))],
        out_specs=row,
        out_shape=jax.ShapeDtypeStruct((l, RW), BF16),
        scratch_shapes=[pltpu.VMEM((RW // LANE, LANE, LANE), F32), pltpu.VMEM((t, RW), F32)],
        compiler_params=_params(dimension_semantics=("arbitrary",)),
        name="rwkv_scan",
    )(atp, rp, bh, kh, vb, u0, y0, bonus, gs, dl, pa)


def _ssd_conv(main_ref, halo_ref, wb_ref, buf_ref, first):
    q = main_ref.shape[0]
    buf_ref[0:SUB, :] = jnp.where(first, 0.0, halo_ref[...])
    buf_ref[SUB:SUB + q, :] = main_ref[...]
    acc = wb_ref[SCONV:SCONV + 1, :]
    for j in range(SCONV):
        acc = acc + buf_ref[pl.ds(SUB - (SCONV - 1) + j, q), :] * wb_ref[j:j + 1, :]
    return _silu(acc)


def _ssd_kernel(z_ref, x_ref, b_ref, c_ref, dt_ref, xh_ref, bhalo_ref, chalo_ref,
                cwx_ref, cwb_ref, cwc_ref, hp_ref, dx_ref, nw_ref,
                o_ref, prev_ref, xbuf_ref, bbuf_ref, cbuf_ref, ybuf_ref):
    g = pl.program_id(0)
    first = pl.program_id(1) == 0
    q = z_ref.shape[0]

    @pl.when(first)
    def _():
        prev_ref[...] = jnp.zeros_like(prev_ref)

    xs = _ssd_conv(x_ref, xh_ref, cwx_ref, xbuf_ref, first)
    bm = _ssd_conv(b_ref, bhalo_ref, cwb_ref, bbuf_ref, first)
    cm = _ssd_conv(c_ref, chalo_ref, cwc_ref, cbuf_ref, first)

    sl_r = _iota((LANE, LANE), 0)
    sl_c = _iota((LANE, LANE), 1)
    sel = jnp.where((sl_r == sl_c + g * SHG) & (sl_c < SHG), 1.0, 0.0).astype(BF16)
    dt_raw = _dot_sel(dt_ref[...], sel)
    dt = _softplus(dt_raw + hp_ref[0:1, :])
    a_neg = -jnp.exp(hp_ref[1:2, :])
    acs = _seg_cumsum(dt * a_neg, q)
    acs_t = acs.T
    acs_last = acs[q - 1:q, :]

    expand = jnp.where(_iota((LANE, SGW), 0) == _idiv(_iota((LANE, SGW), 1), SHEAD), 1.0, 0.0).astype(BF16)
    expand_t = jnp.where(_iota((SGW, LANE), 1) == _idiv(_iota((SGW, LANE), 0), SHEAD), 1.0, 0.0).astype(BF16)
    dt_e = _dot_sel(dt, expand)
    eacs_e = _dot_sel(jnp.exp(acs), expand)
    ds_e = _dot_sel(jnp.exp(acs_last - acs), expand)
    xc = xs * dt_e

    cmb = cm.astype(BF16)
    bmb = bm.astype(BF16)
    scores = _dg(cmb, bmb, NT)
    prev = prev_ref[...]
    y_off = _dg(cmb, prev.astype(BF16), NT) * eacs_e
    states = _dg((xc * ds_e).astype(BF16), bmb, TN)
    eh, em, el = _split3(jnp.exp(acs_t))
    cdec = _dg(expand_t, eh) + (_dg(expand_t, em) + _dg(expand_t, el))
    prev_ref[...] = prev * cdec[:, q - 1:q] + states

    causal = _iota((q, q), 1) <= _iota((q, q), 0)
    for h in range(SHG):
        diff = acs[:, h:h + 1] - acs_t[h:h + 1, :]
        lm = jnp.exp(jnp.where(causal, diff, -jnp.inf))
        ybuf_ref[:, h * SHEAD:(h + 1) * SHEAD] = _bdot(scores * lm, xc[:, h * SHEAD:(h + 1) * SHEAD])
    y = ybuf_ref[...] + y_off + dx_ref[0:1, :] * xs
    y = y * _silu(z_ref[...])
    ms = jnp.mean(y * y, axis=-1, keepdims=True)
    o_ref[...] = (y * lax.rsqrt(ms + EPS) * nw_ref[0:1, :]).astype(BF16)


def _ssd(p, ps, cw8, hp, dx8, nw8):
    l = p.shape[0]
    q = SCHUNK
    hb = q // SUB

    def halo_idx(i):
        return jnp.maximum(i * hb - 1, 0)

    zb, xb = T_Z // SGW, T_X // SGW
    bb, cb, db = T_B // LANE, T_C // LANE, S_DT // LANE
    in_specs = [
        pl.BlockSpec((q, SGW), lambda g, i: (i, zb + g)),
        pl.BlockSpec((q, SGW), lambda g, i: (i, xb + g)),
        pl.BlockSpec((q, LANE), lambda g, i: (i, bb + g)),
        pl.BlockSpec((q, LANE), lambda g, i: (i, cb + g)),
        pl.BlockSpec((q, LANE), lambda g, i: (i, db)),
        pl.BlockSpec((SUB, SGW), lambda g, i: (halo_idx(i), xb + g)),
        pl.BlockSpec((SUB, LANE), lambda g, i: (halo_idx(i), bb + g)),
        pl.BlockSpec((SUB, LANE), lambda g, i: (halo_idx(i), cb + g)),
        pl.BlockSpec((SUB, SGW), lambda g, i: (0, g)),
        pl.BlockSpec((SUB, LANE), lambda g, i: (0, SW // LANE + g)),
        pl.BlockSpec((SUB, LANE), lambda g, i: (0, SW // LANE + SGROUPS + g)),
        pl.BlockSpec((SUB, LANE), lambda g, i: (0, g)),
        pl.BlockSpec((SUB, SGW), lambda g, i: (0, g)),
        pl.BlockSpec((SUB, SGW), lambda g, i: (0, g)),
    ]
    return pl.pallas_call(
        _ssd_kernel,
        grid=(SGROUPS, l // q),
        in_specs=in_specs,
        out_specs=pl.BlockSpec((q, SGW), lambda g, i: (i, g)),
        out_shape=jax.ShapeDtypeStruct((l, SW), BF16),
        scratch_shapes=[pltpu.VMEM((SGW, SSTATE), F32),
                        pltpu.VMEM((q + SUB, SGW), F32),
                        pltpu.VMEM((q + SUB, LANE), F32),
                        pltpu.VMEM((q + SUB, LANE), F32),
                        pltpu.VMEM((q, SGW), F32)],
        compiler_params=_params(dimension_semantics=("parallel", "arbitrary")),
        name="ssd",
    )(p, p, p, p, ps, p, p, p, cw8, cw8, cw8, hp, dx8, nw8)


GT = 256


def _gla_kernel(q_ref, k_ref, v_ref, g_ref, gd_ref, gkw_ref, gkb_ref, nw_ref, o_ref, st_ref, oi_ref):
    @pl.when(pl.program_id(1) == 0)
    def _():
        st_ref[...] = jnp.zeros_like(st_ref)

    t = q_ref.shape[0]
    nchunk = t // GCHUNK
    gk = _log_sigmoid(_dot3(gd_ref[...], gkw_ref[...]) + gkb_ref[0:1, :]) * (1.0 / 16.0)
    bc = _seg_cumsum(gk, GCHUNK)
    mid = _chunk_row(bc, GCHUNK, GCHUNK // 2)
    last = _chunk_row(bc, GCHUNK, GCHUNK - 1)
    qs = q_ref[...] * (GHK ** -0.5)
    k = k_ref[...]
    vb = v_ref[...].astype(BF16)

    ri = _iota((t, t), 0)
    ci = _iota((t, t), 1)
    incl = (_idiv(ri, GCHUNK) == _idiv(ci, GCHUNK)) & (ci <= ri)
    attn = _bdot(qs * jnp.exp(bc - mid), k * jnp.exp(mid - bc), NT)
    attn = jnp.where(incl, attn, 0.0)
    o_intra = _bdot(attn, vb)

    q_in = (qs * jnp.exp(bc)).astype(BF16)
    k_st = (k * jnp.exp(last - bc)).astype(BF16)
    dlast = jnp.exp(last)
    for c in range(nchunk):
        rows = slice(c * GCHUNK, (c + 1) * GCHUNK)
        st = st_ref[...]
        oi_ref[rows, :] = _dg(q_in[rows], st.astype(BF16), NT)
        st_ref[...] = st * dlast[c * GCHUNK:c * GCHUNK + 1, :] + _dg(vb[rows], k_st[rows], TN)
    o = o_intra + oi_ref[...]
    ms = jnp.mean(o * o, axis=-1, keepdims=True)
    o = o * lax.rsqrt(ms + EPS) * nw_ref[0:1, :]
    o_ref[...] = (o * _silu(g_ref[...])).astype(BF16)


def _gla(p, pgd, gkwp, gkb8, nw8):
    l = p.shape[0]
    t = min(GT, l)
    return pl.pallas_call(
        _gla_kernel,
        grid=(GHEADS, l // t),
        in_specs=[pl.BlockSpec((t, GHK), lambda h, i: (i, O_Q // GHK + h)),
                  pl.BlockSpec((t, GHK), lambda h, i: (i, O_K // GHK + h)),
                  pl.BlockSpec((t, GHV), lambda h, i: (i, O_V // GHV + h)),
                  pl.BlockSpec((t, GHV), lambda h, i: (i, O_G // GHV + h)),
                  pl.BlockSpec((t, LANE), lambda h, i: (i, 0)),
                  pl.BlockSpec((LANE, GHK), lambda h, i: (0, h)),
                  pl.BlockSpec((SUB, GHK), lambda h, i: (0, h)),
                  pl.BlockSpec((SUB, GHV), lambda h, i: (0, 0))],
        out_specs=pl.BlockSpec((t, GHV), lambda h, i: (i, h)),
        out_shape=jax.ShapeDtypeStruct((l, GVAL), BF16),
        scratch_shapes=[pltpu.VMEM((GHV, GHK), F32), pltpu.VMEM((t, GHV), F32)],
        compiler_params=_params(dimension_semantics=("parallel", "arbitrary")),
        name="gla",
    )(p, p, p, p, pgd, gkwp, gkb8, nw8)


def _rows8(*rows):
    n = rows[0].shape[0]
    parts = [r.astype(F32)[None, :] for r in rows]
    if len(rows) < SUB:
        parts.append(jnp.zeros((SUB - len(rows), n), F32))
    return jnp.concatenate(parts, axis=0)


def _pad_cols(w, n):
    return jnp.concatenate([w, jnp.zeros((w.shape[0], n - w.shape[1]), w.dtype)], axis=1)


def _pad_rows(w, n):
    return jnp.concatenate([w, jnp.zeros((n - w.shape[0], w.shape[1]), w.dtype)], axis=0)


def _lane_shift_kernel(a_ref, b_ref, o_ref):
    half = LANE // 2
    hi = _iota(a_ref.shape, 1) >= half
    o_ref[...] = pltpu.roll(jnp.where(hi, a_ref[...], b_ref[...]), half, 1).astype(BF16)


def _even_zxbc_weight(w):
    d = w.shape[0]
    first = E_ZXBC // LANE
    assert E_ZXBC - first * LANE == LANE // 2
    return pl.pallas_call(
        _lane_shift_kernel,
        grid=(T_COLS // LANE,),
        in_specs=[pl.BlockSpec((d, LANE), lambda j: (0, first + j)),
                  pl.BlockSpec((d, LANE), lambda j: (0, first + j + 1))],
        out_specs=pl.BlockSpec((d, LANE), lambda j: (0, j)),
        out_shape=jax.ShapeDtypeStruct((d, T_COLS), BF16),
        compiler_params=_params(dimension_semantics=("parallel",)),
        name="weight_lane_shift",
    )(w, w)


def _even_small_weight(w):
    wd = w[:, E_MAIN:E_MAIN + DECAY_LORA]
    ad = w[:, E_MAIN + DECAY_LORA:E_ZXBC]
    dt = w[:, E_ZXBC + T_COLS:]
    parts = [_pad_cols(wd, LANE), _pad_cols(ad, LANE), _pad_cols(dt, 2 * LANE)]
    return jnp.concatenate(parts, axis=1).astype(BF16)


def kernel(x, norm_w, final_norm_w, w_in_even, w_out_even, rwkv_mu, rwkv_w0, rwkv_w2, rwkv_a0, rwkv_a2, rwkv_k_k, rwkv_k_a, rwkv_r_k, rwkv_ln_w, rwkv_ln_b, rwkv_v0, rwkv_v1, rwkv_v2, ssm_conv_w, ssm_conv_b, ssm_dt_bias, ssm_A_log, ssm_D, ssm_norm_w, w_in_odd, w_out_odd, gla_gk_w, gla_gk_b, gla_norm_w):
    bsz, l, d = x.shape
    depth = norm_w.shape[0]
    w_out_even_bf = w_out_even.astype(BF16)
    w_out_odd_bf = w_out_odd.astype(BF16)
    outs = []
    for b in range(bsz):
        res = x[b]
        h = _rmsnorm(res, _rows8(norm_w[0]))
        p_first = None
        for layer in range(depth):
            i = layer // 2
            last = layer == depth - 1
            nw_next = _rows8(final_norm_w if last else norm_w[layer + 1])
            if layer % 2 == 0:
                mu = rwkv_mu[i]
                zpad = jnp.zeros((LANE - DECAY_LORA,), F32)
                mu_small = jnp.concatenate([mu[E_MAIN:E_MAIN + DECAY_LORA], zpad, mu[E_MAIN + DECAY_LORA:],
                                            zpad, jnp.zeros((S_COLS - 2 * LANE,), F32)])
                p = _inproj(h, w_in_even[i], E_MAIN, tn=512, mu8=_rows8(mu[:E_MAIN]))
                pt = _inproj(h, _even_zxbc_weight(w_in_even[i]), T_COLS, tn=512)
                ps = _inproj(h, _even_small_weight(w_in_even[i]), S_COLS, tn=S_COLS, mu8=_rows8(mu_small))
                v0 = rwkv_v0[i - 1] if i > 0 else jnp.zeros((RW,), F32)
                pa = _rows8(rwkv_w0[i], rwkv_a0[i], rwkv_k_k[i], rwkv_k_a[i],
                            rwkv_r_k[i].reshape(RW), rwkv_ln_w[i], rwkv_ln_b[i], v0)
                w2p = _pad_rows(rwkv_w2[i], LANE)
                a2p = _pad_rows(rwkv_a2[i], LANE)
                if i == 0:
                    vmix = None
                    p_first = p
                else:
                    tl = _vlora(p, _pad_cols(rwkv_v1[i - 1], LANE))
                    vmix = (tl, _pad_rows(rwkv_v2[i - 1], LANE), p_first)
                (atp, rp, bh, kh, vb, u0, y0, bonus, gs, dl) = _rwkv_prep(p, ps, pa, w2p, a2p, vmix)
                y_a = _rwkv_scan(atp, rp, bh, kh, vb, u0, y0, bonus, gs, dl, pa)

                cw8 = jnp.concatenate([ssm_conv_w[i], ssm_conv_b[i][None, :],
                                       jnp.zeros((SUB - SCONV - 1, ssm_conv_w.shape[2]), F32)], axis=0)
                pad_g = lambda a: _pad_cols(a.reshape(SGROUPS, SHG), LANE).reshape(SGROUPS * LANE)
                hp = _rows8(pad_g(ssm_dt_bias[i]), pad_g(ssm_A_log[i]))
                dx8 = _rows8(jnp.repeat(ssm_D[i], SHEAD))
                y_b = _ssd(pt, ps, cw8, hp, dx8, _rows8(ssm_norm_w[i]))
                o = _outproj([y_a, y_b], w_out_even_bf[i], res, nw_next, last)
            else:
                p = _inproj(h, w_in_odd[i], O_MAIN, tn=512)
                pgd = _inproj(h, _pad_cols(w_in_odd[i][:, O_MAIN:], LANE), LANE, tn=LANE)
                y = _gla(p, pgd, _pad_rows(gla_gk_w[i], LANE), _rows8(gla_gk_b[i]), _rows8(gla_norm_w[i]))
                o = _outproj([y], w_out_odd_bf[i], res, nw_next, last)
            if last:
                res = o[0]
            else:
                res, h = o
        outs.append(res)
    return jnp.stack(outs).astype(x.dtype)
```

```python
import functools

import jax
import jax.numpy as jnp
from jax import lax
from jax.experimental import pallas as pl
from jax.experimental.pallas import tpu as pltpu

F32 = jnp.float32
BF16 = jnp.bfloat16

D_MODEL = 2048
EPS = 1e-5
RW = 2048
RH = 64
DECAY_LORA = 96
AAA_LORA = 96
MV_LORA = 64
RWKV_GN_EPS = 64e-5
RCHUNK = 64
SW = 2048
SHEAD = 64
SGROUPS = 4
SSTATE = 128
SCONV = 4
SCHUNK = 128
SGW = SW // SGROUPS
SHG = SGW // SHEAD
GHEADS = 4
GKEY = 1024
GVAL = 2048
GHK = 256
GHV = 512
GRANK = 16
GCHUNK = 64

LANE = 128
SUB = 8
VMEM_LIMIT = 56 * 1024 * 1024

E_R, E_K, E_V, E_G = 0, 2048, 4096, 6144
E_MAIN = 8192
E_ZXBC = 8384
T_Z, T_X = 0, 2048
T_B, T_C = 4096, 4608
T_COLS = 5120
S_WD, S_AD, S_DT = 0, 128, 256
S_COLS = 512
O_Q, O_K, O_V, O_G = 0, 1024, 2048, 4096
O_MAIN = 6144

NN = (((1,), (0,)), ((), ()))
NT = (((1,), (1,)), ((), ()))
TN = (((0,), (0,)), ((), ()))


def _dg(a, b, dims=NN):
    return lax.dot_general(a, b, dims, preferred_element_type=F32)


def _bdot(a, b, dims=NN):
    return _dg(a.astype(BF16), b.astype(BF16), dims)


def _split2(x):
    hi = x.astype(BF16)
    lo = (x - hi.astype(F32)).astype(BF16)
    return hi, lo


def _split3(x):
    hi = x.astype(BF16)
    r = x - hi.astype(F32)
    mid = r.astype(BF16)
    lo = (r - mid.astype(F32)).astype(BF16)
    return hi, mid, lo


def _dot3(a, b, dims=NN):
    ah, al = _split2(a)
    bh, bl = _split2(b)
    return _dg(ah, bh, dims) + (_dg(ah, bl, dims) + _dg(al, bh, dims))


def _dot_sel(a, sel_bf16, dims=NN):
    ah, am, al = _split3(a)
    return _dg(ah, sel_bf16, dims) + (_dg(am, sel_bf16, dims) + _dg(al, sel_bf16, dims))


def _sigmoid(x):
    return 1.0 / (1.0 + jnp.exp(-x))


def _silu(x):
    return x * _sigmoid(x)


def _log_sigmoid(x):
    return jnp.minimum(x, 0.0) - jnp.log(1.0 + jnp.exp(-jnp.abs(x)))


def _softplus(x):
    return jnp.maximum(x, 0.0) + jnp.log(1.0 + jnp.exp(-jnp.abs(x)))


def _iota(shape, axis):
    return lax.broadcasted_iota(jnp.int32, shape, axis)


def _idiv(x, n):
    return lax.shift_right_logical(x, n.bit_length() - 1)


def _seg_cumsum(x, chunk):
    rin = _iota(x.shape, 0) & (chunk - 1)
    s = 1
    while s < chunk:
        x = x + jnp.where(rin >= s, pltpu.roll(x, s, 0), 0.0)
        s *= 2
    return x


def _chunk_row(x, chunk, r):
    t, w = x.shape
    x3 = x.reshape(t // chunk, chunk, w)
    return jnp.broadcast_to(x3[:, r:r + 1, :], x3.shape).reshape(t, w)


def _head_ones(n, head, scale=1.0):
    r = _idiv(_iota((n, n), 0), head)
    c = _idiv(_iota((n, n), 1), head)
    return jnp.where(r == c, scale, 0.0).astype(BF16)


def _params(**kw):
    return pltpu.CompilerParams(vmem_limit_bytes=VMEM_LIMIT, **kw)


def _rmsnorm_kernel(x_ref, w_ref, o_ref):
    x = x_ref[...]
    ms = jnp.mean(x * x, axis=-1, keepdims=True)
    o_ref[...] = (x * lax.rsqrt(ms + EPS) * w_ref[0:1, :]).astype(BF16)


def _rmsnorm(x, w8, tm=512):
    l, d = x.shape
    return pl.pallas_call(
        _rmsnorm_kernel,
        grid=(l // tm,),
        in_specs=[pl.BlockSpec((tm, d), lambda i: (i, 0)),
                  pl.BlockSpec((SUB, d), lambda i: (0, 0))],
        out_specs=pl.BlockSpec((tm, d), lambda i: (i, 0)),
        out_shape=jax.ShapeDtypeStruct((l, d), BF16),
        compiler_params=_params(dimension_semantics=("parallel",)),
        name="rmsnorm",
    )(x, w8)


HALO = 16


def _inproj_kernel(*refs, shift):
    if shift:
        h_ref, hh_ref, w_ref, mu_ref, o_ref = refs
    else:
        h_ref, w_ref, o_ref = refs
    w = w_ref[...].astype(BF16)
    p = jnp.dot(h_ref[...], w, preferred_element_type=F32)
    if shift:
        prev = jnp.dot(hh_ref[...], w, preferred_element_type=F32)[HALO - 1:HALO, :]
        prev = jnp.where(pl.program_id(0) == 0, 0.0, prev)
        pprev = jnp.where(_iota(p.shape, 0) == 0, prev, pltpu.roll(p, 1, 0))
        p = p + (pprev - p) * mu_ref[0:1, :]
    o_ref[...] = p


def _layer_spec(w, layer, block, index_map):
    if w.ndim == 2:
        return pl.BlockSpec(block, index_map)
    return pl.BlockSpec((None,) + block, lambda *ids: (layer,) + index_map(*ids))


def _inproj(h, w, n, tn, mu8=None, layer=0, tm=2048):
    l, d = h.shape
    tm = min(tm, l)
    shift = mu8 is not None
    in_specs = [pl.BlockSpec((tm, d), lambda i, j: (i, 0))]
    args = [h]
    if shift:
        hb = tm // HALO
        in_specs.append(pl.BlockSpec((HALO, d), lambda i, j: (jnp.maximum(i * hb - 1, 0), 0)))
        args.append(h)
    in_specs.append(_layer_spec(w, layer, (d, tn), lambda i, j: (0, j)))
    args.append(w)
    if shift:
        in_specs.append(pl.BlockSpec((SUB, tn), lambda i, j: (0, j)))
        args.append(mu8)
    return pl.pallas_call(
        functools.partial(_inproj_kernel, shift=shift),
        grid=(l // tm, n // tn),
        in_specs=in_specs,
        out_specs=pl.BlockSpec((tm, tn), lambda i, j: (i, j)),
        out_shape=jax.ShapeDtypeStruct((l, n), F32),
        compiler_params=_params(dimension_semantics=("parallel", "parallel")),
        name="inproj",
    )(*args)


def _outproj_kernel(*refs, n_in, final):
    y_refs = refs[:n_in]
    w_ref, res_ref, nw_ref = refs[n_in:n_in + 3]
    outs = refs[n_in + 3:]
    r = res_ref[...]
    for s in range(n_in):
        kw = y_refs[s].shape[1]
        r = r + jnp.dot(y_refs[s][...], w_ref[s * kw:(s + 1) * kw, :], preferred_element_type=F32)
    ms = jnp.mean(r * r, axis=-1, keepdims=True)
    hn = r * lax.rsqrt(ms + EPS) * nw_ref[0:1, :]
    if final:
        outs[0][...] = hn
    else:
        outs[0][...] = r
        outs[1][...] = hn.astype(BF16)


def _outproj(ys, w, layer, res, nw8, final, tm=512):
    l, d = res.shape
    n_in = len(ys)
    kdim = w.shape[1]
    tm = min(tm, l)
    in_specs = [pl.BlockSpec((tm, y.shape[1]), lambda i: (i, 0)) for y in ys]
    in_specs += [pl.BlockSpec((None, kdim, d), lambda i: (layer, 0, 0), pipeline_mode=pl.Buffered(1)),
                 pl.BlockSpec((tm, d), lambda i: (i, 0)),
                 pl.BlockSpec((SUB, d), lambda i: (0, 0))]
    row_spec = pl.BlockSpec((tm, d), lambda i: (i, 0))
    if final:
        out_specs = [row_spec]
        out_shape = [jax.ShapeDtypeStruct((l, d), F32)]
    else:
        out_specs = [row_spec, row_spec]
        out_shape = [jax.ShapeDtypeStruct((l, d), F32), jax.ShapeDtypeStruct((l, d), BF16)]
    return pl.pallas_call(
        functools.partial(_outproj_kernel, n_in=n_in, final=final),
        grid=(l // tm,),
        in_specs=in_specs,
        out_specs=out_specs,
        out_shape=out_shape,
        compiler_params=_params(dimension_semantics=("parallel",)),
        name="outproj",
    )(*ys, w, res, nw8)


def _vlora_kernel(v_ref, v1_ref, o_ref):
    o_ref[...] = _dot3(v_ref[...], v1_ref[...])


def _vlora(p, v1p, t=512):
    l = p.shape[0]
    t = min(t, l)
    cb = E_V // RW
    return pl.pallas_call(
        _vlora_kernel,
        grid=(l // t,),
        in_specs=[pl.BlockSpec((t, RW), lambda i: (i, cb)),
                  pl.BlockSpec((RW, LANE), lambda i: (0, 0))],
        out_specs=pl.BlockSpec((t, LANE), lambda i: (i, 0)),
        out_shape=jax.ShapeDtypeStruct((l, LANE), F32),
        compiler_params=_params(dimension_semantics=("parallel",)),
        name="rwkv_vlora",
    )(p, v1p)


RT = 1024
RSUB = 1024
RST = 256


def _rwkv_prep_kernel(*refs, has_vmix, nsub):
    (r_ref, k_ref, v_ref, g_ref, wd_ref, ad_ref, pa_ref, w2_ref, a2_ref) = refs[:9]
    pos = 9
    if has_vmix:
        tl_ref, v2_ref, vf_ref = refs[pos:pos + 3]
        pos += 3
    orefs = refs[pos:]
    st = r_ref.shape[0] // nsub
    for j in range(nsub):
        rs = slice(j * st, (j + 1) * st)
        vmix_j = (tl_ref[rs, :], v2_ref[...], vf_ref[rs, :]) if has_vmix else None
        _rwkv_prep_subtile(r_ref[rs, :], k_ref[rs, :], v_ref[rs, :], g_ref[rs, :], wd_ref[rs, :],
                           ad_ref[rs, :], pa_ref, w2_ref[...], a2_ref[...], vmix_j, orefs, j, st)


def _rwkv_prep_subtile(r, k, v, g, wd, ad, pa_ref, w2, a2, vmix, orefs, j, st):
    (atp_ref, rp_ref, bh_ref, kh2_ref, vb_ref, u0_ref, y0_ref, bonus_ref, gs_ref, dl_ref) = orefs
    rs = slice(j * st, (j + 1) * st)
    nchunk = st // RCHUNK
    w0, a0, k_k, k_a = (pa_ref[i:i + 1, :] for i in range(4))
    r_k, v0 = pa_ref[4:5, :], pa_ref[7:8, :]

    w_log = _log_sigmoid(w0 + _dot3(jnp.tanh(wd), w2)) - 0.5
    logd = -jnp.exp(w_log)
    a = _sigmoid(a0 + _dot3(ad, a2))
    if vmix is not None:
        tl, v2, vf = vmix
        v = v + (vf - v) * _sigmoid(v0 + _dot3(tl, v2))

    hm = _head_ones(LANE, RH)
    kk = k * k_k
    kk2 = kk * kk
    k2h, k2l = _split2(kk2)
    ss = _dg(k2h, hm) + _dg(k2l, hm)
    kk = kk * lax.rsqrt(jnp.maximum(ss, 1e-24))
    kh = k * (1.0 + (a - 1.0) * k_a)
    av = -kk
    bv = kk * a

    rkh, rkl = _split2(r * kh * r_k)
    bonus_ref[rs, :] = (_dg(rkh, hm) + _dg(rkl, hm)) * v
    gs_ref[rs, :] = _silu(g)

    cum = _seg_cumsum(logd, RCHUNK)
    cl = _chunk_row(cum, RCHUNK, RCHUNK - 1)
    e_n = jnp.exp(-cum)
    e_l = jnp.exp(cl - cum)
    at = av * jnp.exp(cum - logd)
    rt = r * jnp.exp(cum)
    bt = bv * e_n
    kt = kh * e_n
    bh_ref[rs, :] = (bv * e_l).astype(BF16)
    kh2_ref[rs, :] = (kh * e_l).astype(BF16)
    vb_ref[rs, :] = v.astype(BF16)
    dl_ref[j * nchunk * SUB:(j + 1) * nchunk * SUB, :] = (
        jnp.exp(cl).reshape(nchunk, RCHUNK, LANE)[:, 0:SUB, :].reshape(nchunk * SUB, LANE))

    lane = _iota((RCHUNK, LANE), 1)
    si = lane & (RH - 1)
    ti = _iota((RCHUNK, LANE), 0)
    strict = si < ti
    incl = si <= ti
    eye = jnp.where(si == ti, 1.0, 0.0)
    head0 = lane < RH

    def stack(xc):
        xc = xc.astype(BF16)
        zero = jnp.zeros_like(xc)
        return jnp.concatenate([jnp.where(head0, xc, zero), jnp.where(head0, zero, xc)], axis=0)

    chunks = range(nchunk)
    rows = [slice(c * RCHUNK, (c + 1) * RCHUNK) for c in chunks]
    orows = [slice(j * st + c * RCHUNK, j * st + (c + 1) * RCHUNK) for c in chunks]
    ats = [stack(at[rw]) for rw in rows]
    vs = [stack(v[rw]) for rw in rows]
    amat = []
    for c in chunks:
        lhs = jnp.concatenate([at[rows[c]], rt[rows[c]]], axis=0).astype(BF16)
        rhs = jnp.concatenate([stack(bt[rows[c]]), stack(kt[rows[c]])], axis=0)
        amat.append(_dg(lhs, rhs, NT))
    a_ab = [jnp.where(strict, m[:RCHUNK, :LANE], 0.0) for m in amat]
    a_ak = [jnp.where(strict, m[:RCHUNK, LANE:], 0.0) for m in amat]
    a_rb = [jnp.where(incl, m[RCHUNK:, :LANE], 0.0).astype(BF16) for m in amat]
    a_rk = [jnp.where(incl, m[RCHUNK:, LANE:], 0.0) for m in amat]
    akv2 = [_dg(jnp.concatenate([a_ak[c], a_rk[c]], axis=0).astype(BF16), vs[c]) for c in chunks]
    akv = [m[:RCHUNK] for m in akv2]
    arkv = [m[RCHUNK:] for m in akv2]
    inv = [eye + m for m in a_ab]
    pw = [_dg(m.astype(BF16), stack(m)) for m in a_ab]
    s = 2
    while s < RCHUNK // 2:
        both = [_dg(pw[c].astype(BF16), jnp.concatenate([stack(pw[c]), stack(inv[c])], axis=1))
                for c in chunks]
        pw = [m[:, :LANE] for m in both]
        inv = [inv[c] + both[c][:, LANE:] for c in chunks]
        s *= 2
    inv = [inv[c] + _dg(pw[c].astype(BF16), stack(inv[c])) for c in chunks]
    x = [_dg(inv[c].astype(BF16), jnp.concatenate([ats[c], stack(akv[c])], axis=1)) for c in chunks]
    z = [_dg(a_rb[c], jnp.concatenate([stack(x[c][:, :LANE]), stack(x[c][:, LANE:])], axis=1))
         for c in chunks]
    for c in chunks:
        atp_ref[orows[c], :] = x[c][:, :LANE].astype(BF16)
        u0_ref[orows[c], :] = x[c][:, LANE:]
        rp_ref[orows[c], :] = (rt[rows[c]] + z[c][:, :LANE]).astype(BF16)
        y0_ref[orows[c], :] = z[c][:, LANE:] + arkv[c]


def _rwkv_prep(p, ps, pa, w2p, a2p, vmix):
    l = p.shape[0]
    t = min(RT, l)
    nsub = max(t // RSUB, 1)
    npair = RW // LANE
    has_vmix = vmix is not None

    def main(off):
        cb = off // LANE
        return pl.BlockSpec((t, LANE), lambda q, i: (i, cb + q))

    def main1(off):
        cb = off // LANE
        return pl.BlockSpec((t, LANE), lambda q, i: (i, cb))

    par = pl.BlockSpec((SUB, LANE), lambda q, i: (0, q))
    lora = pl.BlockSpec((LANE, LANE), lambda q, i: (0, q))
    row = pl.BlockSpec((t, LANE), lambda q, i: (i, q))

    in_specs = [main(E_R), main(E_K), main(E_V), main(E_G), main1(S_WD), main1(S_AD), par, lora, lora]
    args = [p] * 4 + [ps] * 2 + [pa, w2p, a2p]
    if has_vmix:
        tl, v2p, p_first = vmix
        in_specs += [pl.BlockSpec((t, LANE), lambda q, i: (i, 0)), lora, main(E_V)]
        args += [tl, v2p, p_first]
    dl_rows = (t // RCHUNK) * SUB
    out_specs = [row] * 9 + [pl.BlockSpec((dl_rows, LANE), lambda q, i: (i, q))]
    sd = jax.ShapeDtypeStruct
    out_shape = ([sd((l, RW), BF16)] * 5 + [sd((l, RW), F32)] * 4
                 + [sd((l // RCHUNK * SUB, RW), F32)])
    return pl.pallas_call(
        functools.partial(_rwkv_prep_kernel, has_vmix=has_vmix, nsub=nsub),
        grid=(npair, l // t),
        in_specs=in_specs,
        out_specs=out_specs,
        out_shape=out_shape,
        compiler_params=_params(dimension_semantics=("parallel", "parallel")),
        name="rwkv_prep",
    )(*args)


def _rwkv_scan_kernel(atp_ref, rp_ref, bh_ref, kh_ref, vb_ref, u0_ref, y0_ref,
                      bonus_ref, gs_ref, dl_ref, pa_ref, o_ref, s_ref, y_ref):
    @pl.when(pl.program_id(0) == 0)
    def _():
        s_ref[...] = jnp.zeros_like(s_ref)

    t = atp_ref.shape[0]
    npair = RW // LANE
    bd = _idiv(_iota((LANE, LANE), 0), RH) == _idiv(_iota((LANE, LANE), 1), RH)
    hmean = _head_ones(LANE, RH, 1.0 / RH)

    pairs = range(npair)
    lanes = [slice(q * LANE, (q + 1) * LANE) for q in pairs]
    for c in range(t // RCHUNK):
        rows = slice(c * RCHUNK, (c + 1) * RCHUNK)
        s_old = [s_ref[q] for q in pairs]
        o = [_dg(jnp.concatenate([atp_ref[rows, lanes[q]], rp_ref[rows, lanes[q]]], axis=0),
                 s_old[q].astype(BF16), NT) for q in pairs]
        ds = []
        for q in pairs:
            u = o[q][:RCHUNK] + u0_ref[rows, lanes[q]]
            y_ref[rows, lanes[q]] = o[q][RCHUNK:] + y0_ref[rows, lanes[q]]
            uv = jnp.concatenate([u.astype(BF16), vb_ref[rows, lanes[q]]], axis=0)
            bk = jnp.concatenate([bh_ref[rows, lanes[q]], kh_ref[rows, lanes[q]]], axis=0)
            ds.append(_dg(uv, bk, TN))
        for q in pairs:
            dl = dl_ref[c * SUB:c * SUB + 1, lanes[q]]
            s_ref[q] = s_old[q] * dl + jnp.where(bd, ds[q], 0.0)

    for q in range(npair):
        sl = slice(q * LANE, (q + 1) * LANE)
        y = y_ref[:, sl]
        yh, yl = _split2(y)
        mean = _dg(yh, hmean) + _dg(yl, hmean)
        d = y - mean
        var = _bdot(d * d, hmean)
        yn = d * lax.rsqrt(var + RWKV_GN_EPS) * pa_ref[5:6, sl] + pa_ref[6:7, sl]
        o_ref[:, sl] = ((yn + bonus_ref[:, sl]) * gs_ref[:, sl]).astype(BF16)


def _rwkv_scan(atp, rp, bh, kh, vb, u0, y0, bonus, gs, dl, pa):
    l = atp.shape[0]
    t = min(RST, l)
    row = pl.BlockSpec((t, RW), lambda i: (i, 0))
    return pl.pallas_call(
        _rwkv_scan_kernel,
        grid=(l // t,),
        in_specs=[row] * 9 + [pl.BlockSpec(((t // RCHUNK) * SUB, RW), lambda i: (i, 0)),
                              pl.BlockSpec((SUB, RW), lambda i: (0, 0))],
        out_specs=row,
        out_shape=jax.ShapeDtypeStruct((l, RW), BF16),
        scratch_shapes=[pltpu.VMEM((RW // LANE, LANE, LANE), F32), pltpu.VMEM((t, RW), F32)],
        compiler_params=_params(dimension_semantics=("arbitrary",)),
        name="rwkv_scan",
    )(atp, rp, bh, kh, vb, u0, y0, bonus, gs, dl, pa)


def _ssd_conv(main_ref, halo_ref, wb_ref, buf_ref, first):
    q = main_ref.shape[0]
    buf_ref[0:SUB, :] = jnp.where(first, 0.0, halo_ref[...])
    buf_ref[SUB:SUB + q, :] = main_ref[...]
    acc = wb_ref[SCONV:SCONV + 1, :]
    for j in range(SCONV):
        acc = acc + buf_ref[pl.ds(SUB - (SCONV - 1) + j, q), :] * wb_ref[j:j + 1, :]
    return _silu(acc)


def _ssd_kernel(z_ref, x_ref, b_ref, c_ref, dt_ref, xh_ref, bhalo_ref, chalo_ref,
                cwx_ref, cwb_ref, cwc_ref, hp_ref, dx_ref, nw_ref,
                o_ref, prev_ref, xbuf_ref, bbuf_ref, cbuf_ref, ybuf_ref):
    g = pl.program_id(0)
    first = pl.program_id(1) == 0
    q = z_ref.shape[0]

    @pl.when(first)
    def _():
        prev_ref[...] = jnp.zeros_like(prev_ref)

    xs = _ssd_conv(x_ref, xh_ref, cwx_ref, xbuf_ref, first)
    bm = _ssd_conv(b_ref, bhalo_ref, cwb_ref, bbuf_ref, first)
    cm = _ssd_conv(c_ref, chalo_ref, cwc_ref, cbuf_ref, first)

    sl_r = _iota((LANE, LANE), 0)
    sl_c = _iota((LANE, LANE), 1)
    sel = jnp.where((sl_r == sl_c + g * SHG) & (sl_c < SHG), 1.0, 0.0).astype(BF16)
    dt_raw = _dot_sel(dt_ref[...], sel)
    dt = _softplus(dt_raw + hp_ref[0:1, :])
    a_neg = -jnp.exp(hp_ref[1:2, :])
    acs = _seg_cumsum(dt * a_neg, q)
    acs_t = acs.T
    acs_last = acs[q - 1:q, :]

    expand = jnp.where(_iota((LANE, SGW), 0) == _idiv(_iota((LANE, SGW), 1), SHEAD), 1.0, 0.0).astype(BF16)
    expand_t = jnp.where(_iota((SGW, LANE), 1) == _idiv(_iota((SGW, LANE), 0), SHEAD), 1.0, 0.0).astype(BF16)
    dt_e = _dot_sel(dt, expand)
    eacs_e = _dot_sel(jnp.exp(acs), expand)
    ds_e = _dot_sel(jnp.exp(acs_last - acs), expand)
    xc = xs * dt_e

    cmb = cm.astype(BF16)
    bmb = bm.astype(BF16)
    scores = _dg(cmb, bmb, NT)
    prev = prev_ref[...]
    y_off = _dg(cmb, prev.astype(BF16), NT) * eacs_e
    states = _dg((xc * ds_e).astype(BF16), bmb, TN)
    eh, em, el = _split3(jnp.exp(acs_t))
    cdec = _dg(expand_t, eh) + (_dg(expand_t, em) + _dg(expand_t, el))
    prev_ref[...] = prev * cdec[:, q - 1:q] + states

    causal = _iota((q, q), 1) <= _iota((q, q), 0)
    for h in range(SHG):
        diff = acs[:, h:h + 1] - acs_t[h:h + 1, :]
        lm = jnp.exp(jnp.where(causal, diff, -jnp.inf))
        ybuf_ref[:, h * SHEAD:(h + 1) * SHEAD] = _bdot(scores * lm, xc[:, h * SHEAD:(h + 1) * SHEAD])
    y = ybuf_ref[...] + y_off + dx_ref[0:1, :] * xs
    y = y * _silu(z_ref[...])
    ms = jnp.mean(y * y, axis=-1, keepdims=True)
    o_ref[...] = (y * lax.rsqrt(ms + EPS) * nw_ref[0:1, :]).astype(BF16)


def _ssd(p, ps, cw8, hp, dx8, nw8):
    l = p.shape[0]
    q = SCHUNK
    hb = q // SUB

    def halo_idx(i):
        return jnp.maximum(i * hb - 1, 0)

    zb, xb = T_Z // SGW, T_X // SGW
    bb, cb, db = T_B // LANE, T_C // LANE, S_DT // LANE
    in_specs = [
        pl.BlockSpec((q, SGW), lambda g, i: (i, zb + g)),
        pl.BlockSpec((q, SGW), lambda g, i: (i, xb + g)),
        pl.BlockSpec((q, LANE), lambda g, i: (i, bb + g)),
        pl.BlockSpec((q, LANE), lambda g, i: (i, cb + g)),
        pl.BlockSpec((q, LANE), lambda g, i: (i, db)),
        pl.BlockSpec((SUB, SGW), lambda g, i: (halo_idx(i), xb + g)),
        pl.BlockSpec((SUB, LANE), lambda g, i: (halo_idx(i), bb + g)),
        pl.BlockSpec((SUB, LANE), lambda g, i: (halo_idx(i), cb + g)),
        pl.BlockSpec((SUB, SGW), lambda g, i: (0, g)),
        pl.BlockSpec((SUB, LANE), lambda g, i: (0, SW // LANE + g)),
        pl.BlockSpec((SUB, LANE), lambda g, i: (0, SW // LANE + SGROUPS + g)),
        pl.BlockSpec((SUB, LANE), lambda g, i: (0, g)),
        pl.BlockSpec((SUB, SGW), lambda g, i: (0, g)),
        pl.BlockSpec((SUB, SGW), lambda g, i: (0, g)),
    ]
    return pl.pallas_call(
        _ssd_kernel,
        grid=(SGROUPS, l // q),
        in_specs=in_specs,
        out_specs=pl.BlockSpec((q, SGW), lambda g, i: (i, g)),
        out_shape=jax.ShapeDtypeStruct((l, SW), BF16),
        scratch_shapes=[pltpu.VMEM((SGW, SSTATE), F32),
                        pltpu.VMEM((q + SUB, SGW), F32),
                        pltpu.VMEM((q + SUB, LANE), F32),
                        pltpu.VMEM((q + SUB, LANE), F32),
                        pltpu.VMEM((q, SGW), F32)],
        compiler_params=_params(dimension_semantics=("parallel", "arbitrary")),
        name="ssd",
    )(p, p, p, p, ps, p, p, p, cw8, cw8, cw8, hp, dx8, nw8)


GT = 256


def _gla_kernel(q_ref, k_ref, v_ref, g_ref, gd_ref, gkw_ref, gkb_ref, nw_ref, o_ref, st_ref, oi_ref):
    @pl.when(pl.program_id(1) == 0)
    def _():
        st_ref[...] = jnp.zeros_like(st_ref)

    t = q_ref.shape[0]
    nchunk = t // GCHUNK
    gk = _log_sigmoid(_dot3(gd_ref[...], gkw_ref[...]) + gkb_ref[0:1, :]) * (1.0 / 16.0)
    bc = _seg_cumsum(gk, GCHUNK)
    mid = _chunk_row(bc, GCHUNK, GCHUNK // 2)
    last = _chunk_row(bc, GCHUNK, GCHUNK - 1)
    qs = q_ref[...] * (GHK ** -0.5)
    k = k_ref[...]
    vb = v_ref[...].astype(BF16)

    ri = _iota((t, t), 0)
    ci = _iota((t, t), 1)
    incl = (_idiv(ri, GCHUNK) == _idiv(ci, GCHUNK)) & (ci <= ri)
    attn = _bdot(qs * jnp.exp(bc - mid), k * jnp.exp(mid - bc), NT)
    attn = jnp.where(incl, attn, 0.0)
    o_intra = _bdot(attn, vb)

    q_in = (qs * jnp.exp(bc)).astype(BF16)
    k_st = (k * jnp.exp(last - bc)).astype(BF16)
    dlast = jnp.exp(last)
    for c in range(nchunk):
        rows = slice(c * GCHUNK, (c + 1) * GCHUNK)
        st = st_ref[...]
        oi_ref[rows, :] = _dg(q_in[rows], st.astype(BF16), NT)
        st_ref[...] = st * dlast[c * GCHUNK:c * GCHUNK + 1, :] + _dg(vb[rows], k_st[rows], TN)
    o = o_intra + oi_ref[...]
    ms = jnp.mean(o * o, axis=-1, keepdims=True)
    o = o * lax.rsqrt(ms + EPS) * nw_ref[0:1, :]
    o_ref[...] = (o * _silu(g_ref[...])).astype(BF16)


def _gla(p, pgd, gkwp, gkb8, nw8):
    l = p.shape[0]
    t = min(GT, l)
    return pl.pallas_call(
        _gla_kernel,
        grid=(GHEADS, l // t),
        in_specs=[pl.BlockSpec((t, GHK), lambda h, i: (i, O_Q // GHK + h)),
                  pl.BlockSpec((t, GHK), lambda h, i: (i, O_K // GHK + h)),
                  pl.BlockSpec((t, GHV), lambda h, i: (i, O_V // GHV + h)),
                  pl.BlockSpec((t, GHV), lambda h, i: (i, O_G // GHV + h)),
                  pl.BlockSpec((t, LANE), lambda h, i: (i, 0)),
                  pl.BlockSpec((LANE, GHK), lambda h, i: (0, h)),
                  pl.BlockSpec((SUB, GHK), lambda h, i: (0, h)),
                  pl.BlockSpec((SUB, GHV), lambda h, i: (0, 0))],
        out_specs=pl.BlockSpec((t, GHV), lambda h, i: (i, h)),
        out_shape=jax.ShapeDtypeStruct((l, GVAL), BF16),
        scratch_shapes=[pltpu.VMEM((GHV, GHK), F32), pltpu.VMEM((t, GHV), F32)],
        compiler_params=_params(dimension_semantics=("parallel", "arbitrary")),
        name="gla",
    )(p, p, p, p, pgd, gkwp, gkb8, nw8)


def _rows8(*rows):
    n = rows[0].shape[0]
    parts = [r.astype(F32)[None, :] for r in rows]
    if len(rows) < SUB:
        parts.append(jnp.zeros((SUB - len(rows), n), F32))
    return jnp.concatenate(parts, axis=0)


def _pad_cols(w, n):
    return jnp.concatenate([w, jnp.zeros((w.shape[0], n - w.shape[1]), w.dtype)], axis=1)


def _pad_rows(w, n):
    return jnp.concatenate([w, jnp.zeros((n - w.shape[0], w.shape[1]), w.dtype)], axis=0)


def _lane_shift_kernel(a_ref, b_ref, o_ref):
    half = LANE // 2
    hi = _iota(a_ref.shape, 1) >= half
    o_ref[...] = pltpu.roll(jnp.where(hi, a_ref[...], b_ref[...]), half, 1).astype(BF16)


def _even_zxbc_weight(w, layer):
    d = w.shape[1]
    first = E_ZXBC // LANE
    assert E_ZXBC - first * LANE == LANE // 2
    return pl.pallas_call(
        _lane_shift_kernel,
        grid=(T_COLS // LANE,),
        in_specs=[_layer_spec(w, layer, (d, LANE), lambda j: (0, first + j)),
                  _layer_spec(w, layer, (d, LANE), lambda j: (0, first + j + 1))],
        out_specs=pl.BlockSpec((d, LANE), lambda j: (0, j)),
        out_shape=jax.ShapeDtypeStruct((d, T_COLS), BF16),
        compiler_params=_params(dimension_semantics=("parallel",)),
        name="weight_lane_shift",
    )(w, w)


def _even_small_weight(w):
    wd = w[:, E_MAIN:E_MAIN + DECAY_LORA]
    ad = w[:, E_MAIN + DECAY_LORA:E_ZXBC]
    dt = w[:, E_ZXBC + T_COLS:]
    parts = [_pad_cols(wd, LANE), _pad_cols(ad, LANE), _pad_cols(dt, 2 * LANE)]
    return jnp.concatenate(parts, axis=1).astype(BF16)


def kernel(x, norm_w, final_norm_w, w_in_even, w_out_even, rwkv_mu, rwkv_w0, rwkv_w2, rwkv_a0, rwkv_a2, rwkv_k_k, rwkv_k_a, rwkv_r_k, rwkv_ln_w, rwkv_ln_b, rwkv_v0, rwkv_v1, rwkv_v2, ssm_conv_w, ssm_conv_b, ssm_dt_bias, ssm_A_log, ssm_D, ssm_norm_w, w_in_odd, w_out_odd, gla_gk_w, gla_gk_b, gla_norm_w):
    bsz, l, d = x.shape
    depth = norm_w.shape[0]
    w_out_even_bf = w_out_even.astype(BF16)
    w_out_odd_bf = w_out_odd.astype(BF16)
    outs = []
    for b in range(bsz):
        res = x[b]
        h = _rmsnorm(res, _rows8(norm_w[0]))
        p_first = None
        for layer in range(depth):
            i = layer // 2
            last = layer == depth - 1
            nw_next = _rows8(final_norm_w if last else norm_w[layer + 1])
            if layer % 2 == 0:
                mu = rwkv_mu[i]
                zpad = jnp.zeros((LANE - DECAY_LORA,), F32)
                mu_small = jnp.concatenate([mu[E_MAIN:E_MAIN + DECAY_LORA], zpad, mu[E_MAIN + DECAY_LORA:],
                                            zpad, jnp.zeros((S_COLS - 2 * LANE,), F32)])
                p = _inproj(h, w_in_even, E_MAIN, tn=512, mu8=_rows8(mu[:E_MAIN]), layer=i)
                pt = _inproj(h, _even_zxbc_weight(w_in_even, i), T_COLS, tn=512)
                ps = _inproj(h, _even_small_weight(w_in_even[i]), S_COLS, tn=S_COLS, mu8=_rows8(mu_small))
                v0 = rwkv_v0[i - 1] if i > 0 else jnp.zeros((RW,), F32)
                pa = _rows8(rwkv_w0[i], rwkv_a0[i], rwkv_k_k[i], rwkv_k_a[i],
                            rwkv_r_k[i].reshape(RW), rwkv_ln_w[i], rwkv_ln_b[i], v0)
                w2p = _pad_rows(rwkv_w2[i], LANE)
                a2p = _pad_rows(rwkv_a2[i], LANE)
                if i == 0:
                    vmix = None
                    p_first = p
                else:
                    tl = _vlora(p, _pad_cols(rwkv_v1[i - 1], LANE))
                    vmix = (tl, _pad_rows(rwkv_v2[i - 1], LANE), p_first)
                (atp, rp, bh, kh, vb, u0, y0, bonus, gs, dl) = _rwkv_prep(p, ps, pa, w2p, a2p, vmix)
                y_a = _rwkv_scan(atp, rp, bh, kh, vb, u0, y0, bonus, gs, dl, pa)

                cw8 = jnp.concatenate([ssm_conv_w[i], ssm_conv_b[i][None, :],
                                       jnp.zeros((SUB - SCONV - 1, ssm_conv_w.shape[2]), F32)], axis=0)
                pad_g = lambda a: _pad_cols(a.reshape(SGROUPS, SHG), LANE).reshape(SGROUPS * LANE)
                hp = _rows8(pad_g(ssm_dt_bias[i]), pad_g(ssm_A_log[i]))
                dx8 = _rows8(jnp.repeat(ssm_D[i], SHEAD))
                y_b = _ssd(pt, ps, cw8, hp, dx8, _rows8(ssm_norm_w[i]))
                o = _outproj([y_a, y_b], w_out_even_bf, i, res, nw_next, last)
            else:
                p = _inproj(h, w_in_odd, O_MAIN, tn=512, layer=i)
                pgd = _inproj(h, _pad_cols(w_in_odd[i][:, O_MAIN:], LANE), LANE, tn=LANE)
                y = _gla(p, pgd, _pad_rows(gla_gk_w[i], LANE), _rows8(gla_gk_b[i]), _rows8(gla_norm_w[i]))
                o = _outproj([y], w_out_odd_bf, i, res, nw_next, last)
            if last:
                res = o[0]
            else:
                res, h = o
        outs.append(res)
    return jnp.stack(outs).astype(x.dtype)
```

```python
import functools

import jax
import jax.numpy as jnp
from jax import lax
from jax.experimental import pallas as pl
from jax.experimental.pallas import tpu as pltpu

F32 = jnp.float32
BF16 = jnp.bfloat16

D_MODEL = 2048
EPS = 1e-5
RW = 2048
RH = 64
DECAY_LORA = 96
AAA_LORA = 96
MV_LORA = 64
RWKV_GN_EPS = 64e-5
RCHUNK = 64
SW = 2048
SHEAD = 64
SGROUPS = 4
SSTATE = 128
SCONV = 4
SCHUNK = 128
SGW = SW // SGROUPS
SHG = SGW // SHEAD
GHEADS = 4
GKEY = 1024
GVAL = 2048
GHK = 256
GHV = 512
GRANK = 16
GCHUNK = 64

LANE = 128
SUB = 8
VMEM_LIMIT = 56 * 1024 * 1024

E_R, E_K, E_V, E_G = 0, 2048, 4096, 6144
E_MAIN = 8192
E_WD = 8192
E_AD = 8288
E_ZXBC = 8384
E_COLS = 13536
T_Z, T_X = 0, 2048
T_B, T_C = 4096, 4608
T_COLS = 5120
S_WD, S_AD, S_DT = 0, 128, 256
S_COLS = 384
S_DT_COL0 = E_COLS - LANE
S_DT_LANE0 = LANE - SW // SHEAD
O_COLS = 6160
O_GD_COL0 = O_COLS - LANE
O_GD_LANE0 = LANE - GRANK
O_Q, O_K, O_V, O_G = 0, 1024, 2048, 4096
O_MAIN = 6144

NN = (((1,), (0,)), ((), ()))
NT = (((1,), (1,)), ((), ()))
TN = (((0,), (0,)), ((), ()))


def _dg(a, b, dims=NN):
    return lax.dot_general(a, b, dims, preferred_element_type=F32)


def _bdot(a, b, dims=NN):
    return _dg(a.astype(BF16), b.astype(BF16), dims)


def _split2(x):
    hi = x.astype(BF16)
    lo = (x - hi.astype(F32)).astype(BF16)
    return hi, lo


def _split3(x):
    hi = x.astype(BF16)
    r = x - hi.astype(F32)
    mid = r.astype(BF16)
    lo = (r - mid.astype(F32)).astype(BF16)
    return hi, mid, lo


def _dot3(a, b, dims=NN):
    ah, al = _split2(a)
    bh, bl = _split2(b)
    return _dg(ah, bh, dims) + (_dg(ah, bl, dims) + _dg(al, bh, dims))


def _dot_sel(a, sel_bf16, dims=NN):
    ah, am, al = _split3(a)
    return _dg(ah, sel_bf16, dims) + (_dg(am, sel_bf16, dims) + _dg(al, sel_bf16, dims))


def _sigmoid(x):
    return 1.0 / (1.0 + jnp.exp(-x))


def _silu(x):
    return x * _sigmoid(x)


def _log_sigmoid(x):
    return jnp.minimum(x, 0.0) - jnp.log(1.0 + jnp.exp(-jnp.abs(x)))


def _softplus(x):
    return jnp.maximum(x, 0.0) + jnp.log(1.0 + jnp.exp(-jnp.abs(x)))


def _iota(shape, axis):
    return lax.broadcasted_iota(jnp.int32, shape, axis)


def _idiv(x, n):
    return lax.shift_right_logical(x, n.bit_length() - 1)


def _seg_cumsum(x, chunk):
    rin = _iota(x.shape, 0) & (chunk - 1)
    s = 1
    while s < chunk:
        x = x + jnp.where(rin >= s, pltpu.roll(x, s, 0), 0.0)
        s *= 2
    return x


def _chunk_row(x, chunk, r):
    t, w = x.shape
    x3 = x.reshape(t // chunk, chunk, w)
    return jnp.broadcast_to(x3[:, r:r + 1, :], x3.shape).reshape(t, w)


def _head_ones(n, head, scale=1.0):
    r = _idiv(_iota((n, n), 0), head)
    c = _idiv(_iota((n, n), 1), head)
    return jnp.where(r == c, scale, 0.0).astype(BF16)


def _params(**kw):
    return pltpu.CompilerParams(vmem_limit_bytes=VMEM_LIMIT, **kw)


def _rmsnorm_kernel(x_ref, w_ref, o_ref):
    x = x_ref[...]
    ms = jnp.mean(x * x, axis=-1, keepdims=True)
    o_ref[...] = (x * lax.rsqrt(ms + EPS) * w_ref[0:1, :]).astype(BF16)


def _rmsnorm(x, w8, tm=512):
    l, d = x.shape
    return pl.pallas_call(
        _rmsnorm_kernel,
        grid=(l // tm,),
        in_specs=[pl.BlockSpec((tm, d), lambda i: (i, 0)),
                  pl.BlockSpec((SUB, d), lambda i: (0, 0))],
        out_specs=pl.BlockSpec((tm, d), lambda i: (i, 0)),
        out_shape=jax.ShapeDtypeStruct((l, d), BF16),
        compiler_params=_params(dimension_semantics=("parallel",)),
        name="rmsnorm",
    )(x, w8)


HALO = 16


def _inproj_kernel(*refs, shift):
    if shift:
        h_ref, hh_ref, w_ref, mu_ref, o_ref = refs
    else:
        h_ref, w_ref, o_ref = refs
    w = w_ref[0].astype(BF16)
    p = _dg(h_ref[...], w, NT)
    if shift:
        prev = _dg(hh_ref[...], w, NT)[HALO - 1:HALO, :]
        prev = jnp.where(pl.program_id(0) == 0, 0.0, prev)
        pprev = jnp.where(_iota(p.shape, 0) == 0, prev, pltpu.roll(p, 1, 0))
        p = p + (pprev - p) * mu_ref[0:1, :]
    o_ref[...] = p


def _inproj(h, wt, layer, col0, nblk, tn, mu8=None, tm=2048):
    l, d = h.shape
    tm = min(tm, l)
    shift = mu8 is not None
    in_specs = [pl.BlockSpec((tm, d), lambda i, j: (i, 0))]
    args = [h]
    if shift:
        hb = tm // HALO
        in_specs.append(pl.BlockSpec((HALO, d), lambda i, j: (jnp.maximum(i * hb - 1, 0), 0)))
        args.append(h)
    def w_map(i, j):
        c = col0(j)
        return (layer, c if isinstance(c, int) else pl.multiple_of(c, SUB), 0)

    in_specs.append(pl.BlockSpec((pl.Element(1), pl.Element(tn), pl.Element(d)), w_map))
    args.append(wt)
    if shift:
        in_specs.append(pl.BlockSpec((SUB, tn), lambda i, j: (0, j)))
        args.append(mu8)
    return pl.pallas_call(
        functools.partial(_inproj_kernel, shift=shift),
        grid=(l // tm, nblk),
        in_specs=in_specs,
        out_specs=pl.BlockSpec((tm, tn), lambda i, j: (i, j)),
        out_shape=jax.ShapeDtypeStruct((l, nblk * tn), F32),
        compiler_params=_params(dimension_semantics=("parallel", "parallel")),
        name="inproj",
    )(*args)


def _outproj_kernel(*refs, n_in, final):
    y_refs = refs[:n_in]
    w_ref, res_ref, nw_ref = refs[n_in:n_in + 3]
    outs = refs[n_in + 3:]
    r = res_ref[...]
    for s in range(n_in):
        kw = y_refs[s].shape[1]
        r = r + jnp.dot(y_refs[s][...], w_ref[s * kw:(s + 1) * kw, :], preferred_element_type=F32)
    ms = jnp.mean(r * r, axis=-1, keepdims=True)
    hn = r * lax.rsqrt(ms + EPS) * nw_ref[0:1, :]
    if final:
        outs[0][...] = hn
    else:
        outs[0][...] = r
        outs[1][...] = hn.astype(BF16)


def _outproj(ys, w, layer, res, nw8, final, tm=512):
    l, d = res.shape
    n_in = len(ys)
    kdim = w.shape[1]
    tm = min(tm, l)
    in_specs = [pl.BlockSpec((tm, y.shape[1]), lambda i: (i, 0)) for y in ys]
    in_specs += [pl.BlockSpec((None, kdim, d), lambda i: (layer, 0, 0), pipeline_mode=pl.Buffered(1)),
                 pl.BlockSpec((tm, d), lambda i: (i, 0)),
                 pl.BlockSpec((SUB, d), lambda i: (0, 0))]
    row_spec = pl.BlockSpec((tm, d), lambda i: (i, 0))
    if final:
        out_specs = [row_spec]
        out_shape = [jax.ShapeDtypeStruct((l, d), F32)]
    else:
        out_specs = [row_spec, row_spec]
        out_shape = [jax.ShapeDtypeStruct((l, d), F32), jax.ShapeDtypeStruct((l, d), BF16)]
    return pl.pallas_call(
        functools.partial(_outproj_kernel, n_in=n_in, final=final),
        grid=(l // tm,),
        in_specs=in_specs,
        out_specs=out_specs,
        out_shape=out_shape,
        compiler_params=_params(dimension_semantics=("parallel",)),
        name="outproj",
    )(*ys, w, res, nw8)


def _vlora_kernel(v_ref, v1_ref, o_ref):
    o_ref[...] = _dot3(v_ref[...], v1_ref[...])


def _vlora(p, v1p, t=512):
    l = p.shape[0]
    t = min(t, l)
    cb = E_V // RW
    return pl.pallas_call(
        _vlora_kernel,
        grid=(l // t,),
        in_specs=[pl.BlockSpec((t, RW), lambda i: (i, cb)),
                  pl.BlockSpec((RW, LANE), lambda i: (0, 0))],
        out_specs=pl.BlockSpec((t, LANE), lambda i: (i, 0)),
        out_shape=jax.ShapeDtypeStruct((l, LANE), F32),
        compiler_params=_params(dimension_semantics=("parallel",)),
        name="rwkv_vlora",
    )(p, v1p)


RT = 1024
RSUB = 1024
RST = 256


def _rwkv_prep_kernel(*refs, has_vmix, nsub):
    (r_ref, k_ref, v_ref, g_ref, wd_ref, ad_ref, pa_ref, w2_ref, a2_ref) = refs[:9]
    pos = 9
    if has_vmix:
        tl_ref, v2_ref, vf_ref = refs[pos:pos + 3]
        pos += 3
    orefs = refs[pos:]
    st = r_ref.shape[0] // nsub
    for j in range(nsub):
        rs = slice(j * st, (j + 1) * st)
        vmix_j = (tl_ref[rs, :], v2_ref[...], vf_ref[rs, :]) if has_vmix else None
        _rwkv_prep_subtile(r_ref[rs, :], k_ref[rs, :], v_ref[rs, :], g_ref[rs, :], wd_ref[rs, :],
                           ad_ref[rs, :], pa_ref, w2_ref[...], a2_ref[...], vmix_j, orefs, j, st)


def _rwkv_prep_subtile(r, k, v, g, wd, ad, pa_ref, w2, a2, vmix, orefs, j, st):
    (atp_ref, rp_ref, bh_ref, kh2_ref, vb_ref, u0_ref, y0_ref, bonus_ref, gs_ref, dl_ref) = orefs
    rs = slice(j * st, (j + 1) * st)
    nchunk = st // RCHUNK
    w0, a0, k_k, k_a = (pa_ref[i:i + 1, :] for i in range(4))
    r_k, v0 = pa_ref[4:5, :], pa_ref[7:8, :]

    w_log = _log_sigmoid(w0 + _dot3(jnp.tanh(wd), w2)) - 0.5
    logd = -jnp.exp(w_log)
    a = _sigmoid(a0 + _dot3(ad, a2))
    if vmix is not None:
        tl, v2, vf = vmix
        v = v + (vf - v) * _sigmoid(v0 + _dot3(tl, v2))

    hm = _head_ones(LANE, RH)
    kk = k * k_k
    kk2 = kk * kk
    k2h, k2l = _split2(kk2)
    ss = _dg(k2h, hm) + _dg(k2l, hm)
    kk = kk * lax.rsqrt(jnp.maximum(ss, 1e-24))
    kh = k * (1.0 + (a - 1.0) * k_a)
    av = -kk
    bv = kk * a

    rkh, rkl = _split2(r * kh * r_k)
    bonus_ref[rs, :] = (_dg(rkh, hm) + _dg(rkl, hm)) * v
    gs_ref[rs, :] = _silu(g)

    cum = _seg_cumsum(logd, RCHUNK)
    cl = _chunk_row(cum, RCHUNK, RCHUNK - 1)
    e_n = jnp.exp(-cum)
    e_l = jnp.exp(cl - cum)
    at = av * jnp.exp(cum - logd)
    rt = r * jnp.exp(cum)
    bt = bv * e_n
    kt = kh * e_n
    bh_ref[rs, :] = (bv * e_l).astype(BF16)
    kh2_ref[rs, :] = (kh * e_l).astype(BF16)
    vb_ref[rs, :] = v.astype(BF16)
    dl_ref[j * nchunk * SUB:(j + 1) * nchunk * SUB, :] = (
        jnp.exp(cl).reshape(nchunk, RCHUNK, LANE)[:, 0:SUB, :].reshape(nchunk * SUB, LANE))

    lane = _iota((RCHUNK, LANE), 1)
    si = lane & (RH - 1)
    ti = _iota((RCHUNK, LANE), 0)
    strict = si < ti
    incl = si <= ti
    eye = jnp.where(si == ti, 1.0, 0.0)
    head0 = lane < RH

    def stack(xc):
        xc = xc.astype(BF16)
        zero = jnp.zeros_like(xc)
        return jnp.concatenate([jnp.where(head0, xc, zero), jnp.where(head0, zero, xc)], axis=0)

    chunks = range(nchunk)
    rows = [slice(c * RCHUNK, (c + 1) * RCHUNK) for c in chunks]
    orows = [slice(j * st + c * RCHUNK, j * st + (c + 1) * RCHUNK) for c in chunks]
    ats = [stack(at[rw]) for rw in rows]
    vs = [stack(v[rw]) for rw in rows]
    amat = []
    for c in chunks:
        lhs = jnp.concatenate([at[rows[c]], rt[rows[c]]], axis=0).astype(BF16)
        rhs = jnp.concatenate([stack(bt[rows[c]]), stack(kt[rows[c]])], axis=0)
        amat.append(_dg(lhs, rhs, NT))
    a_ab = [jnp.where(strict, m[:RCHUNK, :LANE], 0.0) for m in amat]
    a_ak = [jnp.where(strict, m[:RCHUNK, LANE:], 0.0) for m in amat]
    a_rb = [jnp.where(incl, m[RCHUNK:, :LANE], 0.0).astype(BF16) for m in amat]
    a_rk = [jnp.where(incl, m[RCHUNK:, LANE:], 0.0) for m in amat]
    akv2 = [_dg(jnp.concatenate([a_ak[c], a_rk[c]], axis=0).astype(BF16), vs[c]) for c in chunks]
    akv = [m[:RCHUNK] for m in akv2]
    arkv = [m[RCHUNK:] for m in akv2]
    inv = [eye + m for m in a_ab]
    pw = [_dg(m.astype(BF16), stack(m)) for m in a_ab]
    s = 2
    while s < RCHUNK // 2:
        both = [_dg(pw[c].astype(BF16), jnp.concatenate([stack(pw[c]), stack(inv[c])], axis=1))
                for c in chunks]
        pw = [m[:, :LANE] for m in both]
        inv = [inv[c] + both[c][:, LANE:] for c in chunks]
        s *= 2
    inv = [inv[c] + _dg(pw[c].astype(BF16), stack(inv[c])) for c in chunks]
    x = [_dg(inv[c].astype(BF16), jnp.concatenate([ats[c], stack(akv[c])], axis=1)) for c in chunks]
    z = [_dg(a_rb[c], jnp.concatenate([stack(x[c][:, :LANE]), stack(x[c][:, LANE:])], axis=1))
         for c in chunks]
    for c in chunks:
        atp_ref[orows[c], :] = x[c][:, :LANE].astype(BF16)
        u0_ref[orows[c], :] = x[c][:, LANE:]
        rp_ref[orows[c], :] = (rt[rows[c]] + z[c][:, :LANE]).astype(BF16)
        y0_ref[orows[c], :] = z[c][:, LANE:] + arkv[c]


def _rwkv_prep(p, ps, pa, w2p, a2p, vmix):
    l = p.shape[0]
    t = min(RT, l)
    nsub = max(t // RSUB, 1)
    npair = RW // LANE
    has_vmix = vmix is not None

    def main(off):
        cb = off // LANE
        return pl.BlockSpec((t, LANE), lambda q, i: (i, cb + q))

    def main1(off):
        cb = off // LANE
        return pl.BlockSpec((t, LANE), lambda q, i: (i, cb))

    par = pl.BlockSpec((SUB, LANE), lambda q, i: (0, q))
    lora = pl.BlockSpec((LANE, LANE), lambda q, i: (0, q))
    row = pl.BlockSpec((t, LANE), lambda q, i: (i, q))

    in_specs = [main(E_R), main(E_K), main(E_V), main(E_G), main1(S_WD), main1(S_AD), par, lora, lora]
    args = [p] * 4 + [ps] * 2 + [pa, w2p, a2p]
    if has_vmix:
        tl, v2p, p_first = vmix
        in_specs += [pl.BlockSpec((t, LANE), lambda q, i: (i, 0)), lora, main(E_V)]
        args += [tl, v2p, p_first]
    dl_rows = (t // RCHUNK) * SUB
    out_specs = [row] * 9 + [pl.BlockSpec((dl_rows, LANE), lambda q, i: (i, q))]
    sd = jax.ShapeDtypeStruct
    out_shape = ([sd((l, RW), BF16)] * 5 + [sd((l, RW), F32)] * 4
                 + [sd((l // RCHUNK * SUB, RW), F32)])
    return pl.pallas_call(
        functools.partial(_rwkv_prep_kernel, has_vmix=has_vmix, nsub=nsub),
        grid=(npair, l // t),
        in_specs=in_specs,
        out_specs=out_specs,
        out_shape=out_shape,
        compiler_params=_params(dimension_semantics=("parallel", "parallel")),
        name="rwkv_prep",
    )(*args)


def _rwkv_scan_kernel(atp_ref, rp_ref, bh_ref, kh_ref, vb_ref, u0_ref, y0_ref,
                      bonus_ref, gs_ref, dl_ref, pa_ref, o_ref, s_ref, y_ref):
    @pl.when(pl.program_id(0) == 0)
    def _():
        s_ref[...] = jnp.zeros_like(s_ref)

    t = atp_ref.shape[0]
    npair = RW // LANE
    bd = _idiv(_iota((LANE, LANE), 0), RH) == _idiv(_iota((LANE, LANE), 1), RH)
    hmean = _head_ones(LANE, RH, 1.0 / RH)

    pairs = range(npair)
    lanes = [slice(q * LANE, (q + 1) * LANE) for q in pairs]
    for c in range(t // RCHUNK):
        rows = slice(c * RCHUNK, (c + 1) * RCHUNK)
        s_old = [s_ref[q] for q in pairs]
        o = [_dg(jnp.concatenate([atp_ref[rows, lanes[q]], rp_ref[rows, lanes[q]]], axis=0),
                 s_old[q].astype(BF16), NT) for q in pairs]
        ds = []
        for q in pairs:
            u = o[q][:RCHUNK] + u0_ref[rows, lanes[q]]
            y_ref[rows, lanes[q]] = o[q][RCHUNK:] + y0_ref[rows, lanes[q]]
            uv = jnp.concatenate([u.astype(BF16), vb_ref[rows, lanes[q]]], axis=0)
            bk = jnp.concatenate([bh_ref[rows, lanes[q]], kh_ref[rows, lanes[q]]], axis=0)
            ds.append(_dg(uv, bk, TN))
        for q in pairs:
            dl = dl_ref[c * SUB:c * SUB + 1, lanes[q]]
            s_ref[q] = s_old[q] * dl + jnp.where(bd, ds[q], 0.0)

    for q in range(npair):
        sl = slice(q * LANE, (q + 1) * LANE)
        y = y_ref[:, sl]
        yh, yl = _split2(y)
        mean = _dg(yh, hmean) + _dg(yl, hmean)
        d = y - mean
        var = _bdot(d * d, hmean)
        yn = d * lax.rsqrt(var + RWKV_GN_EPS) * pa_ref[5:6, sl] + pa_ref[6:7, sl]
        o_ref[:, sl] = ((yn + bonus_ref[:, sl]) * gs_ref[:, sl]).astype(BF16)


def _rwkv_scan(atp, rp, bh, kh, vb, u0, y0, bonus, gs, dl, pa):
    l = atp.shape[0]
    t = min(RST, l)
    row = pl.BlockSpec((t, RW), lambda i: (i, 0))
    return pl.pallas_call(
        _rwkv_scan_kernel,
        grid=(l // t,),
        in_specs=[row] * 9 + [pl.BlockSpec(((t // RCHUNK) * SUB, RW), lambda i: (i, 0)),
                              pl.BlockSpec((SUB, RW), lambda i: (0, 0))],
        out_specs=row,
        out_shape=jax.ShapeDtypeStruct((l, RW), BF16),
        scratch_shapes=[pltpu.VMEM((RW // LANE, LANE, LANE), F32), pltpu.VMEM((t, RW), F32)],
        compiler_params=_params(dimension_semantics=("arbitrary",)),
        name="rwkv_scan",
    )(atp, rp, bh, kh, vb, u0, y0, bonus, gs, dl, pa)


def _ssd_conv(main_ref, halo_ref, wb_ref, buf_ref, first):
    q = main_ref.shape[0]
    buf_ref[0:SUB, :] = jnp.where(first, 0.0, halo_ref[...])
    buf_ref[SUB:SUB + q, :] = main_ref[...]
    acc = wb_ref[SCONV:SCONV + 1, :]
    for j in range(SCONV):
        acc = acc + buf_ref[pl.ds(SUB - (SCONV - 1) + j, q), :] * wb_ref[j:j + 1, :]
    return _silu(acc)


def _ssd_kernel(z_ref, x_ref, b_ref, c_ref, dt_ref, xh_ref, bhalo_ref, chalo_ref,
                cwx_ref, cwb_ref, cwc_ref, hp_ref, dx_ref, nw_ref,
                o_ref, prev_ref, xbuf_ref, bbuf_ref, cbuf_ref, ybuf_ref):
    g = pl.program_id(0)
    first = pl.program_id(1) == 0
    q = z_ref.shape[0]

    @pl.when(first)
    def _():
        prev_ref[...] = jnp.zeros_like(prev_ref)

    xs = _ssd_conv(x_ref, xh_ref, cwx_ref, xbuf_ref, first)
    bm = _ssd_conv(b_ref, bhalo_ref, cwb_ref, bbuf_ref, first)
    cm = _ssd_conv(c_ref, chalo_ref, cwc_ref, cbuf_ref, first)

    sl_r = _iota((LANE, LANE), 0)
    sl_c = _iota((LANE, LANE), 1)
    sel = jnp.where((sl_r == sl_c + (g * SHG + S_DT_LANE0)) & (sl_c < SHG), 1.0, 0.0).astype(BF16)
    dt_raw = _dot_sel(dt_ref[...], sel)
    dt = _softplus(dt_raw + hp_ref[0:1, :])
    a_neg = -jnp.exp(hp_ref[1:2, :])
    acs = _seg_cumsum(dt * a_neg, q)
    acs_t = acs.T
    acs_last = acs[q - 1:q, :]

    expand = jnp.where(_iota((LANE, SGW), 0) == _idiv(_iota((LANE, SGW), 1), SHEAD), 1.0, 0.0).astype(BF16)
    expand_t = jnp.where(_iota((SGW, LANE), 1) == _idiv(_iota((SGW, LANE), 0), SHEAD), 1.0, 0.0).astype(BF16)
    dt_e = _dot_sel(dt, expand)
    eacs_e = _dot_sel(jnp.exp(acs), expand)
    ds_e = _dot_sel(jnp.exp(acs_last - acs), expand)
    xc = xs * dt_e

    cmb = cm.astype(BF16)
    bmb = bm.astype(BF16)
    scores = _dg(cmb, bmb, NT)
    prev = prev_ref[...]
    y_off = _dg(cmb, prev.astype(BF16), NT) * eacs_e
    states = _dg((xc * ds_e).astype(BF16), bmb, TN)
    eh, em, el = _split3(jnp.exp(acs_t))
    cdec = _dg(expand_t, eh) + (_dg(expand_t, em) + _dg(expand_t, el))
    prev_ref[...] = prev * cdec[:, q - 1:q] + states

    causal = _iota((q, q), 1) <= _iota((q, q), 0)
    for h in range(SHG):
        diff = acs[:, h:h + 1] - acs_t[h:h + 1, :]
        lm = jnp.exp(jnp.where(causal, diff, -jnp.inf))
        ybuf_ref[:, h * SHEAD:(h + 1) * SHEAD] = _bdot(scores * lm, xc[:, h * SHEAD:(h + 1) * SHEAD])
    y = ybuf_ref[...] + y_off + dx_ref[0:1, :] * xs
    y = y * _silu(z_ref[...])
    ms = jnp.mean(y * y, axis=-1, keepdims=True)
    o_ref[...] = (y * lax.rsqrt(ms + EPS) * nw_ref[0:1, :]).astype(BF16)


def _ssd(p, ps, cw8, hp, dx8, nw8):
    l = p.shape[0]
    q = SCHUNK
    hb = q // SUB

    def halo_idx(i):
        return jnp.maximum(i * hb - 1, 0)

    zb, xb = T_Z // SGW, T_X // SGW
    bb, cb, db = T_B // LANE, T_C // LANE, S_DT // LANE
    in_specs = [
        pl.BlockSpec((q, SGW), lambda g, i: (i, zb + g)),
        pl.BlockSpec((q, SGW), lambda g, i: (i, xb + g)),
        pl.BlockSpec((q, LANE), lambda g, i: (i, bb + g)),
        pl.BlockSpec((q, LANE), lambda g, i: (i, cb + g)),
        pl.BlockSpec((q, LANE), lambda g, i: (i, db)),
        pl.BlockSpec((SUB, SGW), lambda g, i: (halo_idx(i), xb + g)),
        pl.BlockSpec((SUB, LANE), lambda g, i: (halo_idx(i), bb + g)),
        pl.BlockSpec((SUB, LANE), lambda g, i: (halo_idx(i), cb + g)),
        pl.BlockSpec((SUB, SGW), lambda g, i: (0, g)),
        pl.BlockSpec((SUB, LANE), lambda g, i: (0, SW // LANE + g)),
        pl.BlockSpec((SUB, LANE), lambda g, i: (0, SW // LANE + SGROUPS + g)),
        pl.BlockSpec((SUB, LANE), lambda g, i: (0, g)),
        pl.BlockSpec((SUB, SGW), lambda g, i: (0, g)),
        pl.BlockSpec((SUB, SGW), lambda g, i: (0, g)),
    ]
    return pl.pallas_call(
        _ssd_kernel,
        grid=(SGROUPS, l // q),
        in_specs=in_specs,
        out_specs=pl.BlockSpec((q, SGW), lambda g, i: (i, g)),
        out_shape=jax.ShapeDtypeStruct((l, SW), BF16),
        scratch_shapes=[pltpu.VMEM((SGW, SSTATE), F32),
                        pltpu.VMEM((q + SUB, SGW), F32),
                        pltpu.VMEM((q + SUB, LANE), F32),
                        pltpu.VMEM((q + SUB, LANE), F32),
                        pltpu.VMEM((q, SGW), F32)],
        compiler_params=_params(dimension_semantics=("parallel", "arbitrary")),
        name="ssd",
    )(p, p, p, p, ps, p, p, p, cw8, cw8, cw8, hp, dx8, nw8)


GT = 256


def _gla_kernel(q_ref, k_ref, v_ref, g_ref, gd_ref, gkw_ref, gkb_ref, nw_ref, o_ref, st_ref, oi_ref):
    @pl.when(pl.program_id(1) == 0)
    def _():
        st_ref[...] = jnp.zeros_like(st_ref)

    t = q_ref.shape[0]
    nchunk = t // GCHUNK
    gk = _log_sigmoid(_dot3(gd_ref[...], gkw_ref[...]) + gkb_ref[0:1, :]) * (1.0 / 16.0)
    bc = _seg_cumsum(gk, GCHUNK)
    mid = _chunk_row(bc, GCHUNK, GCHUNK // 2)
    last = _chunk_row(bc, GCHUNK, GCHUNK - 1)
    qs = q_ref[...] * (GHK ** -0.5)
    k = k_ref[...]
    vb = v_ref[...].astype(BF16)

    ri = _iota((t, t), 0)
    ci = _iota((t, t), 1)
    incl = (_idiv(ri, GCHUNK) == _idiv(ci, GCHUNK)) & (ci <= ri)
    attn = _bdot(qs * jnp.exp(bc - mid), k * jnp.exp(mid - bc), NT)
    attn = jnp.where(incl, attn, 0.0)
    o_intra = _bdot(attn, vb)

    q_in = (qs * jnp.exp(bc)).astype(BF16)
    k_st = (k * jnp.exp(last - bc)).astype(BF16)
    dlast = jnp.exp(last)
    for c in range(nchunk):
        rows = slice(c * GCHUNK, (c + 1) * GCHUNK)
        st = st_ref[...]
        oi_ref[rows, :] = _dg(q_in[rows], st.astype(BF16), NT)
        st_ref[...] = st * dlast[c * GCHUNK:c * GCHUNK + 1, :] + _dg(vb[rows], k_st[rows], TN)
    o = o_intra + oi_ref[...]
    ms = jnp.mean(o * o, axis=-1, keepdims=True)
    o = o * lax.rsqrt(ms + EPS) * nw_ref[0:1, :]
    o_ref[...] = (o * _silu(g_ref[...])).astype(BF16)


def _gla(p, pgd, gkwp, gkb8, nw8):
    l = p.shape[0]
    t = min(GT, l)
    return pl.pallas_call(
        _gla_kernel,
        grid=(GHEADS, l // t),
        in_specs=[pl.BlockSpec((t, GHK), lambda h, i: (i, O_Q // GHK + h)),
                  pl.BlockSpec((t, GHK), lambda h, i: (i, O_K // GHK + h)),
                  pl.BlockSpec((t, GHV), lambda h, i: (i, O_V // GHV + h)),
                  pl.BlockSpec((t, GHV), lambda h, i: (i, O_G // GHV + h)),
                  pl.BlockSpec((t, LANE), lambda h, i: (i, 0)),
                  pl.BlockSpec((LANE, GHK), lambda h, i: (0, h)),
                  pl.BlockSpec((SUB, GHK), lambda h, i: (0, h)),
                  pl.BlockSpec((SUB, GHV), lambda h, i: (0, 0))],
        out_specs=pl.BlockSpec((t, GHV), lambda h, i: (i, h)),
        out_shape=jax.ShapeDtypeStruct((l, GVAL), BF16),
        scratch_shapes=[pltpu.VMEM((GHV, GHK), F32), pltpu.VMEM((t, GHV), F32)],
        compiler_params=_params(dimension_semantics=("parallel", "arbitrary")),
        name="gla",
    )(p, p, p, p, pgd, gkwp, gkb8, nw8)


def _rows8(*rows):
    n = rows[0].shape[0]
    parts = [r.astype(F32)[None, :] for r in rows]
    if len(rows) < SUB:
        parts.append(jnp.zeros((SUB - len(rows), n), F32))
    return jnp.concatenate(parts, axis=0)


def _pad_cols(w, n):
    return jnp.concatenate([w, jnp.zeros((w.shape[0], n - w.shape[1]), w.dtype)], axis=1)


def _pad_rows(w, n):
    return jnp.concatenate([w, jnp.zeros((n - w.shape[0], w.shape[1]), w.dtype)], axis=0)


def _small_col0(j):
    return pl.multiple_of(jnp.where(j == 2, S_DT_COL0, E_WD + j * (E_AD - E_WD)), SUB)


def kernel(x, norm_w, final_norm_w, w_in_even, w_out_even, rwkv_mu, rwkv_w0, rwkv_w2, rwkv_a0, rwkv_a2, rwkv_k_k, rwkv_k_a, rwkv_r_k, rwkv_ln_w, rwkv_ln_b, rwkv_v0, rwkv_v1, rwkv_v2, ssm_conv_w, ssm_conv_b, ssm_dt_bias, ssm_A_log, ssm_D, ssm_norm_w, w_in_odd, w_out_odd, gla_gk_w, gla_gk_b, gla_norm_w):
    bsz, l, d = x.shape
    depth = norm_w.shape[0]
    w_out_even_bf = w_out_even.astype(BF16)
    w_out_odd_bf = w_out_odd.astype(BF16)
    wt_even = jnp.swapaxes(w_in_even, 1, 2)
    wt_odd = jnp.swapaxes(w_in_odd, 1, 2)
    outs = []
    for b in range(bsz):
        res = x[b]
        h = _rmsnorm(res, _rows8(norm_w[0]))
        p_first = None
        for layer in range(depth):
            i = layer // 2
            last = layer == depth - 1
            nw_next = _rows8(final_norm_w if last else norm_w[layer + 1])
            if layer % 2 == 0:
                mu = rwkv_mu[i]
                zpad = jnp.zeros((LANE - DECAY_LORA,), F32)
                mu_small = jnp.concatenate([mu[E_WD:E_AD], zpad, mu[E_AD:E_ZXBC], zpad,
                                            jnp.zeros((S_COLS - 2 * LANE,), F32)])
                p = _inproj(h, wt_even, i, lambda j: j * 512, E_MAIN // 512, 512, mu8=_rows8(mu[:E_MAIN]))
                pt = _inproj(h, wt_even, i, lambda j: E_ZXBC + j * 512, T_COLS // 512, 512)
                ps = _inproj(h, wt_even, i, _small_col0, S_COLS // LANE, LANE, mu8=_rows8(mu_small))
                v0 = rwkv_v0[i - 1] if i > 0 else jnp.zeros((RW,), F32)
                pa = _rows8(rwkv_w0[i], rwkv_a0[i], rwkv_k_k[i], rwkv_k_a[i],
                            rwkv_r_k[i].reshape(RW), rwkv_ln_w[i], rwkv_ln_b[i], v0)
                w2p = _pad_rows(rwkv_w2[i], LANE)
                a2p = _pad_rows(rwkv_a2[i], LANE)
                if i == 0:
                    vmix = None
                    p_first = p
                else:
                    tl = _vlora(p, _pad_cols(rwkv_v1[i - 1], LANE))
                    vmix = (tl, _pad_rows(rwkv_v2[i - 1], LANE), p_first)
                (atp, rp, bh, kh, vb, u0, y0, bonus, gs, dl) = _rwkv_prep(p, ps, pa, w2p, a2p, vmix)
                y_a = _rwkv_scan(atp, rp, bh, kh, vb, u0, y0, bonus, gs, dl, pa)

                cw8 = jnp.concatenate([ssm_conv_w[i], ssm_conv_b[i][None, :],
                                       jnp.zeros((SUB - SCONV - 1, ssm_conv_w.shape[2]), F32)], axis=0)
                pad_g = lambda a: _pad_cols(a.reshape(SGROUPS, SHG), LANE).reshape(SGROUPS * LANE)
                hp = _rows8(pad_g(ssm_dt_bias[i]), pad_g(ssm_A_log[i]))
                dx8 = _rows8(jnp.repeat(ssm_D[i], SHEAD))
                y_b = _ssd(pt, ps, cw8, hp, dx8, _rows8(ssm_norm_w[i]))
                o = _outproj([y_a, y_b], w_out_even_bf, i, res, nw_next, last)
            else:
                p = _inproj(h, wt_odd, i, lambda j: j * 512, O_MAIN // 512, 512)
                pgd = _inproj(h, wt_odd, i, lambda j: O_GD_COL0, 1, LANE)
                gkwp = jnp.concatenate([jnp.zeros((O_GD_LANE0, GKEY), F32), gla_gk_w[i]], axis=0)
                y = _gla(p, pgd, gkwp, _rows8(gla_gk_b[i]), _rows8(gla_norm_w[i]))
                o = _outproj([y], w_out_odd_bf, i, res, nw_next, last)
            if last:
                res = o[0]
            else:
                res, h = o
        outs.append(res)
    return jnp.stack(outs).astype(x.dtype)
```

```python
import functools
import math

import jax
import jax.numpy as jnp
from jax import lax
from jax.experimental import pallas as pl
from jax.experimental.pallas import tpu as pltpu

F32 = jnp.float32
BF16 = jnp.bfloat16

D_MODEL = 2048
EPS = 1e-5
RW = 2048
RH = 64
DECAY_LORA = 96
AAA_LORA = 96
MV_LORA = 64
RWKV_GN_EPS = 64e-5
RCHUNK = 64
SW = 2048
SHEAD = 64
SGROUPS = 4
SSTATE = 128
SCONV = 4
SCHUNK = 128
SGW = SW // SGROUPS
SHG = SGW // SHEAD
GHEADS = 4
GKEY = 1024
GVAL = 2048
GHK = 256
GHV = 512
GRANK = 16
GCHUNK = 64

LANE = 128
SUB = 8
VMEM_LIMIT = 56 * 1024 * 1024

E_R, E_K, E_V, E_G = 0, 2048, 4096, 6144
E_MAIN = 8192
E_WD = 8192
E_AD = 8288
E_ZXBC = 8384
E_COLS = 13536
T_Z, T_X = 0, 2048
T_B, T_C = 4096, 4608
T_COLS = 5120
S_WD, S_AD, S_DT = 0, 128, 256
S_COLS = 384
S_DT_COL0 = E_COLS - LANE
S_DT_LANE0 = LANE - SW // SHEAD
O_COLS = 6160
O_GD_COL0 = O_COLS - LANE
O_GD_LANE0 = LANE - GRANK
O_Q, O_K, O_V, O_G = 0, 1024, 2048, 4096
O_MAIN = 6144

NN = (((1,), (0,)), ((), ()))
NT = (((1,), (1,)), ((), ()))
TN = (((0,), (0,)), ((), ()))


def _dg(a, b, dims=NN):
    return lax.dot_general(a, b, dims, preferred_element_type=F32)


def _bdot(a, b, dims=NN):
    return _dg(a.astype(BF16), b.astype(BF16), dims)


def _split2(x):
    hi = x.astype(BF16)
    lo = (x - hi.astype(F32)).astype(BF16)
    return hi, lo


def _split3(x):
    hi = x.astype(BF16)
    r = x - hi.astype(F32)
    mid = r.astype(BF16)
    lo = (r - mid.astype(F32)).astype(BF16)
    return hi, mid, lo


def _dot3(a, b, dims=NN):
    ah, al = _split2(a)
    bh, bl = _split2(b)
    return _dg(ah, bh, dims) + (_dg(ah, bl, dims) + _dg(al, bh, dims))


def _dot_sel(a, sel_bf16, dims=NN):
    ah, am, al = _split3(a)
    return _dg(ah, sel_bf16, dims) + (_dg(am, sel_bf16, dims) + _dg(al, sel_bf16, dims))


EXP_NEG_HALF = math.exp(-0.5)


def _sigmoid(x):
    return 0.5 * jnp.tanh(0.5 * x) + 0.5


def _silu(x):
    return x * _sigmoid(x)


def _log_sigmoid(x):
    return jnp.minimum(x, 0.0) - jnp.log(1.0 + jnp.exp(-jnp.abs(x)))


def _softplus(x):
    return jnp.maximum(x, 0.0) + jnp.log(1.0 + jnp.exp(-jnp.abs(x)))


def _iota(shape, axis):
    return lax.broadcasted_iota(jnp.int32, shape, axis)


def _idiv(x, n):
    return lax.shift_right_logical(x, n.bit_length() - 1)


def _seg_cumsum(x, chunk):
    rin = _iota(x.shape, 0) & (chunk - 1)
    s = 1
    while s < chunk:
        x = x + jnp.where(rin >= s, pltpu.roll(x, s, 0), 0.0)
        s *= 2
    return x


def _chunk_row(x, chunk, r):
    t, w = x.shape
    x3 = x.reshape(t // chunk, chunk, w)
    return jnp.broadcast_to(x3[:, r:r + 1, :], x3.shape).reshape(t, w)


def _head_ones(n, head, scale=1.0):
    r = _idiv(_iota((n, n), 0), head)
    c = _idiv(_iota((n, n), 1), head)
    return jnp.where(r == c, scale, 0.0).astype(BF16)


def _params(**kw):
    return pltpu.CompilerParams(vmem_limit_bytes=VMEM_LIMIT, **kw)


def _rmsnorm_kernel(x_ref, w_ref, o_ref):
    x = x_ref[...]
    ms = jnp.mean(x * x, axis=-1, keepdims=True)
    o_ref[...] = (x * lax.rsqrt(ms + EPS) * w_ref[0:1, :]).astype(BF16)


def _rmsnorm(x, w8, tm=512):
    l, d = x.shape
    return pl.pallas_call(
        _rmsnorm_kernel,
        grid=(l // tm,),
        in_specs=[pl.BlockSpec((tm, d), lambda i: (i, 0)),
                  pl.BlockSpec((SUB, d), lambda i: (0, 0))],
        out_specs=pl.BlockSpec((tm, d), lambda i: (i, 0)),
        out_shape=jax.ShapeDtypeStruct((l, d), BF16),
        compiler_params=_params(dimension_semantics=("parallel",)),
        name="rmsnorm",
    )(x, w8)


HALO = 16


def _inproj_kernel(*refs, shift):
    if shift:
        h_ref, hh_ref, w_ref, mu_ref, o_ref = refs
    else:
        h_ref, w_ref, o_ref = refs
    w = w_ref[0].astype(BF16)
    p = _dg(h_ref[...], w, NT)
    if shift:
        prev = _dg(hh_ref[...], w, NT)[HALO - 1:HALO, :]
        prev = jnp.where(pl.program_id(0) == 0, 0.0, prev)
        pprev = jnp.where(_iota(p.shape, 0) == 0, prev, pltpu.roll(p, 1, 0))
        p = p + (pprev - p) * mu_ref[0:1, :]
    o_ref[...] = p


def _inproj(h, wt, layer, col0, nblk, tn, mu8=None, tm=2048):
    l, d = h.shape
    tm = min(tm, l)
    shift = mu8 is not None
    in_specs = [pl.BlockSpec((tm, d), lambda i, j: (i, 0))]
    args = [h]
    if shift:
        hb = tm // HALO
        in_specs.append(pl.BlockSpec((HALO, d), lambda i, j: (jnp.maximum(i * hb - 1, 0), 0)))
        args.append(h)
    def w_map(i, j):
        c = col0(j)
        return (layer, c if isinstance(c, int) else pl.multiple_of(c, SUB), 0)

    in_specs.append(pl.BlockSpec((pl.Element(1), pl.Element(tn), pl.Element(d)), w_map))
    args.append(wt)
    if shift:
        in_specs.append(pl.BlockSpec((SUB, tn), lambda i, j: (0, j)))
        args.append(mu8)
    return pl.pallas_call(
        functools.partial(_inproj_kernel, shift=shift),
        grid=(l // tm, nblk),
        in_specs=in_specs,
        out_specs=pl.BlockSpec((tm, tn), lambda i, j: (i, j)),
        out_shape=jax.ShapeDtypeStruct((l, nblk * tn), F32),
        compiler_params=_params(dimension_semantics=("parallel", "parallel")),
        name="inproj",
    )(*args)


def _outproj_kernel(*refs, n_in, final):
    y_refs = refs[:n_in]
    w_ref, res_ref, nw_ref = refs[n_in:n_in + 3]
    outs = refs[n_in + 3:]
    r = res_ref[...]
    for s in range(n_in):
        kw = y_refs[s].shape[1]
        r = r + jnp.dot(y_refs[s][...], w_ref[s * kw:(s + 1) * kw, :], preferred_element_type=F32)
    ms = jnp.mean(r * r, axis=-1, keepdims=True)
    hn = r * lax.rsqrt(ms + EPS) * nw_ref[0:1, :]
    if final:
        outs[0][...] = hn
    else:
        outs[0][...] = r
        outs[1][...] = hn.astype(BF16)


def _outproj(ys, w, layer, res, nw8, final, tm=512):
    l, d = res.shape
    n_in = len(ys)
    kdim = w.shape[1]
    tm = min(tm, l)
    in_specs = [pl.BlockSpec((tm, y.shape[1]), lambda i: (i, 0)) for y in ys]
    in_specs += [pl.BlockSpec((None, kdim, d), lambda i: (layer, 0, 0), pipeline_mode=pl.Buffered(1)),
                 pl.BlockSpec((tm, d), lambda i: (i, 0)),
                 pl.BlockSpec((SUB, d), lambda i: (0, 0))]
    row_spec = pl.BlockSpec((tm, d), lambda i: (i, 0))
    if final:
        out_specs = [row_spec]
        out_shape = [jax.ShapeDtypeStruct((l, d), F32)]
    else:
        out_specs = [row_spec, row_spec]
        out_shape = [jax.ShapeDtypeStruct((l, d), F32), jax.ShapeDtypeStruct((l, d), BF16)]
    return pl.pallas_call(
        functools.partial(_outproj_kernel, n_in=n_in, final=final),
        grid=(l // tm,),
        in_specs=in_specs,
        out_specs=out_specs,
        out_shape=out_shape,
        compiler_params=_params(dimension_semantics=("parallel",)),
        name="outproj",
    )(*ys, w, res, nw8)


def _vlora_kernel(v_ref, v1_ref, o_ref):
    o_ref[...] = _dot3(v_ref[...], v1_ref[...])


def _vlora(p, v1p, t=512):
    l = p.shape[0]
    t = min(t, l)
    cb = E_V // RW
    return pl.pallas_call(
        _vlora_kernel,
        grid=(l // t,),
        in_specs=[pl.BlockSpec((t, RW), lambda i: (i, cb)),
                  pl.BlockSpec((RW, LANE), lambda i: (0, 0))],
        out_specs=pl.BlockSpec((t, LANE), lambda i: (i, 0)),
        out_shape=jax.ShapeDtypeStruct((l, LANE), F32),
        compiler_params=_params(dimension_semantics=("parallel",)),
        name="rwkv_vlora",
    )(p, v1p)


RT = 1024
RSUB = 1024
RST = 256


def _rwkv_prep_kernel(*refs, has_vmix, nsub):
    (r_ref, k_ref, v_ref, g_ref, wd_ref, ad_ref, pa_ref, w2_ref, a2_ref) = refs[:9]
    pos = 9
    if has_vmix:
        tl_ref, v2_ref, vf_ref = refs[pos:pos + 3]
        pos += 3
    orefs = refs[pos:]
    st = r_ref.shape[0] // nsub
    for j in range(nsub):
        rs = slice(j * st, (j + 1) * st)
        vmix_j = (tl_ref[rs, :], v2_ref[...], vf_ref[rs, :]) if has_vmix else None
        _rwkv_prep_subtile(r_ref[rs, :], k_ref[rs, :], v_ref[rs, :], g_ref[rs, :], wd_ref[rs, :],
                           ad_ref[rs, :], pa_ref, w2_ref[...], a2_ref[...], vmix_j, orefs, j, st)


def _rwkv_prep_subtile(r, k, v, g, wd, ad, pa_ref, w2, a2, vmix, orefs, j, st):
    (atp_ref, rp_ref, bh_ref, kh2_ref, vb_ref, u0_ref, y0_ref, bonus_ref, gs_ref, dl_ref) = orefs
    rs = slice(j * st, (j + 1) * st)
    nchunk = st // RCHUNK
    w0, a0, k_k, k_a = (pa_ref[i:i + 1, :] for i in range(4))
    r_k, v0 = pa_ref[4:5, :], pa_ref[7:8, :]

    logd = -EXP_NEG_HALF * _sigmoid(w0 + _dot3(jnp.tanh(wd), w2))
    a = _sigmoid(a0 + _dot3(ad, a2))
    if vmix is not None:
        tl, v2, vf = vmix
        v = v + (vf - v) * _sigmoid(v0 + _dot3(tl, v2))

    hm = _head_ones(LANE, RH)
    kk = k * k_k
    kk2 = kk * kk
    k2h, k2l = _split2(kk2)
    ss = _dg(k2h, hm) + _dg(k2l, hm)
    kk = kk * lax.rsqrt(jnp.maximum(ss, 1e-24))
    kh = k * (1.0 + (a - 1.0) * k_a)
    av = -kk
    bv = kk * a

    rkh, rkl = _split2(r * kh * r_k)
    bonus_ref[rs, :] = (_dg(rkh, hm) + _dg(rkl, hm)) * v
    gs_ref[rs, :] = _silu(g)

    cum = _seg_cumsum(logd, RCHUNK)
    e_c = jnp.exp(cum)
    e_n = jnp.exp(-cum)
    e_p = jnp.where((_iota(cum.shape, 0) & (RCHUNK - 1)) == 0, 1.0, pltpu.roll(e_c, 1, 0))
    p_end = jnp.exp(cum.reshape(nchunk, RCHUNK, LANE)[:, RCHUNK - 1:RCHUNK, :])
    e_l = e_n * jnp.broadcast_to(p_end, (nchunk, RCHUNK, LANE)).reshape(st, LANE)
    at = av * e_p
    rt = r * e_c
    bt = bv * e_n
    kt = kh * e_n
    bh_ref[rs, :] = (bv * e_l).astype(BF16)
    kh2_ref[rs, :] = (kh * e_l).astype(BF16)
    vb_ref[rs, :] = v.astype(BF16)
    dl_ref[j * nchunk * SUB:(j + 1) * nchunk * SUB, :] = (
        jnp.broadcast_to(p_end, (nchunk, SUB, LANE)).reshape(nchunk * SUB, LANE))

    lane = _iota((RCHUNK, LANE), 1)
    si = lane & (RH - 1)
    ti = _iota((RCHUNK, LANE), 0)
    strict = si < ti
    incl = si <= ti
    eye = jnp.where(si == ti, 1.0, 0.0)
    head0 = lane < RH

    def stack(xc):
        xc = xc.astype(BF16)
        zero = jnp.zeros_like(xc)
        return jnp.concatenate([jnp.where(head0, xc, zero), jnp.where(head0, zero, xc)], axis=0)

    chunks = range(nchunk)
    rows = [slice(c * RCHUNK, (c + 1) * RCHUNK) for c in chunks]
    orows = [slice(j * st + c * RCHUNK, j * st + (c + 1) * RCHUNK) for c in chunks]
    ats = [stack(at[rw]) for rw in rows]
    vs = [stack(v[rw]) for rw in rows]
    amat = []
    for c in chunks:
        lhs = jnp.concatenate([at[rows[c]], rt[rows[c]]], axis=0).astype(BF16)
        rhs = jnp.concatenate([stack(bt[rows[c]]), stack(kt[rows[c]])], axis=0)
        amat.append(_dg(lhs, rhs, NT))
    a_ab = [jnp.where(strict, m[:RCHUNK, :LANE], 0.0) for m in amat]
    a_ak = [jnp.where(strict, m[:RCHUNK, LANE:], 0.0) for m in amat]
    a_rb = [jnp.where(incl, m[RCHUNK:, :LANE], 0.0).astype(BF16) for m in amat]
    a_rk = [jnp.where(incl, m[RCHUNK:, LANE:], 0.0) for m in amat]
    akv2 = [_dg(jnp.concatenate([a_ak[c], a_rk[c]], axis=0).astype(BF16), vs[c]) for c in chunks]
    akv = [m[:RCHUNK] for m in akv2]
    arkv = [m[RCHUNK:] for m in akv2]
    inv = [eye + m for m in a_ab]
    pw = [_dg(m.astype(BF16), stack(m)) for m in a_ab]
    s = 2
    while s < RCHUNK // 2:
        both = [_dg(pw[c].astype(BF16), jnp.concatenate([stack(pw[c]), stack(inv[c])], axis=1))
                for c in chunks]
        pw = [m[:, :LANE] for m in both]
        inv = [inv[c] + both[c][:, LANE:] for c in chunks]
        s *= 2
    inv = [inv[c] + _dg(pw[c].astype(BF16), stack(inv[c])) for c in chunks]
    x = [_dg(inv[c].astype(BF16), jnp.concatenate([ats[c], stack(akv[c])], axis=1)) for c in chunks]
    z = [_dg(a_rb[c], jnp.concatenate([stack(x[c][:, :LANE]), stack(x[c][:, LANE:])], axis=1))
         for c in chunks]
    for c in chunks:
        atp_ref[orows[c], :] = x[c][:, :LANE].astype(BF16)
        u0_ref[orows[c], :] = x[c][:, LANE:]
        rp_ref[orows[c], :] = (rt[rows[c]] + z[c][:, :LANE]).astype(BF16)
        y0_ref[orows[c], :] = z[c][:, LANE:] + arkv[c]


def _rwkv_prep(p, ps, pa, w2p, a2p, vmix):
    l = p.shape[0]
    t = min(RT, l)
    nsub = max(t // RSUB, 1)
    npair = RW // LANE
    has_vmix = vmix is not None

    def main(off):
        cb = off // LANE
        return pl.BlockSpec((t, LANE), lambda q, i: (i, cb + q))

    def main1(off):
        cb = off // LANE
        return pl.BlockSpec((t, LANE), lambda q, i: (i, cb))

    par = pl.BlockSpec((SUB, LANE), lambda q, i: (0, q))
    lora = pl.BlockSpec((LANE, LANE), lambda q, i: (0, q))
    row = pl.BlockSpec((t, LANE), lambda q, i: (i, q))

    in_specs = [main(E_R), main(E_K), main(E_V), main(E_G), main1(S_WD), main1(S_AD), par, lora, lora]
    args = [p] * 4 + [ps] * 2 + [pa, w2p, a2p]
    if has_vmix:
        tl, v2p, p_first = vmix
        in_specs += [pl.BlockSpec((t, LANE), lambda q, i: (i, 0)), lora, main(E_V)]
        args += [tl, v2p, p_first]
    dl_rows = (t // RCHUNK) * SUB
    out_specs = [row] * 9 + [pl.BlockSpec((dl_rows, LANE), lambda q, i: (i, q))]
    sd = jax.ShapeDtypeStruct
    out_shape = ([sd((l, RW), BF16)] * 5 + [sd((l, RW), F32)] * 4
                 + [sd((l // RCHUNK * SUB, RW), F32)])
    return pl.pallas_call(
        functools.partial(_rwkv_prep_kernel, has_vmix=has_vmix, nsub=nsub),
        grid=(npair, l // t),
        in_specs=in_specs,
        out_specs=out_specs,
        out_shape=out_shape,
        compiler_params=_params(dimension_semantics=("parallel", "parallel")),
        name="rwkv_prep",
    )(*args)


def _rwkv_scan_kernel(atp_ref, rp_ref, bh_ref, kh_ref, vb_ref, u0_ref, y0_ref,
                      bonus_ref, gs_ref, dl_ref, pa_ref, o_ref, s_ref, y_ref):
    @pl.when(pl.program_id(0) == 0)
    def _():
        s_ref[...] = jnp.zeros_like(s_ref)

    t = atp_ref.shape[0]
    npair = RW // LANE
    bd = _idiv(_iota((LANE, LANE), 0), RH) == _idiv(_iota((LANE, LANE), 1), RH)
    hmean = _head_ones(LANE, RH, 1.0 / RH)

    pairs = range(npair)
    lanes = [slice(q * LANE, (q + 1) * LANE) for q in pairs]
    for c in range(t // RCHUNK):
        rows = slice(c * RCHUNK, (c + 1) * RCHUNK)
        s_old = [s_ref[q] for q in pairs]
        o = [_dg(jnp.concatenate([atp_ref[rows, lanes[q]], rp_ref[rows, lanes[q]]], axis=0),
                 s_old[q].astype(BF16), NT) for q in pairs]
        ds = []
        for q in pairs:
            u = o[q][:RCHUNK] + u0_ref[rows, lanes[q]]
            y_ref[rows, lanes[q]] = o[q][RCHUNK:] + y0_ref[rows, lanes[q]]
            uv = jnp.concatenate([u.astype(BF16), vb_ref[rows, lanes[q]]], axis=0)
            bk = jnp.concatenate([bh_ref[rows, lanes[q]], kh_ref[rows, lanes[q]]], axis=0)
            ds.append(_dg(uv, bk, TN))
        for q in pairs:
            dl = dl_ref[c * SUB:c * SUB + 1, lanes[q]]
            s_ref[q] = s_old[q] * dl + jnp.where(bd, ds[q], 0.0)

    for q in range(npair):
        sl = slice(q * LANE, (q + 1) * LANE)
        y = y_ref[:, sl]
        yh, yl = _split2(y)
        mean = _dg(yh, hmean) + _dg(yl, hmean)
        d = y - mean
        var = _bdot(d * d, hmean)
        yn = d * lax.rsqrt(var + RWKV_GN_EPS) * pa_ref[5:6, sl] + pa_ref[6:7, sl]
        o_ref[:, sl] = ((yn + bonus_ref[:, sl]) * gs_ref[:, sl]).astype(BF16)


def _rwkv_scan(atp, rp, bh, kh, vb, u0, y0, bonus, gs, dl, pa):
    l = atp.shape[0]
    t = min(RST, l)
    row = pl.BlockSpec((t, RW), lambda i: (i, 0))
    return pl.pallas_call(
        _rwkv_scan_kernel,
        grid=(l // t,),
        in_specs=[row] * 9 + [pl.BlockSpec(((t // RCHUNK) * SUB, RW), lambda i: (i, 0)),
                              pl.BlockSpec((SUB, RW), lambda i: (0, 0))],
        out_specs=row,
        out_shape=jax.ShapeDtypeStruct((l, RW), BF16),
        scratch_shapes=[pltpu.VMEM((RW // LANE, LANE, LANE), F32), pltpu.VMEM((t, RW), F32)],
        compiler_params=_params(dimension_semantics=("arbitrary",)),
        name="rwkv_scan",
    )(atp, rp, bh, kh, vb, u0, y0, bonus, gs, dl, pa)


def _ssd_conv(main_ref, halo_ref, wb_ref, buf_ref, first):
    q = main_ref.shape[0]
    buf_ref[0:SUB, :] = jnp.where(first, 0.0, halo_ref[...])
    buf_ref[SUB:SUB + q, :] = main_ref[...]
    acc = wb_ref[SCONV:SCONV + 1, :]
    for j in range(SCONV):
        acc = acc + buf_ref[pl.ds(SUB - (SCONV - 1) + j, q), :] * wb_ref[j:j + 1, :]
    return _silu(acc)


def _ssd_kernel(z_ref, x_ref, b_ref, c_ref, dt_ref, xh_ref, bhalo_ref, chalo_ref,
                cwx_ref, cwb_ref, cwc_ref, hp_ref, dx_ref, nw_ref,
                o_ref, prev_ref, xbuf_ref, bbuf_ref, cbuf_ref, ybuf_ref):
    first = pl.program_id(0) == 0
    q = z_ref.shape[0]
    groups = range(SGROUPS)

    @pl.when(first)
    def _():
        prev_ref[...] = jnp.zeros_like(prev_ref)

    xs = _ssd_conv(x_ref, xh_ref, cwx_ref, xbuf_ref, first)
    bmb = _ssd_conv(b_ref, bhalo_ref, cwb_ref, bbuf_ref, first).astype(BF16)
    cmb = _ssd_conv(c_ref, chalo_ref, cwc_ref, cbuf_ref, first).astype(BF16)

    dt = _softplus(dt_ref[...] + hp_ref[0:1, :])
    a_neg = -jnp.exp(hp_ref[1:2, :])
    acs = _seg_cumsum(dt * a_neg, q)
    acs_t = acs.T
    acs_last = acs[q - 1:q, :]

    expand = jnp.where(_iota((LANE, SW), 0) == _idiv(_iota((LANE, SW), 1), SHEAD) + S_DT_LANE0,
                       1.0, 0.0).astype(BF16)
    dt_e = _dot_sel(dt, expand)
    eacs_e = _dot_sel(jnp.exp(acs), expand)
    ds_e = _dot_sel(jnp.exp(acs_last - acs), expand)
    xc = xs * dt_e
    xcd = (xc * ds_e).astype(BF16)

    gl = [slice(g * SSTATE, (g + 1) * SSTATE) for g in groups]
    gc = [slice(g * SGW, (g + 1) * SGW) for g in groups]
    scores = [_dg(cmb[:, gl[g]], bmb[:, gl[g]], NT) for g in groups]
    prev = prev_ref[...]
    prevb = prev.astype(BF16)
    y_off = [_dg(cmb[:, gl[g]], prevb[:, gc[g]]) for g in groups]
    states = [_dg(bmb[:, gl[g]], xcd[:, gc[g]], TN) for g in groups]
    for g in groups:
        prev_ref[:, gc[g]] = prev[:, gc[g]] * eacs_e[q - 1:q, gc[g]] + states[g]

    causal = _iota((q, q), 1) <= _iota((q, q), 0)
    for h in range(SW // SHEAD):
        lane = S_DT_LANE0 + h
        hc = slice(h * SHEAD, (h + 1) * SHEAD)
        diff = acs[:, lane:lane + 1] - acs_t[lane:lane + 1, :]
        lm = jnp.exp(jnp.where(causal, diff, -jnp.inf))
        ybuf_ref[:, hc] = _bdot(scores[h // SHG] * lm, xc[:, hc])
    for g in groups:
        y = ybuf_ref[:, gc[g]] + y_off[g] * eacs_e[:, gc[g]] + dx_ref[0:1, gc[g]] * xs[:, gc[g]]
        y = y * _silu(z_ref[:, gc[g]])
        ms = jnp.mean(y * y, axis=-1, keepdims=True)
        o_ref[:, gc[g]] = (y * lax.rsqrt(ms + EPS) * nw_ref[0:1, gc[g]]).astype(BF16)


def _ssd(p, ps, cw8, hp, dx8, nw8):
    l = p.shape[0]
    q = SCHUNK
    hb = q // SUB
    gn = SGROUPS * SSTATE

    def halo_idx(i):
        return jnp.maximum(i * hb - 1, 0)

    zb, xb = T_Z // SW, T_X // SW
    bb, cb, db = T_B // gn, T_C // gn, S_DT // LANE
    in_specs = [
        pl.BlockSpec((q, SW), lambda i: (i, zb)),
        pl.BlockSpec((q, SW), lambda i: (i, xb)),
        pl.BlockSpec((q, gn), lambda i: (i, bb)),
        pl.BlockSpec((q, gn), lambda i: (i, cb)),
        pl.BlockSpec((q, LANE), lambda i: (i, db)),
        pl.BlockSpec((SUB, SW), lambda i: (halo_idx(i), xb)),
        pl.BlockSpec((SUB, gn), lambda i: (halo_idx(i), bb)),
        pl.BlockSpec((SUB, gn), lambda i: (halo_idx(i), cb)),
        pl.BlockSpec((SUB, SW), lambda i: (0, 0)),
        pl.BlockSpec((SUB, gn), lambda i: (0, SW // gn)),
        pl.BlockSpec((SUB, gn), lambda i: (0, SW // gn + 1)),
        pl.BlockSpec((SUB, LANE), lambda i: (0, 0)),
        pl.BlockSpec((SUB, SW), lambda i: (0, 0)),
        pl.BlockSpec((SUB, SW), lambda i: (0, 0)),
    ]
    return pl.pallas_call(
        _ssd_kernel,
        grid=(l // q,),
        in_specs=in_specs,
        out_specs=pl.BlockSpec((q, SW), lambda i: (i, 0)),
        out_shape=jax.ShapeDtypeStruct((l, SW), BF16),
        scratch_shapes=[pltpu.VMEM((SSTATE, SW), F32),
                        pltpu.VMEM((q + SUB, SW), F32),
                        pltpu.VMEM((q + SUB, gn), F32),
                        pltpu.VMEM((q + SUB, gn), F32),
                        pltpu.VMEM((q, SW), F32)],
        compiler_params=_params(dimension_semantics=("arbitrary",)),
        name="ssd",
    )(p, p, p, p, ps, p, p, p, cw8, cw8, cw8, hp, dx8, nw8)


GT = 256


def _gla_kernel(q_ref, k_ref, v_ref, g_ref, gd_ref, gkw_ref, gkb_ref, nw_ref, o_ref, st_ref, oi_ref):
    @pl.when(pl.program_id(0) == 0)
    def _():
        st_ref[...] = jnp.zeros_like(st_ref)

    t = q_ref.shape[0]
    nchunk = t // GCHUNK
    heads = range(GHEADS)
    kl = [slice(h * GHK, (h + 1) * GHK) for h in heads]
    vl = [slice(h * GHV, (h + 1) * GHV) for h in heads]
    gk = _log_sigmoid(_dot3(gd_ref[...], gkw_ref[...]) + gkb_ref[0:1, :]) * (1.0 / 16.0)
    bc = _seg_cumsum(gk, GCHUNK)
    mid = _chunk_row(bc, GCHUNK, GCHUNK // 2)
    last = _chunk_row(bc, GCHUNK, GCHUNK - 1)
    qs = q_ref[...] * (GHK ** -0.5)
    k = k_ref[...]
    vb = v_ref[...].astype(BF16)
    qm = (qs * jnp.exp(bc - mid)).astype(BF16)
    km = (k * jnp.exp(mid - bc)).astype(BF16)
    q_in = (qs * jnp.exp(bc)).astype(BF16)
    k_st = (k * jnp.exp(last - bc)).astype(BF16)
    dlast = jnp.exp(last)

    ri = _iota((t, t), 0)
    ci = _iota((t, t), 1)
    incl = (_idiv(ri, GCHUNK) == _idiv(ci, GCHUNK)) & (ci <= ri)
    attn = [jnp.where(incl, _dg(qm[:, kl[h]], km[:, kl[h]], NT), 0.0).astype(BF16) for h in heads]
    o_intra = [_dg(attn[h], vb[:, vl[h]]) for h in heads]

    for c in range(nchunk):
        rows = slice(c * GCHUNK, (c + 1) * GCHUNK)
        st = [st_ref[h] for h in heads]
        for h in heads:
            oi_ref[rows, vl[h]] = _dg(q_in[rows, kl[h]], st[h].astype(BF16), NT)
        for h in heads:
            st_ref[h] = (st[h] * dlast[c * GCHUNK:c * GCHUNK + 1, kl[h]]
                         + _dg(vb[rows, vl[h]], k_st[rows, kl[h]], TN))
    for h in heads:
        o = o_intra[h] + oi_ref[:, vl[h]]
        ms = jnp.mean(o * o, axis=-1, keepdims=True)
        o = o * lax.rsqrt(ms + EPS) * nw_ref[0:1, :]
        o_ref[:, vl[h]] = (o * _silu(g_ref[:, vl[h]])).astype(BF16)


def _gla(p, pgd, gkwp, gkb8, nw8):
    l = p.shape[0]
    t = min(GT, l)
    return pl.pallas_call(
        _gla_kernel,
        grid=(l // t,),
        in_specs=[pl.BlockSpec((t, GKEY), lambda i: (i, O_Q // GKEY)),
                  pl.BlockSpec((t, GKEY), lambda i: (i, O_K // GKEY)),
                  pl.BlockSpec((t, GVAL), lambda i: (i, O_V // GVAL)),
                  pl.BlockSpec((t, GVAL), lambda i: (i, O_G // GVAL)),
                  pl.BlockSpec((t, LANE), lambda i: (i, 0)),
                  pl.BlockSpec((LANE, GKEY), lambda i: (0, 0)),
                  pl.BlockSpec((SUB, GKEY), lambda i: (0, 0)),
                  pl.BlockSpec((SUB, GHV), lambda i: (0, 0))],
        out_specs=pl.BlockSpec((t, GVAL), lambda i: (i, 0)),
        out_shape=jax.ShapeDtypeStruct((l, GVAL), BF16),
        scratch_shapes=[pltpu.VMEM((GHEADS, GHV, GHK), F32), pltpu.VMEM((t, GVAL), F32)],
        compiler_params=_params(dimension_semantics=("arbitrary",)),
        name="gla",
    )(p, p, p, p, pgd, gkwp, gkb8, nw8)


def _rows8(*rows):
    n = rows[0].shape[0]
    parts = [r.astype(F32)[None, :] for r in rows]
    if len(rows) < SUB:
        parts.append(jnp.zeros((SUB - len(rows), n), F32))
    return jnp.concatenate(parts, axis=0)


def _pad_cols(w, n):
    return jnp.concatenate([w, jnp.zeros((w.shape[0], n - w.shape[1]), w.dtype)], axis=1)


def _pad_rows(w, n):
    return jnp.concatenate([w, jnp.zeros((n - w.shape[0], w.shape[1]), w.dtype)], axis=0)


def _small_col0(j):
    return pl.multiple_of(jnp.where(j == 2, S_DT_COL0, E_WD + j * (E_AD - E_WD)), SUB)


def kernel(x, norm_w, final_norm_w, w_in_even, w_out_even, rwkv_mu, rwkv_w0, rwkv_w2, rwkv_a0, rwkv_a2, rwkv_k_k, rwkv_k_a, rwkv_r_k, rwkv_ln_w, rwkv_ln_b, rwkv_v0, rwkv_v1, rwkv_v2, ssm_conv_w, ssm_conv_b, ssm_dt_bias, ssm_A_log, ssm_D, ssm_norm_w, w_in_odd, w_out_odd, gla_gk_w, gla_gk_b, gla_norm_w):
    bsz, l, d = x.shape
    depth = norm_w.shape[0]
    w_out_even_bf = w_out_even.astype(BF16)
    w_out_odd_bf = w_out_odd.astype(BF16)
    wt_even = jnp.swapaxes(w_in_even, 1, 2)
    wt_odd = jnp.swapaxes(w_in_odd, 1, 2)
    outs = []
    for b in range(bsz):
        res = x[b]
        h = _rmsnorm(res, _rows8(norm_w[0]))
        p_first = None
        for layer in range(depth):
            i = layer // 2
            last = layer == depth - 1
            nw_next = _rows8(final_norm_w if last else norm_w[layer + 1])
            if layer % 2 == 0:
                mu = rwkv_mu[i]
                zpad = jnp.zeros((LANE - DECAY_LORA,), F32)
                mu_small = jnp.concatenate([mu[E_WD:E_AD], zpad, mu[E_AD:E_ZXBC], zpad,
                                            jnp.zeros((S_COLS - 2 * LANE,), F32)])
                p = _inproj(h, wt_even, i, lambda j: j * 512, E_MAIN // 512, 512, mu8=_rows8(mu[:E_MAIN]))
                pt = _inproj(h, wt_even, i, lambda j: E_ZXBC + j * 512, T_COLS // 512, 512)
                ps = _inproj(h, wt_even, i, _small_col0, S_COLS // LANE, LANE, mu8=_rows8(mu_small))
                v0 = rwkv_v0[i - 1] if i > 0 else jnp.zeros((RW,), F32)
                pa = _rows8(rwkv_w0[i], rwkv_a0[i], rwkv_k_k[i], rwkv_k_a[i],
                            rwkv_r_k[i].reshape(RW), rwkv_ln_w[i], rwkv_ln_b[i], v0)
                w2p = _pad_rows(rwkv_w2[i], LANE)
                a2p = _pad_rows(rwkv_a2[i], LANE)
                if i == 0:
                    vmix = None
                    p_first = p
                else:
                    tl = _vlora(p, _pad_cols(rwkv_v1[i - 1], LANE))
                    vmix = (tl, _pad_rows(rwkv_v2[i - 1], LANE), p_first)
                (atp, rp, bh, kh, vb, u0, y0, bonus, gs, dl) = _rwkv_prep(p, ps, pa, w2p, a2p, vmix)
                y_a = _rwkv_scan(atp, rp, bh, kh, vb, u0, y0, bonus, gs, dl, pa)

                cw8 = jnp.concatenate([ssm_conv_w[i], ssm_conv_b[i][None, :],
                                       jnp.zeros((SUB - SCONV - 1, ssm_conv_w.shape[2]), F32)], axis=0)
                lead = jnp.zeros((S_DT_LANE0,), F32)
                hp = _rows8(jnp.concatenate([lead, ssm_dt_bias[i]]), jnp.concatenate([lead, ssm_A_log[i]]))
                dx8 = _rows8(jnp.repeat(ssm_D[i], SHEAD))
                y_b = _ssd(pt, ps, cw8, hp, dx8, _rows8(ssm_norm_w[i]))
                o = _outproj([y_a, y_b], w_out_even_bf, i, res, nw_next, last)
            else:
                p = _inproj(h, wt_odd, i, lambda j: j * 512, O_MAIN // 512, 512)
                pgd = _inproj(h, wt_odd, i, lambda j: O_GD_COL0, 1, LANE)
                gkwp = jnp.concatenate([jnp.zeros((O_GD_LANE0, GKEY), F32), gla_gk_w[i]], axis=0)
                y = _gla(p, pgd, gkwp, _rows8(gla_gk_b[i]), _rows8(gla_norm_w[i]))
                o = _outproj([y], w_out_odd_bf, i, res, nw_next, last)
            if last:
                res = o[0]
            else:
                res, h = o
        outs.append(res)
    return jnp.stack(outs).astype(x.dtype)
```

```python
import functools
import math

import jax
import jax.numpy as jnp
from jax import lax
from jax.experimental import pallas as pl
from jax.experimental.pallas import tpu as pltpu

F32 = jnp.float32
BF16 = jnp.bfloat16

D_MODEL = 2048
EPS = 1e-5
RW = 2048
RH = 64
DECAY_LORA = 96
AAA_LORA = 96
MV_LORA = 64
RWKV_GN_EPS = 64e-5
RCHUNK = 64
SW = 2048
SHEAD = 64
SGROUPS = 4
SSTATE = 128
SCONV = 4
SCHUNK = 128
SGW = SW // SGROUPS
SHG = SGW // SHEAD
GHEADS = 4
GKEY = 1024
GVAL = 2048
GHK = 256
GHV = 512
GRANK = 16
GCHUNK = 64

LANE = 128
SUB = 8
VMEM_LIMIT = 56 * 1024 * 1024

E_R, E_K, E_V, E_G = 0, 2048, 4096, 6144
E_MAIN = 8192
E_WD = 8192
E_AD = 8288
E_ZXBC = 8384
E_COLS = 13536
T_Z, T_X = 0, 2048
T_B, T_C = 4096, 4608
T_COLS = 5120
S_BLK = 256
S_LORA = 0
S_COLS = 2 * S_BLK
S_DT = S_COLS - LANE
S_DT_COL0 = E_COLS - S_BLK
S_DT_LANE0 = LANE - SW // SHEAD
O_COLS = 6160
O_GD_COL0 = O_COLS - LANE
O_GD_LANE0 = LANE - GRANK
O_Q, O_K, O_V, O_G = 0, 1024, 2048, 4096
O_MAIN = 6144

NN = (((1,), (0,)), ((), ()))
NT = (((1,), (1,)), ((), ()))
TN = (((0,), (0,)), ((), ()))


def _dg(a, b, dims=NN):
    return lax.dot_general(a, b, dims, preferred_element_type=F32)


def _bdot(a, b, dims=NN):
    return _dg(a.astype(BF16), b.astype(BF16), dims)


def _split2(x):
    hi = x.astype(BF16)
    lo = (x - hi.astype(F32)).astype(BF16)
    return hi, lo


def _split3(x):
    hi = x.astype(BF16)
    r = x - hi.astype(F32)
    mid = r.astype(BF16)
    lo = (r - mid.astype(F32)).astype(BF16)
    return hi, mid, lo


def _dot3(a, b, dims=NN):
    ah, al = _split2(a)
    bh, bl = _split2(b)
    return _dg(ah, bh, dims) + (_dg(ah, bl, dims) + _dg(al, bh, dims))


def _dot_sel(a, sel_bf16, dims=NN):
    ah, am, al = _split3(a)
    return _dg(ah, sel_bf16, dims) + (_dg(am, sel_bf16, dims) + _dg(al, sel_bf16, dims))


EXP_NEG_HALF = math.exp(-0.5)


def _sigmoid(x):
    return 0.5 * jnp.tanh(0.5 * x) + 0.5


def _silu(x):
    return x * _sigmoid(x)


def _log_sigmoid(x):
    return jnp.minimum(x, 0.0) - jnp.log(1.0 + jnp.exp(-jnp.abs(x)))


def _softplus(x):
    return jnp.maximum(x, 0.0) + jnp.log(1.0 + jnp.exp(-jnp.abs(x)))


def _iota(shape, axis):
    return lax.broadcasted_iota(jnp.int32, shape, axis)


def _idiv(x, n):
    return lax.shift_right_logical(x, n.bit_length() - 1)


def _seg_cumsum(x, chunk):
    rin = _iota(x.shape, 0) & (chunk - 1)
    s = 1
    while s < chunk:
        x = x + jnp.where(rin >= s, pltpu.roll(x, s, 0), 0.0)
        s *= 2
    return x


def _chunk_row(x, chunk, r):
    t, w = x.shape
    x3 = x.reshape(t // chunk, chunk, w)
    return jnp.broadcast_to(x3[:, r:r + 1, :], x3.shape).reshape(t, w)


def _head_ones(n, head, scale=1.0):
    r = _idiv(_iota((n, n), 0), head)
    c = _idiv(_iota((n, n), 1), head)
    return jnp.where(r == c, scale, 0.0).astype(BF16)


def _params(**kw):
    return pltpu.CompilerParams(vmem_limit_bytes=VMEM_LIMIT, **kw)


def _rmsnorm_kernel(x_ref, w_ref, o_ref):
    x = x_ref[...]
    ms = jnp.mean(x * x, axis=-1, keepdims=True)
    o_ref[...] = (x * lax.rsqrt(ms + EPS) * w_ref[0:1, :]).astype(BF16)


def _rmsnorm(x, w8, tm=512):
    l, d = x.shape
    return pl.pallas_call(
        _rmsnorm_kernel,
        grid=(l // tm,),
        in_specs=[pl.BlockSpec((tm, d), lambda i: (i, 0)),
                  pl.BlockSpec((SUB, d), lambda i: (0, 0))],
        out_specs=pl.BlockSpec((tm, d), lambda i: (i, 0)),
        out_shape=jax.ShapeDtypeStruct((l, d), BF16),
        compiler_params=_params(dimension_semantics=("parallel",)),
        name="rmsnorm",
    )(x, w8)


HALO = 16
PTN = 512


def _inproj_kernel(*refs, shift):
    if shift:
        h_ref, hh_ref, w_ref, mu_ref, o_ref = refs
    else:
        h_ref, w_ref, o_ref = refs
    w = w_ref[0].astype(BF16)
    p = _dg(h_ref[...], w, NT)
    if shift:
        prev = _dg(hh_ref[...], w, NT)[HALO - 1:HALO, :]
        prev = jnp.where(pl.program_id(0) == 0, 0.0, prev)
        pprev = jnp.where(_iota(p.shape, 0) == 0, prev, pltpu.roll(p, 1, 0))
        p = p + (pprev - p) * mu_ref[0:1, :]
    o_ref[...] = p


def _inproj(h, wt, layer, col0, nblk, tn, mu8=None, tm=2048):
    l, d = h.shape
    tm = min(tm, l)
    shift = mu8 is not None
    in_specs = [pl.BlockSpec((tm, d), lambda i, j: (i, 0))]
    args = [h]
    if shift:
        hb = tm // HALO
        in_specs.append(pl.BlockSpec((HALO, d), lambda i, j: (jnp.maximum(i * hb - 1, 0), 0)))
        args.append(h)
    def w_map(i, j):
        c = col0(j)
        return (layer, c if isinstance(c, int) else pl.multiple_of(c, SUB), 0)

    in_specs.append(pl.BlockSpec((pl.Element(1), pl.Element(tn), pl.Element(d)), w_map))
    args.append(wt)
    if shift:
        in_specs.append(pl.BlockSpec((SUB, tn), lambda i, j: (0, j)))
        args.append(mu8)
    return pl.pallas_call(
        functools.partial(_inproj_kernel, shift=shift),
        grid=(l // tm, nblk),
        in_specs=in_specs,
        out_specs=pl.BlockSpec((tm, tn), lambda i, j: (i, j)),
        out_shape=jax.ShapeDtypeStruct((l, nblk * tn), F32),
        compiler_params=_params(dimension_semantics=("parallel", "parallel")),
        name="inproj",
    )(*args)


def _outproj_kernel(*refs, n_in, final):
    y_refs = refs[:n_in]
    w_ref, res_ref, nw_ref = refs[n_in:n_in + 3]
    outs = refs[n_in + 3:]
    r = res_ref[...]
    for s in range(n_in):
        kw = y_refs[s].shape[1]
        r = r + jnp.dot(y_refs[s][...], w_ref[s * kw:(s + 1) * kw, :], preferred_element_type=F32)
    ms = jnp.mean(r * r, axis=-1, keepdims=True)
    hn = r * lax.rsqrt(ms + EPS) * nw_ref[0:1, :]
    if final:
        outs[0][...] = hn
    else:
        outs[0][...] = r
        outs[1][...] = hn.astype(BF16)


def _outproj(ys, w, layer, res, nw8, final, tm=512):
    l, d = res.shape
    n_in = len(ys)
    kdim = w.shape[1]
    tm = min(tm, l)
    in_specs = [pl.BlockSpec((tm, y.shape[1]), lambda i: (i, 0)) for y in ys]
    in_specs += [pl.BlockSpec((None, kdim, d), lambda i: (layer, 0, 0), pipeline_mode=pl.Buffered(1)),
                 pl.BlockSpec((tm, d), lambda i: (i, 0)),
                 pl.BlockSpec((SUB, d), lambda i: (0, 0))]
    row_spec = pl.BlockSpec((tm, d), lambda i: (i, 0))
    if final:
        out_specs = [row_spec]
        out_shape = [jax.ShapeDtypeStruct((l, d), F32)]
    else:
        out_specs = [row_spec, row_spec]
        out_shape = [jax.ShapeDtypeStruct((l, d), F32), jax.ShapeDtypeStruct((l, d), BF16)]
    return pl.pallas_call(
        functools.partial(_outproj_kernel, n_in=n_in, final=final),
        grid=(l // tm,),
        in_specs=in_specs,
        out_specs=out_specs,
        out_shape=out_shape,
        compiler_params=_params(dimension_semantics=("parallel",)),
        name="outproj",
    )(*ys, w, res, nw8)


def _vlora_kernel(v_ref, v1_ref, o_ref):
    o_ref[...] = _dot3(v_ref[...], v1_ref[...])


def _vlora(p, v1p, t=512):
    l = p.shape[0]
    t = min(t, l)
    cb = E_V // RW
    return pl.pallas_call(
        _vlora_kernel,
        grid=(l // t,),
        in_specs=[pl.BlockSpec((t, RW), lambda i: (i, cb)),
                  pl.BlockSpec((RW, LANE), lambda i: (0, 0))],
        out_specs=pl.BlockSpec((t, LANE), lambda i: (i, 0)),
        out_shape=jax.ShapeDtypeStruct((l, LANE), F32),
        compiler_params=_params(dimension_semantics=("parallel",)),
        name="rwkv_vlora",
    )(p, v1p)


RT = 1024
RHALF = 1024
REW = 1024
RST = 256


def _rwkv_prep_kernel(*refs, has_vmix):
    (r_ref, k_ref, v_ref, g_ref, lo_ref, pa_ref, w2_ref, a2_ref) = refs[:8]
    pos = 8
    if has_vmix:
        tl_ref, v2_ref, vf_ref = refs[pos:pos + 3]
        pos += 3
    (atp_ref, rp_ref, bh_ref, kh2_ref, vb_ref, u0_ref, y0_ref, bonus_ref, gs_ref, dl_ref) = refs[pos:]
    t = r_ref.shape[0]
    half = min(RHALF, t)
    blk = min(REW, t)
    cpb = blk // RCHUNK

    w0, a0, k_k, k_a = (pa_ref[i:i + 1, :] for i in range(4))
    r_k, v0 = pa_ref[4:5, :], pa_ref[7:8, :]
    hm = _head_ones(LANE, RH)
    first_row = (_iota((blk, LANE), 0) & (RCHUNK - 1)) == 0

    def front(r0):
        rs = slice(r0, r0 + blk)
        r, k, v, g, lo = r_ref[rs, :], k_ref[rs, :], v_ref[rs, :], g_ref[rs, :], lo_ref[rs, :]
        logd = -EXP_NEG_HALF * _sigmoid(w0 + _dot3(jnp.tanh(lo), w2_ref[...]))
        a = _sigmoid(a0 + _dot3(lo, a2_ref[...]))
        if has_vmix:
            v = v + (vf_ref[rs, :] - v) * _sigmoid(v0 + _dot3(tl_ref[rs, :], v2_ref[...]))

        kk = k * k_k
        k2h, k2l = _split2(kk * kk)
        ss = _dg(k2h, hm) + _dg(k2l, hm)
        kk = kk * lax.rsqrt(jnp.maximum(ss, 1e-24))
        kh = k * (1.0 + (a - 1.0) * k_a)
        av = -kk
        bv = kk * a

        rkh, rkl = _split2(r * kh * r_k)
        bonus_ref[rs, :] = (_dg(rkh, hm) + _dg(rkl, hm)) * v
        gs_ref[rs, :] = _silu(g)

        cum = _seg_cumsum(logd, RCHUNK)
        e_c = jnp.exp(cum)
        e_n = jnp.exp(-cum)
        e_p = jnp.where(first_row, 1.0, pltpu.roll(e_c, 1, 0))
        p_end = jnp.exp(cum.reshape(cpb, RCHUNK, LANE)[:, RCHUNK - 1:RCHUNK, :])
        e_l = e_n * jnp.broadcast_to(p_end, (cpb, RCHUNK, LANE)).reshape(blk, LANE)
        at = av * e_p
        rt = r * e_c
        bt = bv * e_n
        kt = kh * e_n
        bh_ref[rs, :] = (bv * e_l).astype(BF16)
        kh2_ref[rs, :] = (kh * e_l).astype(BF16)
        vb_ref[rs, :] = v.astype(BF16)
        c0 = r0 // RCHUNK
        dl_ref[c0 * SUB:(c0 + cpb) * SUB, :] = (
            jnp.broadcast_to(p_end, (cpb, SUB, LANE)).reshape(cpb * SUB, LANE))
        out = []
        for c in range(cpb):
            cr = slice(c * RCHUNK, (c + 1) * RCHUNK)
            out.append((r0 + c * RCHUNK, at[cr], rt[cr], bt[cr], kt[cr], v[cr]))
        return out

    lane = _iota((RCHUNK, LANE), 1)
    si = lane & (RH - 1)
    ti = _iota((RCHUNK, LANE), 0)
    strict = si < ti
    incl = si <= ti
    eye = jnp.where(si == ti, 1.0, 0.0)
    head0 = lane < RH

    def stack(xc):
        xc = xc.astype(BF16)
        zero = jnp.zeros_like(xc)
        return jnp.concatenate([jnp.where(head0, xc, zero), jnp.where(head0, zero, xc)], axis=0)

    def chains(ops):
        chunks = range(len(ops))
        ats = [stack(o[1]) for o in ops]
        vs = [stack(o[5]) for o in ops]
        amat = []
        for (_, at, rt, bt, kt, _) in ops:
            lhs = jnp.concatenate([at, rt], axis=0).astype(BF16)
            rhs = jnp.concatenate([stack(bt), stack(kt)], axis=0)
            amat.append(_dg(lhs, rhs, NT))
        yield
        a_ab = [jnp.where(strict, m[:RCHUNK, :LANE], 0.0) for m in amat]
        a_ak = [jnp.where(strict, m[:RCHUNK, LANE:], 0.0) for m in amat]
        a_rb = [jnp.where(incl, m[RCHUNK:, :LANE], 0.0).astype(BF16) for m in amat]
        a_rk = [jnp.where(incl, m[RCHUNK:, LANE:], 0.0) for m in amat]
        akv2 = [_dg(jnp.concatenate([a_ak[c], a_rk[c]], axis=0).astype(BF16), vs[c]) for c in chunks]
        akv = [m[:RCHUNK] for m in akv2]
        arkv = [m[RCHUNK:] for m in akv2]
        yield
        inv = [eye + m for m in a_ab]
        pw = [_dg(m.astype(BF16), stack(m)) for m in a_ab]
        yield
        s = 2
        while s < RCHUNK // 2:
            both = [_dg(pw[c].astype(BF16), jnp.concatenate([stack(pw[c]), stack(inv[c])], axis=1))
                    for c in chunks]
            pw = [m[:, :LANE] for m in both]
            inv = [inv[c] + both[c][:, LANE:] for c in chunks]
            s *= 2
            yield
        inv = [inv[c] + _dg(pw[c].astype(BF16), stack(inv[c])) for c in chunks]
        yield
        x = [_dg(inv[c].astype(BF16), jnp.concatenate([ats[c], stack(akv[c])], axis=1)) for c in chunks]
        yield
        z = [_dg(a_rb[c], jnp.concatenate([stack(x[c][:, :LANE]), stack(x[c][:, LANE:])], axis=1))
             for c in chunks]
        for c in chunks:
            orow = slice(ops[c][0], ops[c][0] + RCHUNK)
            atp_ref[orow, :] = x[c][:, :LANE].astype(BF16)
            u0_ref[orow, :] = x[c][:, LANE:]
            rp_ref[orow, :] = (ops[c][2] + z[c][:, :LANE]).astype(BF16)
            y0_ref[orow, :] = z[c][:, LANE:] + arkv[c]

    halves = [list(range(h0, h0 + half, blk)) for h0 in range(0, t, half)]
    ops = [o for r0 in halves[0] for o in front(r0)]
    for hi in range(len(halves)):
        pending = list(halves[hi + 1]) if hi + 1 < len(halves) else []
        nxt = []
        for _ in chains(ops):
            if pending:
                nxt += front(pending.pop(0))
        while pending:
            nxt += front(pending.pop(0))
        ops = nxt


def _rwkv_prep(p, ps, pa, w2p, a2p, vmix):
    l = p.shape[0]
    t = min(RT, l)
    npair = RW // LANE
    has_vmix = vmix is not None

    def main(off):
        cb = off // LANE
        return pl.BlockSpec((t, LANE), lambda q, i: (i, cb + q))

    par = pl.BlockSpec((SUB, LANE), lambda q, i: (0, q))
    lora = pl.BlockSpec((LANE, LANE), lambda q, i: (0, q))
    lora_in = pl.BlockSpec((S_BLK, LANE), lambda q, i: (0, q))
    row = pl.BlockSpec((t, LANE), lambda q, i: (i, q))

    in_specs = [main(E_R), main(E_K), main(E_V), main(E_G),
                pl.BlockSpec((t, S_BLK), lambda q, i: (i, S_LORA // S_BLK)), par, lora_in, lora_in]
    args = [p] * 4 + [ps, pa, w2p, a2p]
    if has_vmix:
        tl, v2p, p_first = vmix
        in_specs += [pl.BlockSpec((t, LANE), lambda q, i: (i, 0)), lora, main(E_V)]
        args += [tl, v2p, p_first]
    dl_rows = (t // RCHUNK) * SUB
    out_specs = [row] * 9 + [pl.BlockSpec((dl_rows, LANE), lambda q, i: (i, q))]
    sd = jax.ShapeDtypeStruct
    out_shape = ([sd((l, RW), BF16)] * 5 + [sd((l, RW), F32)] * 4
                 + [sd((l // RCHUNK * SUB, RW), F32)])
    return pl.pallas_call(
        functools.partial(_rwkv_prep_kernel, has_vmix=has_vmix),
        grid=(npair, l // t),
        in_specs=in_specs,
        out_specs=out_specs,
        out_shape=out_shape,
        compiler_params=_params(dimension_semantics=("parallel", "parallel")),
        name="rwkv_prep",
    )(*args)


def _rwkv_scan_kernel(atp_ref, rp_ref, bh_ref, kh_ref, vb_ref, u0_ref, y0_ref,
                      bonus_ref, gs_ref, dl_ref, pa_ref, o_ref, s_ref, y_ref):
    @pl.when(pl.program_id(0) == 0)
    def _():
        s_ref[...] = jnp.zeros_like(s_ref)

    t = atp_ref.shape[0]
    npair = RW // LANE
    bd = _idiv(_iota((LANE, LANE), 0), RH) == _idiv(_iota((LANE, LANE), 1), RH)
    hmean = _head_ones(LANE, RH, 1.0 / RH)

    pairs = range(npair)
    lanes = [slice(q * LANE, (q + 1) * LANE) for q in pairs]
    for c in range(t // RCHUNK):
        rows = slice(c * RCHUNK, (c + 1) * RCHUNK)
        s_old = [s_ref[q] for q in pairs]
        o = [_dg(jnp.concatenate([atp_ref[rows, lanes[q]], rp_ref[rows, lanes[q]]], axis=0),
                 s_old[q].astype(BF16), NT) for q in pairs]
        ds = []
        for q in pairs:
            u = o[q][:RCHUNK] + u0_ref[rows, lanes[q]]
            y_ref[rows, lanes[q]] = o[q][RCHUNK:] + y0_ref[rows, lanes[q]]
            uv = jnp.concatenate([u.astype(BF16), vb_ref[rows, lanes[q]]], axis=0)
            bk = jnp.concatenate([bh_ref[rows, lanes[q]], kh_ref[rows, lanes[q]]], axis=0)
            ds.append(_dg(uv, bk, TN))
        for q in pairs:
            dl = dl_ref[c * SUB:c * SUB + 1, lanes[q]]
            s_ref[q] = s_old[q] * dl + jnp.where(bd, ds[q], 0.0)

    for q in range(npair):
        sl = slice(q * LANE, (q + 1) * LANE)
        y = y_ref[:, sl]
        yh, yl = _split2(y)
        mean = _dg(yh, hmean) + _dg(yl, hmean)
        d = y - mean
        var = _bdot(d * d, hmean)
        yn = d * lax.rsqrt(var + RWKV_GN_EPS) * pa_ref[5:6, sl] + pa_ref[6:7, sl]
        o_ref[:, sl] = ((yn + bonus_ref[:, sl]) * gs_ref[:, sl]).astype(BF16)


def _rwkv_scan(atp, rp, bh, kh, vb, u0, y0, bonus, gs, dl, pa):
    l = atp.shape[0]
    t = min(RST, l)
    row = pl.BlockSpec((t, RW), lambda i: (i, 0))
    return pl.pallas_call(
        _rwkv_scan_kernel,
        grid=(l // t,),
        in_specs=[row] * 9 + [pl.BlockSpec(((t // RCHUNK) * SUB, RW), lambda i: (i, 0)),
                              pl.BlockSpec((SUB, RW), lambda i: (0, 0))],
        out_specs=row,
        out_shape=jax.ShapeDtypeStruct((l, RW), BF16),
        scratch_shapes=[pltpu.VMEM((RW // LANE, LANE, LANE), F32), pltpu.VMEM((t, RW), F32)],
        compiler_params=_params(dimension_semantics=("arbitrary",)),
        name="rwkv_scan",
    )(atp, rp, bh, kh, vb, u0, y0, bonus, gs, dl, pa)


def _ssd_conv(main_ref, halo_ref, wb_ref, buf_ref, first):
    q = main_ref.shape[0]
    buf_ref[0:SUB, :] = jnp.where(first, 0.0, halo_ref[...])
    buf_ref[SUB:SUB + q, :] = main_ref[...]
    xb = buf_ref[...]
    acc = wb_ref[SCONV:SCONV + 1, :] + xb[SUB:, :] * wb_ref[SCONV - 1:SCONV, :]
    for s in range(1, SCONV):
        acc = acc + pltpu.roll(xb, s, 0)[SUB:, :] * wb_ref[SCONV - 1 - s:SCONV - s, :]
    return _silu(acc)


def _ssd_kernel(z_ref, x_ref, b_ref, c_ref, dt_ref, xh_ref, bhalo_ref, chalo_ref,
                cwx_ref, cwb_ref, cwc_ref, hp_ref, dx_ref, nw_ref,
                o_ref, prev_ref, xbuf_ref, bbuf_ref, cbuf_ref, ybuf_ref):
    first = pl.program_id(0) == 0
    q = z_ref.shape[0]
    groups = range(SGROUPS)

    @pl.when(first)
    def _():
        prev_ref[...] = jnp.zeros_like(prev_ref)

    xs = _ssd_conv(x_ref, xh_ref, cwx_ref, xbuf_ref, first)
    bmb = _ssd_conv(b_ref, bhalo_ref, cwb_ref, bbuf_ref, first).astype(BF16)
    cmb = _ssd_conv(c_ref, chalo_ref, cwc_ref, cbuf_ref, first).astype(BF16)

    dt = _softplus(dt_ref[...] + hp_ref[0:1, :])
    a_neg = -jnp.exp(hp_ref[1:2, :])
    acs = _seg_cumsum(dt * a_neg, q)
    acs_t = acs.T
    acs_last = acs[q - 1:q, :]

    expand = jnp.where(_iota((LANE, SW), 0) == _idiv(_iota((LANE, SW), 1), SHEAD) + S_DT_LANE0,
                       1.0, 0.0).astype(BF16)
    th, tl = _split2(jnp.concatenate([dt, jnp.exp(acs), jnp.exp(acs_last - acs)], axis=0))
    ex = _dg(jnp.concatenate([th, tl], axis=0), expand)
    ex = ex[:3 * q] + ex[3 * q:]
    dt_e, eacs_e, ds_e = ex[:q], ex[q:2 * q], ex[2 * q:]
    xc = xs * dt_e
    xcd = (xc * ds_e).astype(BF16)

    gl = [slice(g * SSTATE, (g + 1) * SSTATE) for g in groups]
    gc = [slice(g * SGW, (g + 1) * SGW) for g in groups]
    scores = [_dg(cmb[:, gl[g]], bmb[:, gl[g]], NT) for g in groups]
    prev = prev_ref[...]
    prevb = prev.astype(BF16)
    y_off = [_dg(cmb[:, gl[g]], prevb[:, gc[g]]) for g in groups]
    states = [_dg(bmb[:, gl[g]], xcd[:, gc[g]], TN) for g in groups]
    for g in groups:
        prev_ref[:, gc[g]] = prev[:, gc[g]] * eacs_e[q - 1:q, gc[g]] + states[g]

    causal = _iota((q, q), 1) <= _iota((q, q), 0)
    for h in range(SW // SHEAD):
        lane = S_DT_LANE0 + h
        hc = slice(h * SHEAD, (h + 1) * SHEAD)
        diff = acs[:, lane:lane + 1] - acs_t[lane:lane + 1, :]
        lm = jnp.exp(jnp.where(causal, diff, -jnp.inf))
        ybuf_ref[:, hc] = _bdot(scores[h // SHG] * lm, xc[:, hc])
    for g in groups:
        y = ybuf_ref[:, gc[g]] + y_off[g] * eacs_e[:, gc[g]] + dx_ref[0:1, gc[g]] * xs[:, gc[g]]
        y = y * _silu(z_ref[:, gc[g]])
        ms = jnp.mean(y * y, axis=-1, keepdims=True)
        o_ref[:, gc[g]] = (y * lax.rsqrt(ms + EPS) * nw_ref[0:1, gc[g]]).astype(BF16)


def _ssd(p, ps, cw8, hp, dx8, nw8):
    l = p.shape[0]
    q = SCHUNK
    hb = q // SUB
    gn = SGROUPS * SSTATE

    def halo_idx(i):
        return jnp.maximum(i * hb - 1, 0)

    zb, xb = T_Z // SW, T_X // SW
    bb, cb, db = T_B // gn, T_C // gn, S_DT // LANE
    in_specs = [
        pl.BlockSpec((q, SW), lambda i: (i, zb)),
        pl.BlockSpec((q, SW), lambda i: (i, xb)),
        pl.BlockSpec((q, gn), lambda i: (i, bb)),
        pl.BlockSpec((q, gn), lambda i: (i, cb)),
        pl.BlockSpec((q, LANE), lambda i: (i, db)),
        pl.BlockSpec((SUB, SW), lambda i: (halo_idx(i), xb)),
        pl.BlockSpec((SUB, gn), lambda i: (halo_idx(i), bb)),
        pl.BlockSpec((SUB, gn), lambda i: (halo_idx(i), cb)),
        pl.BlockSpec((SUB, SW), lambda i: (0, 0)),
        pl.BlockSpec((SUB, gn), lambda i: (0, SW // gn)),
        pl.BlockSpec((SUB, gn), lambda i: (0, SW // gn + 1)),
        pl.BlockSpec((SUB, LANE), lambda i: (0, 0)),
        pl.BlockSpec((SUB, SW), lambda i: (0, 0)),
        pl.BlockSpec((SUB, SW), lambda i: (0, 0)),
    ]
    return pl.pallas_call(
        _ssd_kernel,
        grid=(l // q,),
        in_specs=in_specs,
        out_specs=pl.BlockSpec((q, SW), lambda i: (i, 0)),
        out_shape=jax.ShapeDtypeStruct((l, SW), BF16),
        scratch_shapes=[pltpu.VMEM((SSTATE, SW), F32),
                        pltpu.VMEM((q + SUB, SW), F32),
                        pltpu.VMEM((q + SUB, gn), F32),
                        pltpu.VMEM((q + SUB, gn), F32),
                        pltpu.VMEM((q, SW), F32)],
        compiler_params=_params(dimension_semantics=("arbitrary",)),
        name="ssd",
    )(p, p, p, p, ps, p, p, p, cw8, cw8, cw8, hp, dx8, nw8)


GT = 256


def _gla_kernel(q_ref, k_ref, v_ref, g_ref, gd_ref, gkw_ref, gkb_ref, nw_ref, o_ref, st_ref, oi_ref):
    @pl.when(pl.program_id(0) == 0)
    def _():
        st_ref[...] = jnp.zeros_like(st_ref)

    t = q_ref.shape[0]
    nchunk = t // GCHUNK
    heads = range(GHEADS)
    kl = [slice(h * GHK, (h + 1) * GHK) for h in heads]
    vl = [slice(h * GHV, (h + 1) * GHV) for h in heads]
    gk = _log_sigmoid(_dot3(gd_ref[...], gkw_ref[...]) + gkb_ref[0:1, :]) * (1.0 / 16.0)
    bc = _seg_cumsum(gk, GCHUNK)
    mid = _chunk_row(bc, GCHUNK, GCHUNK // 2)
    last = _chunk_row(bc, GCHUNK, GCHUNK - 1)
    qs = q_ref[...] * (GHK ** -0.5)
    k = k_ref[...]
    vb = v_ref[...].astype(BF16)
    qm = (qs * jnp.exp(bc - mid)).astype(BF16)
    km = (k * jnp.exp(mid - bc)).astype(BF16)
    q_in = (qs * jnp.exp(bc)).astype(BF16)
    k_st = (k * jnp.exp(last - bc)).astype(BF16)
    dlast = jnp.exp(last)

    ri = _iota((t, t), 0)
    ci = _iota((t, t), 1)
    incl = (_idiv(ri, GCHUNK) == _idiv(ci, GCHUNK)) & (ci <= ri)
    attn = [jnp.where(incl, _dg(qm[:, kl[h]], km[:, kl[h]], NT), 0.0).astype(BF16) for h in heads]
    o_intra = [_dg(attn[h], vb[:, vl[h]]) for h in heads]

    for c in range(nchunk):
        rows = slice(c * GCHUNK, (c + 1) * GCHUNK)
        st = [st_ref[h] for h in heads]
        for h in heads:
            oi_ref[rows, vl[h]] = _dg(q_in[rows, kl[h]], st[h].astype(BF16), NT)
        for h in heads:
            st_ref[h] = (st[h] * dlast[c * GCHUNK:c * GCHUNK + 1, kl[h]]
                         + _dg(vb[rows, vl[h]], k_st[rows, kl[h]], TN))
    for h in heads:
        o = o_intra[h] + oi_ref[:, vl[h]]
        ms = jnp.mean(o * o, axis=-1, keepdims=True)
        o = o * lax.rsqrt(ms + EPS) * nw_ref[0:1, :]
        o_ref[:, vl[h]] = (o * _silu(g_ref[:, vl[h]])).astype(BF16)


def _gla(p, pgd, gkwp, gkb8, nw8):
    l = p.shape[0]
    t = min(GT, l)
    return pl.pallas_call(
        _gla_kernel,
        grid=(l // t,),
        in_specs=[pl.BlockSpec((t, GKEY), lambda i: (i, O_Q // GKEY)),
                  pl.BlockSpec((t, GKEY), lambda i: (i, O_K // GKEY)),
                  pl.BlockSpec((t, GVAL), lambda i: (i, O_V // GVAL)),
                  pl.BlockSpec((t, GVAL), lambda i: (i, O_G // GVAL)),
                  pl.BlockSpec((t, LANE), lambda i: (i, 0)),
                  pl.BlockSpec((LANE, GKEY), lambda i: (0, 0)),
                  pl.BlockSpec((SUB, GKEY), lambda i: (0, 0)),
                  pl.BlockSpec((SUB, GHV), lambda i: (0, 0))],
        out_specs=pl.BlockSpec((t, GVAL), lambda i: (i, 0)),
        out_shape=jax.ShapeDtypeStruct((l, GVAL), BF16),
        scratch_shapes=[pltpu.VMEM((GHEADS, GHV, GHK), F32), pltpu.VMEM((t, GVAL), F32)],
        compiler_params=_params(dimension_semantics=("arbitrary",)),
        name="gla",
    )(p, p, p, p, pgd, gkwp, gkb8, nw8)


def _rows8(*rows):
    n = rows[0].shape[0]
    parts = [r.astype(F32)[None, :] for r in rows]
    if len(rows) < SUB:
        parts.append(jnp.zeros((SUB - len(rows), n), F32))
    return jnp.concatenate(parts, axis=0)


def _pad_cols(w, n):
    return jnp.concatenate([w, jnp.zeros((w.shape[0], n - w.shape[1]), w.dtype)], axis=1)


def _pad_rows(w, n):
    return jnp.concatenate([w, jnp.zeros((n - w.shape[0], w.shape[1]), w.dtype)], axis=0)


def _small_col0(j):
    return jnp.where(j == 0, E_WD, S_DT_COL0)


def _lora_rows(w, row0):
    return jnp.concatenate([jnp.zeros((row0, w.shape[1]), w.dtype), w,
                            jnp.zeros((S_BLK - row0 - w.shape[0], w.shape[1]), w.dtype)], axis=0)


def kernel(x, norm_w, final_norm_w, w_in_even, w_out_even, rwkv_mu, rwkv_w0, rwkv_w2, rwkv_a0, rwkv_a2, rwkv_k_k, rwkv_k_a, rwkv_r_k, rwkv_ln_w, rwkv_ln_b, rwkv_v0, rwkv_v1, rwkv_v2, ssm_conv_w, ssm_conv_b, ssm_dt_bias, ssm_A_log, ssm_D, ssm_norm_w, w_in_odd, w_out_odd, gla_gk_w, gla_gk_b, gla_norm_w):
    bsz, l, d = x.shape
    depth = norm_w.shape[0]
    w_out_even_bf = w_out_even.astype(BF16)
    w_out_odd_bf = w_out_odd.astype(BF16)
    wt_even = jnp.swapaxes(w_in_even, 1, 2)
    wt_odd = jnp.swapaxes(w_in_odd, 1, 2)
    outs = []
    for b in range(bsz):
        res = x[b]
        h = _rmsnorm(res, _rows8(norm_w[0]))
        p_first = None
        for layer in range(depth):
            i = layer // 2
            last = layer == depth - 1
            nw_next = _rows8(final_norm_w if last else norm_w[layer + 1])
            if layer % 2 == 0:
                mu = rwkv_mu[i]
                mu_small = jnp.concatenate([mu[E_WD:E_ZXBC], jnp.zeros((S_COLS - (E_ZXBC - E_WD),), F32)])
                p = _inproj(h, wt_even, i, lambda j: j * PTN, E_MAIN // PTN, PTN, mu8=_rows8(mu[:E_MAIN]))
                pt = _inproj(h, wt_even, i, lambda j: E_ZXBC + j * PTN, T_COLS // PTN, PTN)
                ps = _inproj(h, wt_even, i, _small_col0, S_COLS // S_BLK, S_BLK, mu8=_rows8(mu_small))
                v0 = rwkv_v0[i - 1] if i > 0 else jnp.zeros((RW,), F32)
                pa = _rows8(rwkv_w0[i], rwkv_a0[i], rwkv_k_k[i], rwkv_k_a[i],
                            rwkv_r_k[i].reshape(RW), rwkv_ln_w[i], rwkv_ln_b[i], v0)
                w2p = _lora_rows(rwkv_w2[i], 0)
                a2p = _lora_rows(rwkv_a2[i], E_AD - E_WD)
                if i == 0:
                    vmix = None
                    p_first = p
                else:
                    tl = _vlora(p, _pad_cols(rwkv_v1[i - 1], LANE))
                    vmix = (tl, _pad_rows(rwkv_v2[i - 1], LANE), p_first)
                (atp, rp, bh, kh, vb, u0, y0, bonus, gs, dl) = _rwkv_prep(p, ps, pa, w2p, a2p, vmix)
                y_a = _rwkv_scan(atp, rp, bh, kh, vb, u0, y0, bonus, gs, dl, pa)

                cw8 = jnp.concatenate([ssm_conv_w[i], ssm_conv_b[i][None, :],
                                       jnp.zeros((SUB - SCONV - 1, ssm_conv_w.shape[2]), F32)], axis=0)
                lead = jnp.zeros((S_DT_LANE0,), F32)
                hp = _rows8(jnp.concatenate([lead, ssm_dt_bias[i]]), jnp.concatenate([lead, ssm_A_log[i]]))
                dx8 = _rows8(jnp.repeat(ssm_D[i], SHEAD))
                y_b = _ssd(pt, ps, cw8, hp, dx8, _rows8(ssm_norm_w[i]))
                o = _outproj([y_a, y_b], w_out_even_bf, i, res, nw_next, last)
            else:
                p = _inproj(h, wt_odd, i, lambda j: j * PTN, O_MAIN // PTN, PTN)
                pgd = _inproj(h, wt_odd, i, lambda j: O_GD_COL0, 1, LANE)
                gkwp = jnp.concatenate([jnp.zeros((O_GD_LANE0, GKEY), F32), gla_gk_w[i]], axis=0)
                y = _gla(p, pgd, gkwp, _rows8(gla_gk_b[i]), _rows8(gla_norm_w[i]))
                o = _outproj([y], w_out_odd_bf, i, res, nw_next, last)
            if last:
                res = o[0]
            else:
                res, h = o
        outs.append(res)
    return jnp.stack(outs).astype(x.dtype)
```

```python
import functools
import math

import jax
import jax.numpy as jnp
from jax import lax
from jax.experimental import pallas as pl
from jax.experimental.pallas import tpu as pltpu

F32 = jnp.float32
BF16 = jnp.bfloat16

D_MODEL = 2048
EPS = 1e-5
RW = 2048
RH = 64
DECAY_LORA = 96
AAA_LORA = 96
MV_LORA = 64
RWKV_GN_EPS = 64e-5
RCHUNK = 64
SW = 2048
SHEAD = 64
SGROUPS = 4
SSTATE = 128
SCONV = 4
SCHUNK = 128
SGW = SW // SGROUPS
SHG = SGW // SHEAD
GHEADS = 4
GKEY = 1024
GVAL = 2048
GHK = 256
GHV = 512
GRANK = 16
GCHUNK = 64

LANE = 128
SUB = 8
VMEM_LIMIT = 56 * 1024 * 1024

PTN = 512
E_R, E_K, E_V, E_G = 0, 2048, 4096, 6144
E_MAIN = 8192
E_WD = 8192
E_AD = 8288
E_ZXBC = 8384
E_COLS = 13536
ZXBC_COLS = 5120
P_Z, P_X = E_MAIN, E_MAIN + 2048
P_B, P_C = E_MAIN + 4096, E_MAIN + 4608
P_LORA = E_MAIN + ZXBC_COLS
P_COLS = P_LORA + 2 * PTN
P_DT = P_COLS - LANE
S_BLK = 256
S_DT_LANE0 = LANE - SW // SHEAD
O_COLS = 6160
O_Q, O_K, O_V, O_G = 0, 1024, 2048, 4096
O_MAIN = 6144
OP_COLS = O_MAIN + PTN
OP_GD = OP_COLS - LANE
O_GD_LANE0 = LANE - GRANK

NN = (((1,), (0,)), ((), ()))
NT = (((1,), (1,)), ((), ()))
TN = (((0,), (0,)), ((), ()))


def _dg(a, b, dims=NN):
    return lax.dot_general(a, b, dims, preferred_element_type=F32)


def _bdot(a, b, dims=NN):
    return _dg(a.astype(BF16), b.astype(BF16), dims)


def _split2(x):
    hi = x.astype(BF16)
    lo = (x - hi.astype(F32)).astype(BF16)
    return hi, lo


def _split3(x):
    hi = x.astype(BF16)
    r = x - hi.astype(F32)
    mid = r.astype(BF16)
    lo = (r - mid.astype(F32)).astype(BF16)
    return hi, mid, lo


def _dot3(a, b, dims=NN):
    ah, al = _split2(a)
    bh, bl = _split2(b)
    return _dg(ah, bh, dims) + (_dg(ah, bl, dims) + _dg(al, bh, dims))


def _dot_sel(a, sel_bf16, dims=NN):
    ah, am, al = _split3(a)
    return _dg(ah, sel_bf16, dims) + (_dg(am, sel_bf16, dims) + _dg(al, sel_bf16, dims))


EXP_NEG_HALF = math.exp(-0.5)


def _sigmoid(x):
    return 0.5 * jnp.tanh(0.5 * x) + 0.5


def _silu(x):
    return x * _sigmoid(x)


def _log_sigmoid(x):
    return jnp.minimum(x, 0.0) - jnp.log(1.0 + jnp.exp(-jnp.abs(x)))


def _softplus(x):
    return jnp.maximum(x, 0.0) + jnp.log(1.0 + jnp.exp(-jnp.abs(x)))


def _iota(shape, axis):
    return lax.broadcasted_iota(jnp.int32, shape, axis)


def _idiv(x, n):
    return lax.shift_right_logical(x, n.bit_length() - 1)


def _seg_cumsum(x, chunk):
    rin = _iota(x.shape, 0) & (chunk - 1)
    s = 1
    while s < chunk:
        x = x + jnp.where(rin >= s, pltpu.roll(x, s, 0), 0.0)
        s *= 2
    return x


def _chunk_row(x, chunk, r):
    t, w = x.shape
    x3 = x.reshape(t // chunk, chunk, w)
    return jnp.broadcast_to(x3[:, r:r + 1, :], x3.shape).reshape(t, w)


def _head_ones(n, head, scale=1.0):
    r = _idiv(_iota((n, n), 0), head)
    c = _idiv(_iota((n, n), 1), head)
    return jnp.where(r == c, scale, 0.0).astype(BF16)


def _params(**kw):
    return pltpu.CompilerParams(vmem_limit_bytes=VMEM_LIMIT, **kw)


def _rmsnorm_kernel(x_ref, w_ref, o_ref):
    x = x_ref[...]
    ms = jnp.mean(x * x, axis=-1, keepdims=True)
    o_ref[...] = (x * lax.rsqrt(ms + EPS) * w_ref[0:1, :]).astype(BF16)


def _rmsnorm(x, w8, tm=512):
    l, d = x.shape
    return pl.pallas_call(
        _rmsnorm_kernel,
        grid=(l // tm,),
        in_specs=[pl.BlockSpec((tm, d), lambda i: (i, 0)),
                  pl.BlockSpec((SUB, d), lambda i: (0, 0))],
        out_specs=pl.BlockSpec((tm, d), lambda i: (i, 0)),
        out_shape=jax.ShapeDtypeStruct((l, d), BF16),
        compiler_params=_params(dimension_semantics=("parallel",)),
        name="rmsnorm",
    )(x, w8)


HALO = 16


def _inproj_kernel(*refs, shift, shifted_block):
    if shift:
        h_ref, hh_ref, w_ref, mu_ref, o_ref = refs
    else:
        h_ref, w_ref, o_ref = refs
    w = w_ref[0].astype(BF16)
    p = _dg(h_ref[...], w, NT)
    if not shift:
        o_ref[...] = p
        return
    do_shift = shifted_block(pl.program_id(1))

    @pl.when(do_shift)
    def _():
        prev = _dg(hh_ref[...], w, NT)[HALO - 1:HALO, :]
        prev = jnp.where(pl.program_id(0) == 0, 0.0, prev)
        pprev = jnp.where(_iota(p.shape, 0) == 0, prev, pltpu.roll(p, 1, 0))
        o_ref[...] = p + (pprev - p) * mu_ref[0:1, :]

    @pl.when(jnp.logical_not(do_shift))
    def _():
        o_ref[...] = p


def _inproj(h, wt, layer, col0, nblk, tn, mu8=None, shifted_block=None, tm=2048):
    l, d = h.shape
    tm = min(tm, l)
    shift = mu8 is not None
    in_specs = [pl.BlockSpec((tm, d), lambda i, j: (i, 0))]
    args = [h]
    if shift:
        hb = tm // HALO
        in_specs.append(pl.BlockSpec((HALO, d), lambda i, j: (jnp.maximum(i * hb - 1, 0), 0)))
        args.append(h)
    def w_map(i, j):
        c = col0(j)
        return (layer, c if isinstance(c, int) else pl.multiple_of(c, SUB), 0)

    in_specs.append(pl.BlockSpec((pl.Element(1), pl.Element(tn), pl.Element(d)), w_map))
    args.append(wt)
    if shift:
        in_specs.append(pl.BlockSpec((SUB, tn), lambda i, j: (0, j)))
        args.append(mu8)
    return pl.pallas_call(
        functools.partial(_inproj_kernel, shift=shift, shifted_block=shifted_block),
        grid=(l // tm, nblk),
        in_specs=in_specs,
        out_specs=pl.BlockSpec((tm, tn), lambda i, j: (i, j)),
        out_shape=jax.ShapeDtypeStruct((l, nblk * tn), F32),
        compiler_params=_params(dimension_semantics=("parallel", "parallel")),
        name="inproj",
    )(*args)


def _outproj_kernel(*refs, n_in, final):
    y_refs = refs[:n_in]
    w_ref, res_ref, nw_ref = refs[n_in:n_in + 3]
    outs = refs[n_in + 3:]
    r = res_ref[...]
    for s in range(n_in):
        kw = y_refs[s].shape[1]
        r = r + jnp.dot(y_refs[s][...], w_ref[s * kw:(s + 1) * kw, :], preferred_element_type=F32)
    ms = jnp.mean(r * r, axis=-1, keepdims=True)
    hn = r * lax.rsqrt(ms + EPS) * nw_ref[0:1, :]
    if final:
        outs[0][...] = hn
    else:
        outs[0][...] = r
        outs[1][...] = hn.astype(BF16)


def _outproj(ys, w, layer, res, nw8, final, tm=512):
    l, d = res.shape
    n_in = len(ys)
    kdim = w.shape[1]
    tm = min(tm, l)
    in_specs = [pl.BlockSpec((tm, y.shape[1]), lambda i: (i, 0)) for y in ys]
    in_specs += [pl.BlockSpec((None, kdim, d), lambda i: (layer, 0, 0), pipeline_mode=pl.Buffered(1)),
                 pl.BlockSpec((tm, d), lambda i: (i, 0)),
                 pl.BlockSpec((SUB, d), lambda i: (0, 0))]
    row_spec = pl.BlockSpec((tm, d), lambda i: (i, 0))
    if final:
        out_specs = [row_spec]
        out_shape = [jax.ShapeDtypeStruct((l, d), F32)]
    else:
        out_specs = [row_spec, row_spec]
        out_shape = [jax.ShapeDtypeStruct((l, d), F32), jax.ShapeDtypeStruct((l, d), BF16)]
    return pl.pallas_call(
        functools.partial(_outproj_kernel, n_in=n_in, final=final),
        grid=(l // tm,),
        in_specs=in_specs,
        out_specs=out_specs,
        out_shape=out_shape,
        compiler_params=_params(dimension_semantics=("parallel",)),
        name="outproj",
    )(*ys, w, res, nw8)


def _vlora_kernel(v_ref, v1_ref, o_ref):
    o_ref[...] = _dot3(v_ref[...], v1_ref[...])


def _vlora(p, v1p, t=512):
    l = p.shape[0]
    t = min(t, l)
    cb = E_V // RW
    return pl.pallas_call(
        _vlora_kernel,
        grid=(l // t,),
        in_specs=[pl.BlockSpec((t, RW), lambda i: (i, cb)),
                  pl.BlockSpec((RW, LANE), lambda i: (0, 0))],
        out_specs=pl.BlockSpec((t, LANE), lambda i: (i, 0)),
        out_shape=jax.ShapeDtypeStruct((l, LANE), F32),
        compiler_params=_params(dimension_semantics=("parallel",)),
        name="rwkv_vlora",
    )(p, v1p)


RT = 1024
RHALF = 1024
REW = 1024
RST = 256


def _rwkv_prep_kernel(*refs, has_vmix):
    (r_ref, k_ref, v_ref, g_ref, lo_ref, pa_ref, w2_ref, a2_ref) = refs[:8]
    pos = 8
    if has_vmix:
        tl_ref, v2_ref, vf_ref = refs[pos:pos + 3]
        pos += 3
    (atp_ref, rp_ref, bh_ref, kh2_ref, vb_ref, u0_ref, y0_ref, bonus_ref, gs_ref, dl_ref) = refs[pos:]
    t = r_ref.shape[0]
    half = min(RHALF, t)
    blk = min(REW, t)
    cpb = blk // RCHUNK

    w0, a0, k_k, k_a = (pa_ref[i:i + 1, :] for i in range(4))
    r_k, v0 = pa_ref[4:5, :], pa_ref[7:8, :]
    hm = _head_ones(LANE, RH)
    first_row = (_iota((blk, LANE), 0) & (RCHUNK - 1)) == 0

    def front(r0):
        rs = slice(r0, r0 + blk)
        r, k, v, g, lo = r_ref[rs, :], k_ref[rs, :], v_ref[rs, :], g_ref[rs, :], lo_ref[rs, :]
        logd = -EXP_NEG_HALF * _sigmoid(w0 + _dot3(jnp.tanh(lo), w2_ref[...]))
        a = _sigmoid(a0 + _dot3(lo, a2_ref[...]))
        if has_vmix:
            v = v + (vf_ref[rs, :] - v) * _sigmoid(v0 + _dot3(tl_ref[rs, :], v2_ref[...]))

        kk = k * k_k
        k2h, k2l = _split2(kk * kk)
        ss = _dg(k2h, hm) + _dg(k2l, hm)
        kk = kk * lax.rsqrt(jnp.maximum(ss, 1e-24))
        kh = k * (1.0 + (a - 1.0) * k_a)
        av = -kk
        bv = kk * a

        rkh, rkl = _split2(r * kh * r_k)
        bonus_ref[rs, :] = (_dg(rkh, hm) + _dg(rkl, hm)) * v
        gs_ref[rs, :] = _silu(g)

        cum = _seg_cumsum(logd, RCHUNK)
        e_c = jnp.exp(cum)
        e_n = jnp.exp(-cum)
        e_p = jnp.where(first_row, 1.0, pltpu.roll(e_c, 1, 0))
        p_end = jnp.exp(cum.reshape(cpb, RCHUNK, LANE)[:, RCHUNK - 1:RCHUNK, :])
        e_l = e_n * jnp.broadcast_to(p_end, (cpb, RCHUNK, LANE)).reshape(blk, LANE)
        at = av * e_p
        rt = r * e_c
        bt = bv * e_n
        kt = kh * e_n
        bh_ref[rs, :] = (bv * e_l).astype(BF16)
        kh2_ref[rs, :] = (kh * e_l).astype(BF16)
        vb_ref[rs, :] = v.astype(BF16)
        c0 = r0 // RCHUNK
        dl_ref[c0 * SUB:(c0 + cpb) * SUB, :] = (
            jnp.broadcast_to(p_end, (cpb, SUB, LANE)).reshape(cpb * SUB, LANE))
        out = []
        for c in range(cpb):
            cr = slice(c * RCHUNK, (c + 1) * RCHUNK)
            out.append((r0 + c * RCHUNK, at[cr], rt[cr], bt[cr], kt[cr], v[cr]))
        return out

    lane = _iota((RCHUNK, LANE), 1)
    si = lane & (RH - 1)
    ti = _iota((RCHUNK, LANE), 0)
    strict = si < ti
    incl = si <= ti
    eye = jnp.where(si == ti, 1.0, 0.0)
    head0 = lane < RH

    def stack(xc):
        xc = xc.astype(BF16)
        zero = jnp.zeros_like(xc)
        return jnp.concatenate([jnp.where(head0, xc, zero), jnp.where(head0, zero, xc)], axis=0)

    def chains(ops):
        chunks = range(len(ops))
        ats = [stack(o[1]) for o in ops]
        vs = [stack(o[5]) for o in ops]
        amat = []
        for (_, at, rt, bt, kt, _) in ops:
            lhs = jnp.concatenate([at, rt], axis=0).astype(BF16)
            rhs = jnp.concatenate([stack(bt), stack(kt)], axis=0)
            amat.append(_dg(lhs, rhs, NT))
        yield
        a_ab = [jnp.where(strict, m[:RCHUNK, :LANE], 0.0) for m in amat]
        a_ak = [jnp.where(strict, m[:RCHUNK, LANE:], 0.0) for m in amat]
        a_rb = [jnp.where(incl, m[RCHUNK:, :LANE], 0.0).astype(BF16) for m in amat]
        a_rk = [jnp.where(incl, m[RCHUNK:, LANE:], 0.0) for m in amat]
        akv2 = [_dg(jnp.concatenate([a_ak[c], a_rk[c]], axis=0).astype(BF16), vs[c]) for c in chunks]
        akv = [m[:RCHUNK] for m in akv2]
        arkv = [m[RCHUNK:] for m in akv2]
        yield
        inv = [eye + m for m in a_ab]
        pw = [_dg(m.astype(BF16), stack(m)) for m in a_ab]
        yield
        s = 2
        while s < RCHUNK // 2:
            both = [_dg(pw[c].astype(BF16), jnp.concatenate([stack(pw[c]), stack(inv[c])], axis=1))
                    for c in chunks]
            pw = [m[:, :LANE] for m in both]
            inv = [inv[c] + both[c][:, LANE:] for c in chunks]
            s *= 2
            yield
        inv = [inv[c] + _dg(pw[c].astype(BF16), stack(inv[c])) for c in chunks]
        yield
        x = [_dg(inv[c].astype(BF16), jnp.concatenate([ats[c], stack(akv[c])], axis=1)) for c in chunks]
        yield
        z = [_dg(a_rb[c], jnp.concatenate([stack(x[c][:, :LANE]), stack(x[c][:, LANE:])], axis=1))
             for c in chunks]
        for c in chunks:
            orow = slice(ops[c][0], ops[c][0] + RCHUNK)
            atp_ref[orow, :] = x[c][:, :LANE].astype(BF16)
            u0_ref[orow, :] = x[c][:, LANE:]
            rp_ref[orow, :] = (ops[c][2] + z[c][:, :LANE]).astype(BF16)
            y0_ref[orow, :] = z[c][:, LANE:] + arkv[c]

    halves = [list(range(h0, h0 + half, blk)) for h0 in range(0, t, half)]
    ops = [o for r0 in halves[0] for o in front(r0)]
    for hi in range(len(halves)):
        pending = list(halves[hi + 1]) if hi + 1 < len(halves) else []
        nxt = []
        for _ in chains(ops):
            if pending:
                nxt += front(pending.pop(0))
        while pending:
            nxt += front(pending.pop(0))
        ops = nxt


def _rwkv_prep(p, pa, w2p, a2p, vmix):
    l = p.shape[0]
    t = min(RT, l)
    npair = RW // LANE
    has_vmix = vmix is not None

    def main(off):
        cb = off // LANE
        return pl.BlockSpec((t, LANE), lambda q, i: (i, cb + q))

    par = pl.BlockSpec((SUB, LANE), lambda q, i: (0, q))
    lora = pl.BlockSpec((LANE, LANE), lambda q, i: (0, q))
    lora_in = pl.BlockSpec((S_BLK, LANE), lambda q, i: (0, q))
    row = pl.BlockSpec((t, LANE), lambda q, i: (i, q))

    in_specs = [main(E_R), main(E_K), main(E_V), main(E_G),
                pl.BlockSpec((t, S_BLK), lambda q, i: (i, P_LORA // S_BLK)), par, lora_in, lora_in]
    args = [p] * 5 + [pa, w2p, a2p]
    if has_vmix:
        tl, v2p, p_first = vmix
        in_specs += [pl.BlockSpec((t, LANE), lambda q, i: (i, 0)), lora, main(E_V)]
        args += [tl, v2p, p_first]
    dl_rows = (t // RCHUNK) * SUB
    out_specs = [row] * 9 + [pl.BlockSpec((dl_rows, LANE), lambda q, i: (i, q))]
    sd = jax.ShapeDtypeStruct
    out_shape = ([sd((l, RW), BF16)] * 5 + [sd((l, RW), F32)] * 4
                 + [sd((l // RCHUNK * SUB, RW), F32)])
    return pl.pallas_call(
        functools.partial(_rwkv_prep_kernel, has_vmix=has_vmix),
        grid=(npair, l // t),
        in_specs=in_specs,
        out_specs=out_specs,
        out_shape=out_shape,
        compiler_params=_params(dimension_semantics=("parallel", "parallel")),
        name="rwkv_prep",
    )(*args)


def _rwkv_scan_kernel(atp_ref, rp_ref, bh_ref, kh_ref, vb_ref, u0_ref, y0_ref,
                      bonus_ref, gs_ref, dl_ref, pa_ref, o_ref, s_ref, y_ref):
    @pl.when(pl.program_id(0) == 0)
    def _():
        s_ref[...] = jnp.zeros_like(s_ref)

    t = atp_ref.shape[0]
    npair = RW // LANE
    bd = _idiv(_iota((LANE, LANE), 0), RH) == _idiv(_iota((LANE, LANE), 1), RH)
    hmean = _head_ones(LANE, RH, 1.0 / RH)

    pairs = range(npair)
    lanes = [slice(q * LANE, (q + 1) * LANE) for q in pairs]
    for c in range(t // RCHUNK):
        rows = slice(c * RCHUNK, (c + 1) * RCHUNK)
        s_old = [s_ref[q] for q in pairs]
        o = [_dg(jnp.concatenate([atp_ref[rows, lanes[q]], rp_ref[rows, lanes[q]]], axis=0),
                 s_old[q].astype(BF16), NT) for q in pairs]
        ds = []
        for q in pairs:
            u = o[q][:RCHUNK] + u0_ref[rows, lanes[q]]
            y_ref[rows, lanes[q]] = o[q][RCHUNK:] + y0_ref[rows, lanes[q]]
            uv = jnp.concatenate([u.astype(BF16), vb_ref[rows, lanes[q]]], axis=0)
            bk = jnp.concatenate([bh_ref[rows, lanes[q]], kh_ref[rows, lanes[q]]], axis=0)
            ds.append(_dg(uv, bk, TN))
        for q in pairs:
            dl = dl_ref[c * SUB:c * SUB + 1, lanes[q]]
            s_ref[q] = s_old[q] * dl + jnp.where(bd, ds[q], 0.0)

    for q in range(npair):
        sl = slice(q * LANE, (q + 1) * LANE)
        y = y_ref[:, sl]
        yh, yl = _split2(y)
        mean = _dg(yh, hmean) + _dg(yl, hmean)
        d = y - mean
        var = _bdot(d * d, hmean)
        yn = d * lax.rsqrt(var + RWKV_GN_EPS) * pa_ref[5:6, sl] + pa_ref[6:7, sl]
        o_ref[:, sl] = ((yn + bonus_ref[:, sl]) * gs_ref[:, sl]).astype(BF16)


def _rwkv_scan(atp, rp, bh, kh, vb, u0, y0, bonus, gs, dl, pa):
    l = atp.shape[0]
    t = min(RST, l)
    row = pl.BlockSpec((t, RW), lambda i: (i, 0))
    return pl.pallas_call(
        _rwkv_scan_kernel,
        grid=(l // t,),
        in_specs=[row] * 9 + [pl.BlockSpec(((t // RCHUNK) * SUB, RW), lambda i: (i, 0)),
                              pl.BlockSpec((SUB, RW), lambda i: (0, 0))],
        out_specs=row,
        out_shape=jax.ShapeDtypeStruct((l, RW), BF16),
        scratch_shapes=[pltpu.VMEM((RW // LANE, LANE, LANE), F32), pltpu.VMEM((t, RW), F32)],
        compiler_params=_params(dimension_semantics=("arbitrary",)),
        name="rwkv_scan",
    )(atp, rp, bh, kh, vb, u0, y0, bonus, gs, dl, pa)


def _ssd_conv(main_ref, halo_ref, wb_ref, buf_ref, first):
    q = main_ref.shape[0]
    buf_ref[0:SUB, :] = jnp.where(first, 0.0, halo_ref[...])
    buf_ref[SUB:SUB + q, :] = main_ref[...]
    xb = buf_ref[...]
    acc = wb_ref[SCONV:SCONV + 1, :] + xb[SUB:, :] * wb_ref[SCONV - 1:SCONV, :]
    for s in range(1, SCONV):
        acc = acc + pltpu.roll(xb, s, 0)[SUB:, :] * wb_ref[SCONV - 1 - s:SCONV - s, :]
    return _silu(acc)


def _ssd_kernel(z_ref, x_ref, b_ref, c_ref, dt_ref, xh_ref, bhalo_ref, chalo_ref,
                cwx_ref, cwb_ref, cwc_ref, hp_ref, dx_ref, nw_ref,
                o_ref, prev_ref, xbuf_ref, bbuf_ref, cbuf_ref, ybuf_ref):
    first = pl.program_id(0) == 0
    q = z_ref.shape[0]
    groups = range(SGROUPS)

    @pl.when(first)
    def _():
        prev_ref[...] = jnp.zeros_like(prev_ref)

    xs = _ssd_conv(x_ref, xh_ref, cwx_ref, xbuf_ref, first)
    bmb = _ssd_conv(b_ref, bhalo_ref, cwb_ref, bbuf_ref, first).astype(BF16)
    cmb = _ssd_conv(c_ref, chalo_ref, cwc_ref, cbuf_ref, first).astype(BF16)

    dt = _softplus(dt_ref[...] + hp_ref[0:1, :])
    a_neg = -jnp.exp(hp_ref[1:2, :])
    acs = _seg_cumsum(dt * a_neg, q)
    acs_t = acs.T
    acs_last = acs[q - 1:q, :]

    expand = jnp.where(_iota((LANE, SW), 0) == _idiv(_iota((LANE, SW), 1), SHEAD) + S_DT_LANE0,
                       1.0, 0.0).astype(BF16)
    th, tl = _split2(jnp.concatenate([dt, jnp.exp(acs), jnp.exp(acs_last - acs)], axis=0))
    ex = _dg(jnp.concatenate([th, tl], axis=0), expand)
    ex = ex[:3 * q] + ex[3 * q:]
    dt_e, eacs_e, ds_e = ex[:q], ex[q:2 * q], ex[2 * q:]
    xc = xs * dt_e
    xcd = (xc * ds_e).astype(BF16)

    gl = [slice(g * SSTATE, (g + 1) * SSTATE) for g in groups]
    gc = [slice(g * SGW, (g + 1) * SGW) for g in groups]
    scores = [_dg(cmb[:, gl[g]], bmb[:, gl[g]], NT) for g in groups]
    prev = prev_ref[...]
    prevb = prev.astype(BF16)
    y_off = [_dg(cmb[:, gl[g]], prevb[:, gc[g]]) for g in groups]
    states = [_dg(bmb[:, gl[g]], xcd[:, gc[g]], TN) for g in groups]
    for g in groups:
        prev_ref[:, gc[g]] = prev[:, gc[g]] * eacs_e[q - 1:q, gc[g]] + states[g]

    causal = _iota((q, q), 1) <= _iota((q, q), 0)
    for h in range(SW // SHEAD):
        lane = S_DT_LANE0 + h
        hc = slice(h * SHEAD, (h + 1) * SHEAD)
        diff = acs[:, lane:lane + 1] - acs_t[lane:lane + 1, :]
        lm = jnp.exp(jnp.where(causal, diff, -jnp.inf))
        ybuf_ref[:, hc] = _bdot(scores[h // SHG] * lm, xc[:, hc])
    for g in groups:
        y = ybuf_ref[:, gc[g]] + y_off[g] * eacs_e[:, gc[g]] + dx_ref[0:1, gc[g]] * xs[:, gc[g]]
        y = y * _silu(z_ref[:, gc[g]])
        ms = jnp.mean(y * y, axis=-1, keepdims=True)
        o_ref[:, gc[g]] = (y * lax.rsqrt(ms + EPS) * nw_ref[0:1, gc[g]]).astype(BF16)


def _ssd(p, cw8, hp, dx8, nw8):
    l = p.shape[0]
    q = SCHUNK
    hb = q // SUB
    gn = SGROUPS * SSTATE

    def halo_idx(i):
        return jnp.maximum(i * hb - 1, 0)

    zb, xb = P_Z // SW, P_X // SW
    bb, cb, db = P_B // gn, P_C // gn, P_DT // LANE
    in_specs = [
        pl.BlockSpec((q, SW), lambda i: (i, zb)),
        pl.BlockSpec((q, SW), lambda i: (i, xb)),
        pl.BlockSpec((q, gn), lambda i: (i, bb)),
        pl.BlockSpec((q, gn), lambda i: (i, cb)),
        pl.BlockSpec((q, LANE), lambda i: (i, db)),
        pl.BlockSpec((SUB, SW), lambda i: (halo_idx(i), xb)),
        pl.BlockSpec((SUB, gn), lambda i: (halo_idx(i), bb)),
        pl.BlockSpec((SUB, gn), lambda i: (halo_idx(i), cb)),
        pl.BlockSpec((SUB, SW), lambda i: (0, 0)),
        pl.BlockSpec((SUB, gn), lambda i: (0, SW // gn)),
        pl.BlockSpec((SUB, gn), lambda i: (0, SW // gn + 1)),
        pl.BlockSpec((SUB, LANE), lambda i: (0, 0)),
        pl.BlockSpec((SUB, SW), lambda i: (0, 0)),
        pl.BlockSpec((SUB, SW), lambda i: (0, 0)),
    ]
    return pl.pallas_call(
        _ssd_kernel,
        grid=(l // q,),
        in_specs=in_specs,
        out_specs=pl.BlockSpec((q, SW), lambda i: (i, 0)),
        out_shape=jax.ShapeDtypeStruct((l, SW), BF16),
        scratch_shapes=[pltpu.VMEM((SSTATE, SW), F32),
                        pltpu.VMEM((q + SUB, SW), F32),
                        pltpu.VMEM((q + SUB, gn), F32),
                        pltpu.VMEM((q + SUB, gn), F32),
                        pltpu.VMEM((q, SW), F32)],
        compiler_params=_params(dimension_semantics=("arbitrary",)),
        name="ssd",
    )(p, p, p, p, p, p, p, p, cw8, cw8, cw8, hp, dx8, nw8)


GT = 256


def _gla_kernel(q_ref, k_ref, v_ref, g_ref, gd_ref, gkw_ref, gkb_ref, nw_ref, o_ref, st_ref, oi_ref):
    @pl.when(pl.program_id(0) == 0)
    def _():
        st_ref[...] = jnp.zeros_like(st_ref)

    t = q_ref.shape[0]
    nchunk = t // GCHUNK
    heads = range(GHEADS)
    kl = [slice(h * GHK, (h + 1) * GHK) for h in heads]
    vl = [slice(h * GHV, (h + 1) * GHV) for h in heads]
    gk = _log_sigmoid(_dot3(gd_ref[...], gkw_ref[...]) + gkb_ref[0:1, :]) * (1.0 / 16.0)
    bc = _seg_cumsum(gk, GCHUNK)
    mid = _chunk_row(bc, GCHUNK, GCHUNK // 2)
    last = _chunk_row(bc, GCHUNK, GCHUNK - 1)
    qs = q_ref[...] * (GHK ** -0.5)
    k = k_ref[...]
    vb = v_ref[...].astype(BF16)
    qm = (qs * jnp.exp(bc - mid)).astype(BF16)
    km = (k * jnp.exp(mid - bc)).astype(BF16)
    q_in = (qs * jnp.exp(bc)).astype(BF16)
    k_st = (k * jnp.exp(last - bc)).astype(BF16)
    dlast = jnp.exp(last)

    ri = _iota((t, t), 0)
    ci = _iota((t, t), 1)
    incl = (_idiv(ri, GCHUNK) == _idiv(ci, GCHUNK)) & (ci <= ri)
    attn = [jnp.where(incl, _dg(qm[:, kl[h]], km[:, kl[h]], NT), 0.0).astype(BF16) for h in heads]
    o_intra = [_dg(attn[h], vb[:, vl[h]]) for h in heads]

    for c in range(nchunk):
        rows = slice(c * GCHUNK, (c + 1) * GCHUNK)
        st = [st_ref[h] for h in heads]
        for h in heads:
            oi_ref[rows, vl[h]] = _dg(q_in[rows, kl[h]], st[h].astype(BF16), NT)
        for h in heads:
            st_ref[h] = (st[h] * dlast[c * GCHUNK:c * GCHUNK + 1, kl[h]]
                         + _dg(vb[rows, vl[h]], k_st[rows, kl[h]], TN))
    for h in heads:
        o = o_intra[h] + oi_ref[:, vl[h]]
        ms = jnp.mean(o * o, axis=-1, keepdims=True)
        o = o * lax.rsqrt(ms + EPS) * nw_ref[0:1, :]
        o_ref[:, vl[h]] = (o * _silu(g_ref[:, vl[h]])).astype(BF16)


def _gla(p, gkwp, gkb8, nw8):
    l = p.shape[0]
    t = min(GT, l)
    return pl.pallas_call(
        _gla_kernel,
        grid=(l // t,),
        in_specs=[pl.BlockSpec((t, GKEY), lambda i: (i, O_Q // GKEY)),
                  pl.BlockSpec((t, GKEY), lambda i: (i, O_K // GKEY)),
                  pl.BlockSpec((t, GVAL), lambda i: (i, O_V // GVAL)),
                  pl.BlockSpec((t, GVAL), lambda i: (i, O_G // GVAL)),
                  pl.BlockSpec((t, LANE), lambda i: (i, OP_GD // LANE)),
                  pl.BlockSpec((LANE, GKEY), lambda i: (0, 0)),
                  pl.BlockSpec((SUB, GKEY), lambda i: (0, 0)),
                  pl.BlockSpec((SUB, GHV), lambda i: (0, 0))],
        out_specs=pl.BlockSpec((t, GVAL), lambda i: (i, 0)),
        out_shape=jax.ShapeDtypeStruct((l, GVAL), BF16),
        scratch_shapes=[pltpu.VMEM((GHEADS, GHV, GHK), F32), pltpu.VMEM((t, GVAL), F32)],
        compiler_params=_params(dimension_semantics=("arbitrary",)),
        name="gla",
    )(p, p, p, p, p, gkwp, gkb8, nw8)


def _rows8(*rows):
    n = rows[0].shape[0]
    parts = [r.astype(F32)[None, :] for r in rows]
    if len(rows) < SUB:
        parts.append(jnp.zeros((SUB - len(rows), n), F32))
    return jnp.concatenate(parts, axis=0)


def _pad_cols(w, n):
    return jnp.concatenate([w, jnp.zeros((w.shape[0], n - w.shape[1]), w.dtype)], axis=1)


def _pad_rows(w, n):
    return jnp.concatenate([w, jnp.zeros((n - w.shape[0], w.shape[1]), w.dtype)], axis=0)


N_MAIN_BLK = E_MAIN // PTN
N_ZXBC_BLK = ZXBC_COLS // PTN


def _even_col0(j):
    main = j * PTN
    zxbc = E_ZXBC + (j - N_MAIN_BLK) * PTN
    tail = jnp.where(j == N_MAIN_BLK + N_ZXBC_BLK, E_WD, E_COLS - PTN)
    return jnp.where(j < N_MAIN_BLK, main, jnp.where(j < N_MAIN_BLK + N_ZXBC_BLK, zxbc, tail))


def _even_shifted(j):
    return (j < N_MAIN_BLK) | (j == N_MAIN_BLK + N_ZXBC_BLK)


def _odd_col0(j):
    return jnp.where(j < O_MAIN // PTN, j * PTN, O_COLS - PTN)


def _lora_rows(w, row0):
    return jnp.concatenate([jnp.zeros((row0, w.shape[1]), w.dtype), w,
                            jnp.zeros((S_BLK - row0 - w.shape[0], w.shape[1]), w.dtype)], axis=0)


def kernel(x, norm_w, final_norm_w, w_in_even, w_out_even, rwkv_mu, rwkv_w0, rwkv_w2, rwkv_a0, rwkv_a2, rwkv_k_k, rwkv_k_a, rwkv_r_k, rwkv_ln_w, rwkv_ln_b, rwkv_v0, rwkv_v1, rwkv_v2, ssm_conv_w, ssm_conv_b, ssm_dt_bias, ssm_A_log, ssm_D, ssm_norm_w, w_in_odd, w_out_odd, gla_gk_w, gla_gk_b, gla_norm_w):
    bsz, l, d = x.shape
    depth = norm_w.shape[0]
    w_out_even_bf = w_out_even.astype(BF16)
    w_out_odd_bf = w_out_odd.astype(BF16)
    wt_even = jnp.swapaxes(w_in_even, 1, 2)
    wt_odd = jnp.swapaxes(w_in_odd, 1, 2)
    outs = []
    for b in range(bsz):
        res = x[b]
        h = _rmsnorm(res, _rows8(norm_w[0]))
        p_first = None
        for layer in range(depth):
            i = layer // 2
            last = layer == depth - 1
            nw_next = _rows8(final_norm_w if last else norm_w[layer + 1])
            if layer % 2 == 0:
                mu = rwkv_mu[i]
                mu_all = jnp.concatenate([mu[:E_MAIN], jnp.zeros((ZXBC_COLS,), F32), mu[E_WD:E_ZXBC],
                                          jnp.zeros((P_COLS - P_LORA - (E_ZXBC - E_WD),), F32)])
                p = _inproj(h, wt_even, i, _even_col0, P_COLS // PTN, PTN, mu8=_rows8(mu_all),
                            shifted_block=_even_shifted)
                v0 = rwkv_v0[i - 1] if i > 0 else jnp.zeros((RW,), F32)
                pa = _rows8(rwkv_w0[i], rwkv_a0[i], rwkv_k_k[i], rwkv_k_a[i],
                            rwkv_r_k[i].reshape(RW), rwkv_ln_w[i], rwkv_ln_b[i], v0)
                w2p = _lora_rows(rwkv_w2[i], 0)
                a2p = _lora_rows(rwkv_a2[i], E_AD - E_WD)
                if i == 0:
                    vmix = None
                    p_first = p
                else:
                    tl = _vlora(p, _pad_cols(rwkv_v1[i - 1], LANE))
                    vmix = (tl, _pad_rows(rwkv_v2[i - 1], LANE), p_first)
                (atp, rp, bh, kh, vb, u0, y0, bonus, gs, dl) = _rwkv_prep(p, pa, w2p, a2p, vmix)
                y_a = _rwkv_scan(atp, rp, bh, kh, vb, u0, y0, bonus, gs, dl, pa)

                cw8 = jnp.concatenate([ssm_conv_w[i], ssm_conv_b[i][None, :],
                                       jnp.zeros((SUB - SCONV - 1, ssm_conv_w.shape[2]), F32)], axis=0)
                lead = jnp.zeros((S_DT_LANE0,), F32)
                hp = _rows8(jnp.concatenate([lead, ssm_dt_bias[i]]), jnp.concatenate([lead, ssm_A_log[i]]))
                dx8 = _rows8(jnp.repeat(ssm_D[i], SHEAD))
                y_b = _ssd(p, cw8, hp, dx8, _rows8(ssm_norm_w[i]))
                o = _outproj([y_a, y_b], w_out_even_bf, i, res, nw_next, last)
            else:
                p = _inproj(h, wt_odd, i, _odd_col0, OP_COLS // PTN, PTN)
                gkwp = jnp.concatenate([jnp.zeros((O_GD_LANE0, GKEY), F32), gla_gk_w[i]], axis=0)
                y = _gla(p, gkwp, _rows8(gla_gk_b[i]), _rows8(gla_norm_w[i]))
                o = _outproj([y], w_out_odd_bf, i, res, nw_next, last)
            if last:
                res = o[0]
            else:
                res, h = o
        outs.append(res)
    return jnp.stack(outs).astype(x.dtype)
```

```python
import functools
import math

import jax
import jax.numpy as jnp
from jax import lax
from jax.experimental import pallas as pl
from jax.experimental.pallas import tpu as pltpu

F32 = jnp.float32
BF16 = jnp.bfloat16

D_MODEL = 2048
EPS = 1e-5
RW = 2048
RH = 64
DECAY_LORA = 96
AAA_LORA = 96
MV_LORA = 64
RWKV_GN_EPS = 64e-5
RCHUNK = 64
SW = 2048
SHEAD = 64
SGROUPS = 4
SSTATE = 128
SCONV = 4
SCHUNK = 128
SGW = SW // SGROUPS
SHG = SGW // SHEAD
GHEADS = 4
GKEY = 1024
GVAL = 2048
GHK = 256
GHV = 512
GRANK = 16
GCHUNK = 64

LANE = 128
SUB = 8
VMEM_LIMIT = 56 * 1024 * 1024

PTN = 512
E_R, E_K, E_V, E_G = 0, 2048, 4096, 6144
E_MAIN = 8192
E_WD = 8192
E_AD = 8288
E_ZXBC = 8384
E_COLS = 13536
ZXBC_COLS = 5120
P_Z, P_X = E_MAIN, E_MAIN + 2048
P_B, P_C = E_MAIN + 4096, E_MAIN + 4608
P_LORA = E_MAIN + ZXBC_COLS
P_COLS = P_LORA + 2 * PTN
P_DT = P_COLS - LANE
S_BLK = 256
S_DT_LANE0 = LANE - SW // SHEAD
O_COLS = 6160
O_Q, O_K, O_V, O_G = 0, 1024, 2048, 4096
O_MAIN = 6144
OP_COLS = O_MAIN + PTN
OP_GD = OP_COLS - LANE
O_GD_LANE0 = LANE - GRANK

NN = (((1,), (0,)), ((), ()))
NT = (((1,), (1,)), ((), ()))
TN = (((0,), (0,)), ((), ()))


def _dg(a, b, dims=NN):
    return lax.dot_general(a, b, dims, preferred_element_type=F32)


def _bdot(a, b, dims=NN):
    return _dg(a.astype(BF16), b.astype(BF16), dims)


def _split2(x):
    hi = x.astype(BF16)
    lo = (x - hi.astype(F32)).astype(BF16)
    return hi, lo


def _split3(x):
    hi = x.astype(BF16)
    r = x - hi.astype(F32)
    mid = r.astype(BF16)
    lo = (r - mid.astype(F32)).astype(BF16)
    return hi, mid, lo


def _dot3(a, b, dims=NN):
    ah, al = _split2(a)
    bh, bl = _split2(b)
    return _dg(ah, bh, dims) + (_dg(ah, bl, dims) + _dg(al, bh, dims))


def _dot_sel(a, sel_bf16, dims=NN):
    ah, am, al = _split3(a)
    return _dg(ah, sel_bf16, dims) + (_dg(am, sel_bf16, dims) + _dg(al, sel_bf16, dims))


EXP_NEG_HALF = math.exp(-0.5)


def _sigmoid(x):
    return 0.5 * jnp.tanh(0.5 * x) + 0.5


def _silu(x):
    return x * _sigmoid(x)


def _log_sigmoid(x):
    return jnp.minimum(x, 0.0) - jnp.log(1.0 + jnp.exp(-jnp.abs(x)))


def _softplus(x):
    return jnp.maximum(x, 0.0) + jnp.log(1.0 + jnp.exp(-jnp.abs(x)))


def _iota(shape, axis):
    return lax.broadcasted_iota(jnp.int32, shape, axis)


def _idiv(x, n):
    return lax.shift_right_logical(x, n.bit_length() - 1)


def _seg_cumsum(x, chunk):
    rin = _iota(x.shape, 0) & (chunk - 1)
    s = 1
    while s < chunk:
        x = x + jnp.where(rin >= s, pltpu.roll(x, s, 0), 0.0)
        s *= 2
    return x


def _chunk_row(x, chunk, r):
    t, w = x.shape
    x3 = x.reshape(t // chunk, chunk, w)
    return jnp.broadcast_to(x3[:, r:r + 1, :], x3.shape).reshape(t, w)


def _head_ones(n, head, scale=1.0):
    r = _idiv(_iota((n, n), 0), head)
    c = _idiv(_iota((n, n), 1), head)
    return jnp.where(r == c, scale, 0.0).astype(BF16)


def _params(**kw):
    return pltpu.CompilerParams(vmem_limit_bytes=VMEM_LIMIT, **kw)


def _rmsnorm_kernel(x_ref, w_ref, o_ref):
    x = x_ref[...]
    ms = jnp.mean(x * x, axis=-1, keepdims=True)
    o_ref[...] = (x * lax.rsqrt(ms + EPS) * w_ref[0:1, :]).astype(BF16)


def _rmsnorm(x, w8, tm=512):
    l, d = x.shape
    return pl.pallas_call(
        _rmsnorm_kernel,
        grid=(l // tm,),
        in_specs=[pl.BlockSpec((tm, d), lambda i: (i, 0)),
                  pl.BlockSpec((SUB, d), lambda i: (0, 0))],
        out_specs=pl.BlockSpec((tm, d), lambda i: (i, 0)),
        out_shape=jax.ShapeDtypeStruct((l, d), BF16),
        compiler_params=_params(dimension_semantics=("parallel",)),
        name="rmsnorm",
    )(x, w8)


HALO = 16


def _inproj_kernel(*refs, shift, shifted_block):
    if shift:
        h_ref, hh_ref, w_ref, mu_ref, o_ref = refs
    else:
        h_ref, w_ref, o_ref = refs
    w = w_ref[0].astype(BF16)
    if not shift:
        o_ref[...] = _dg(h_ref[...], w, NT)
        return
    do_shift = shifted_block(pl.program_id(1))

    @pl.when(do_shift)
    def _():
        p = _dg(h_ref[...], w, NT)
        prev = _dg(hh_ref[...], w, NT)[HALO - 1:HALO, :]
        prev = jnp.where(pl.program_id(0) == 0, 0.0, prev)
        pprev = jnp.where(_iota(p.shape, 0) == 0, prev, pltpu.roll(p, 1, 0))
        o_ref[...] = p + (pprev - p) * mu_ref[0:1, :]

    @pl.when(jnp.logical_not(do_shift))
    def _():
        o_ref[...] = _dg(h_ref[...], w, NT)


def _inproj(h, wt, layer, col0, nblk, tn, mu8=None, shifted_block=None, tm=2048):
    l, d = h.shape
    tm = min(tm, l)
    shift = mu8 is not None
    in_specs = [pl.BlockSpec((tm, d), lambda i, j: (i, 0))]
    args = [h]
    if shift:
        hb = tm // HALO
        in_specs.append(pl.BlockSpec((HALO, d), lambda i, j: (jnp.maximum(i * hb - 1, 0), 0)))
        args.append(h)
    def w_map(i, j):
        c = col0(j)
        return (layer, c if isinstance(c, int) else pl.multiple_of(c, SUB), 0)

    in_specs.append(pl.BlockSpec((pl.Element(1), pl.Element(tn), pl.Element(d)), w_map))
    args.append(wt)
    if shift:
        in_specs.append(pl.BlockSpec((SUB, tn), lambda i, j: (0, j)))
        args.append(mu8)
    return pl.pallas_call(
        functools.partial(_inproj_kernel, shift=shift, shifted_block=shifted_block),
        grid=(l // tm, nblk),
        in_specs=in_specs,
        out_specs=pl.BlockSpec((tm, tn), lambda i, j: (i, j)),
        out_shape=jax.ShapeDtypeStruct((l, nblk * tn), F32),
        compiler_params=_params(dimension_semantics=("parallel", "parallel")),
        name="inproj",
    )(*args)


def _outproj_kernel(*refs, n_in, final):
    y_refs = refs[:n_in]
    w_ref, res_ref, nw_ref = refs[n_in:n_in + 3]
    outs = refs[n_in + 3:]
    r = res_ref[...]
    for s in range(n_in):
        kw = y_refs[s].shape[1]
        r = r + jnp.dot(y_refs[s][...], w_ref[s * kw:(s + 1) * kw, :], preferred_element_type=F32)
    ms = jnp.mean(r * r, axis=-1, keepdims=True)
    hn = r * lax.rsqrt(ms + EPS) * nw_ref[0:1, :]
    if final:
        outs[0][...] = hn
    else:
        outs[0][...] = r
        outs[1][...] = hn.astype(BF16)


def _outproj(ys, w, layer, res, nw8, final, tm=512):
    l, d = res.shape
    n_in = len(ys)
    kdim = w.shape[1]
    tm = min(tm, l)
    in_specs = [pl.BlockSpec((tm, y.shape[1]), lambda i: (i, 0)) for y in ys]
    in_specs += [pl.BlockSpec((None, kdim, d), lambda i: (layer, 0, 0), pipeline_mode=pl.Buffered(1)),
                 pl.BlockSpec((tm, d), lambda i: (i, 0)),
                 pl.BlockSpec((SUB, d), lambda i: (0, 0))]
    row_spec = pl.BlockSpec((tm, d), lambda i: (i, 0))
    if final:
        out_specs = [row_spec]
        out_shape = [jax.ShapeDtypeStruct((l, d), F32)]
    else:
        out_specs = [row_spec, row_spec]
        out_shape = [jax.ShapeDtypeStruct((l, d), F32), jax.ShapeDtypeStruct((l, d), BF16)]
    return pl.pallas_call(
        functools.partial(_outproj_kernel, n_in=n_in, final=final),
        grid=(l // tm,),
        in_specs=in_specs,
        out_specs=out_specs,
        out_shape=out_shape,
        compiler_params=_params(dimension_semantics=("parallel",)),
        name="outproj",
    )(*ys, w, res, nw8)


def _vlora_kernel(v_ref, v1_ref, o_ref):
    o_ref[...] = _dot3(v_ref[...], v1_ref[...])


def _vlora(p, v1p, t=512):
    l = p.shape[0]
    t = min(t, l)
    cb = E_V // RW
    return pl.pallas_call(
        _vlora_kernel,
        grid=(l // t,),
        in_specs=[pl.BlockSpec((t, RW), lambda i: (i, cb)),
                  pl.BlockSpec((RW, LANE), lambda i: (0, 0))],
        out_specs=pl.BlockSpec((t, LANE), lambda i: (i, 0)),
        out_shape=jax.ShapeDtypeStruct((l, LANE), F32),
        compiler_params=_params(dimension_semantics=("parallel",)),
        name="rwkv_vlora",
    )(p, v1p)


RT = 1024
RHALF = 1024
REW = 1024
RST = 256


def _rwkv_prep_kernel(*refs, has_vmix):
    (r_ref, k_ref, v_ref, g_ref, lo_ref, pa_ref, w2_ref, a2_ref) = refs[:8]
    pos = 8
    if has_vmix:
        tl_ref, v2_ref, vf_ref = refs[pos:pos + 3]
        pos += 3
    (atp_ref, rp_ref, bh_ref, kh2_ref, vb_ref, u0_ref, y0_ref, bonus_ref, gs_ref, dl_ref) = refs[pos:]
    t = r_ref.shape[0]
    half = min(RHALF, t)
    blk = min(REW, t)
    cpb = blk // RCHUNK

    w0, a0, k_k, k_a = (pa_ref[i:i + 1, :] for i in range(4))
    r_k, v0 = pa_ref[4:5, :], pa_ref[7:8, :]
    hm = _head_ones(LANE, RH)
    first_row = (_iota((blk, LANE), 0) & (RCHUNK - 1)) == 0

    def front(r0):
        rs = slice(r0, r0 + blk)
        r, k, v, g, lo = r_ref[rs, :], k_ref[rs, :], v_ref[rs, :], g_ref[rs, :], lo_ref[rs, :]
        logd = -EXP_NEG_HALF * _sigmoid(w0 + _dot3(jnp.tanh(lo), w2_ref[...]))
        a = _sigmoid(a0 + _dot3(lo, a2_ref[...]))
        if has_vmix:
            v = v + (vf_ref[rs, :] - v) * _sigmoid(v0 + _dot3(tl_ref[rs, :], v2_ref[...]))

        kk = k * k_k
        k2h, k2l = _split2(kk * kk)
        ss = _dg(k2h, hm) + _dg(k2l, hm)
        kk = kk * lax.rsqrt(jnp.maximum(ss, 1e-24))
        kh = k * (1.0 + (a - 1.0) * k_a)
        av = -kk
        bv = kk * a

        rkh, rkl = _split2(r * kh * r_k)
        bonus_ref[rs, :] = (_dg(rkh, hm) + _dg(rkl, hm)) * v
        gs_ref[rs, :] = _silu(g)

        cum = _seg_cumsum(logd, RCHUNK)
        e_c = jnp.exp(cum)
        e_n = jnp.exp(-cum)
        e_p = jnp.where(first_row, 1.0, pltpu.roll(e_c, 1, 0))
        p_end = jnp.exp(cum.reshape(cpb, RCHUNK, LANE)[:, RCHUNK - 1:RCHUNK, :])
        e_l = e_n * jnp.broadcast_to(p_end, (cpb, RCHUNK, LANE)).reshape(blk, LANE)
        at = av * e_p
        rt = r * e_c
        bt = bv * e_n
        kt = kh * e_n
        bh_ref[rs, :] = (bv * e_l).astype(BF16)
        kh2_ref[rs, :] = (kh * e_l).astype(BF16)
        vb_ref[rs, :] = v.astype(BF16)
        c0 = r0 // RCHUNK
        dl_ref[c0 * SUB:(c0 + cpb) * SUB, :] = (
            jnp.broadcast_to(p_end, (cpb, SUB, LANE)).reshape(cpb * SUB, LANE))
        out = []
        for c in range(cpb):
            cr = slice(c * RCHUNK, (c + 1) * RCHUNK)
            out.append((r0 + c * RCHUNK, at[cr], rt[cr], bt[cr], kt[cr], v[cr]))
        return out

    lane = _iota((RCHUNK, LANE), 1)
    si = lane & (RH - 1)
    ti = _iota((RCHUNK, LANE), 0)
    strict = si < ti
    incl = si <= ti
    eye = jnp.where(si == ti, 1.0, 0.0)
    head0 = lane < RH

    def stack(xc):
        xc = xc.astype(BF16)
        zero = jnp.zeros_like(xc)
        return jnp.concatenate([jnp.where(head0, xc, zero), jnp.where(head0, zero, xc)], axis=0)

    def chains(ops):
        chunks = range(len(ops))
        ats = [stack(o[1]) for o in ops]
        vs = [stack(o[5]) for o in ops]
        amat = []
        for (_, at, rt, bt, kt, _) in ops:
            lhs = jnp.concatenate([at, rt], axis=0).astype(BF16)
            rhs = jnp.concatenate([stack(bt), stack(kt)], axis=0)
            amat.append(_dg(lhs, rhs, NT))
        yield
        a_ab = [jnp.where(strict, m[:RCHUNK, :LANE], 0.0) for m in amat]
        a_ak = [jnp.where(strict, m[:RCHUNK, LANE:], 0.0) for m in amat]
        a_rb = [jnp.where(incl, m[RCHUNK:, :LANE], 0.0).astype(BF16) for m in amat]
        a_rk = [jnp.where(incl, m[RCHUNK:, LANE:], 0.0) for m in amat]
        akv2 = [_dg(jnp.concatenate([a_ak[c], a_rk[c]], axis=0).astype(BF16), vs[c]) for c in chunks]
        akv = [m[:RCHUNK] for m in akv2]
        arkv = [m[RCHUNK:] for m in akv2]
        yield
        inv = [eye + m for m in a_ab]
        pw = [_dg(m.astype(BF16), stack(m)) for m in a_ab]
        yield
        s = 2
        while s < RCHUNK // 2:
            both = [_dg(pw[c].astype(BF16), jnp.concatenate([stack(pw[c]), stack(inv[c])], axis=1))
                    for c in chunks]
            pw = [m[:, :LANE] for m in both]
            inv = [inv[c] + both[c][:, LANE:] for c in chunks]
            s *= 2
            yield
        inv = [inv[c] + _dg(pw[c].astype(BF16), stack(inv[c])) for c in chunks]
        yield
        x = [_dg(inv[c].astype(BF16), jnp.concatenate([ats[c], stack(akv[c])], axis=1)) for c in chunks]
        yield
        z = [_dg(a_rb[c], jnp.concatenate([stack(x[c][:, :LANE]), stack(x[c][:, LANE:])], axis=1))
             for c in chunks]
        for c in chunks:
            orow = slice(ops[c][0], ops[c][0] + RCHUNK)
            atp_ref[orow, :] = x[c][:, :LANE].astype(BF16)
            u0_ref[orow, :] = x[c][:, LANE:]
            rp_ref[orow, :] = (ops[c][2] + z[c][:, :LANE]).astype(BF16)
            y0_ref[orow, :] = z[c][:, LANE:] + arkv[c]

    halves = [list(range(h0, h0 + half, blk)) for h0 in range(0, t, half)]
    ops = [o for r0 in halves[0] for o in front(r0)]
    for hi in range(len(halves)):
        pending = list(halves[hi + 1]) if hi + 1 < len(halves) else []
        nxt = []
        for _ in chains(ops):
            if pending:
                nxt += front(pending.pop(0))
        while pending:
            nxt += front(pending.pop(0))
        ops = nxt


def _rwkv_prep(p, pa, w2p, a2p, vmix):
    l = p.shape[0]
    t = min(RT, l)
    npair = RW // LANE
    has_vmix = vmix is not None

    def main(off):
        cb = off // LANE
        return pl.BlockSpec((t, LANE), lambda q, i: (i, cb + q))

    par = pl.BlockSpec((SUB, LANE), lambda q, i: (0, q))
    lora = pl.BlockSpec((LANE, LANE), lambda q, i: (0, q))
    lora_in = pl.BlockSpec((S_BLK, LANE), lambda q, i: (0, q))
    row = pl.BlockSpec((t, LANE), lambda q, i: (i, q))

    in_specs = [main(E_R), main(E_K), main(E_V), main(E_G),
                pl.BlockSpec((t, S_BLK), lambda q, i: (i, P_LORA // S_BLK)), par, lora_in, lora_in]
    args = [p] * 5 + [pa, w2p, a2p]
    if has_vmix:
        tl, v2p, p_first = vmix
        in_specs += [pl.BlockSpec((t, LANE), lambda q, i: (i, 0)), lora, main(E_V)]
        args += [tl, v2p, p_first]
    dl_rows = (t // RCHUNK) * SUB
    out_specs = [row] * 9 + [pl.BlockSpec((dl_rows, LANE), lambda q, i: (i, q))]
    sd = jax.ShapeDtypeStruct
    out_shape = ([sd((l, RW), BF16)] * 5 + [sd((l, RW), F32)] * 4
                 + [sd((l // RCHUNK * SUB, RW), F32)])
    return pl.pallas_call(
        functools.partial(_rwkv_prep_kernel, has_vmix=has_vmix),
        grid=(npair, l // t),
        in_specs=in_specs,
        out_specs=out_specs,
        out_shape=out_shape,
        compiler_params=_params(dimension_semantics=("parallel", "parallel")),
        name="rwkv_prep",
    )(*args)


def _rwkv_scan_kernel(atp_ref, rp_ref, bh_ref, kh_ref, vb_ref, u0_ref, y0_ref,
                      bonus_ref, gs_ref, dl_ref, pa_ref, o_ref, s_ref, y_ref):
    @pl.when(pl.program_id(0) == 0)
    def _():
        s_ref[...] = jnp.zeros_like(s_ref)

    t = atp_ref.shape[0]
    npair = RW // LANE
    bd = _idiv(_iota((LANE, LANE), 0), RH) == _idiv(_iota((LANE, LANE), 1), RH)
    hmean = _head_ones(LANE, RH, 1.0 / RH)

    pairs = range(npair)
    lanes = [slice(q * LANE, (q + 1) * LANE) for q in pairs]
    for c in range(t // RCHUNK):
        rows = slice(c * RCHUNK, (c + 1) * RCHUNK)
        s_old = [s_ref[q] for q in pairs]
        o = [_dg(jnp.concatenate([atp_ref[rows, lanes[q]], rp_ref[rows, lanes[q]]], axis=0),
                 s_old[q].astype(BF16), NT) for q in pairs]
        ds = []
        for q in pairs:
            u = o[q][:RCHUNK] + u0_ref[rows, lanes[q]]
            y_ref[rows, lanes[q]] = o[q][RCHUNK:] + y0_ref[rows, lanes[q]]
            uv = jnp.concatenate([u.astype(BF16), vb_ref[rows, lanes[q]]], axis=0)
            bk = jnp.concatenate([bh_ref[rows, lanes[q]], kh_ref[rows, lanes[q]]], axis=0)
            ds.append(_dg(uv, bk, TN))
        for q in pairs:
            dl = dl_ref[c * SUB:c * SUB + 1, lanes[q]]
            s_ref[q] = s_old[q] * dl + jnp.where(bd, ds[q], 0.0)

    for q in range(npair):
        sl = slice(q * LANE, (q + 1) * LANE)
        y = y_ref[:, sl]
        yh, yl = _split2(y)
        mean = _dg(yh, hmean) + _dg(yl, hmean)
        d = y - mean
        var = _bdot(d * d, hmean)
        yn = d * lax.rsqrt(var + RWKV_GN_EPS) * pa_ref[5:6, sl] + pa_ref[6:7, sl]
        o_ref[:, sl] = ((yn + bonus_ref[:, sl]) * gs_ref[:, sl]).astype(BF16)


def _rwkv_scan(atp, rp, bh, kh, vb, u0, y0, bonus, gs, dl, pa):
    l = atp.shape[0]
    t = min(RST, l)
    row = pl.BlockSpec((t, RW), lambda i: (i, 0))
    return pl.pallas_call(
        _rwkv_scan_kernel,
        grid=(l // t,),
        in_specs=[row] * 9 + [pl.BlockSpec(((t // RCHUNK) * SUB, RW), lambda i: (i, 0)),
                              pl.BlockSpec((SUB, RW), lambda i: (0, 0))],
        out_specs=row,
        out_shape=jax.ShapeDtypeStruct((l, RW), BF16),
        scratch_shapes=[pltpu.VMEM((RW // LANE, LANE, LANE), F32), pltpu.VMEM((t, RW), F32)],
        compiler_params=_params(dimension_semantics=("arbitrary",)),
        name="rwkv_scan",
    )(atp, rp, bh, kh, vb, u0, y0, bonus, gs, dl, pa)


def _ssd_conv(main_ref, halo_ref, wb_ref, buf_ref, first):
    q = main_ref.shape[0]
    buf_ref[0:SUB, :] = jnp.where(first, 0.0, halo_ref[...])
    buf_ref[SUB:SUB + q, :] = main_ref[...]
    xb = buf_ref[...]
    acc = wb_ref[SCONV:SCONV + 1, :] + xb[SUB:, :] * wb_ref[SCONV - 1:SCONV, :]
    for s in range(1, SCONV):
        acc = acc + pltpu.roll(xb, s, 0)[SUB:, :] * wb_ref[SCONV - 1 - s:SCONV - s, :]
    return _silu(acc)


def _ssd_kernel(z_ref, x_ref, b_ref, c_ref, dt_ref, xh_ref, bhalo_ref, chalo_ref,
                cwx_ref, cwb_ref, cwc_ref, hp_ref, dx_ref, nw_ref,
                o_ref, prev_ref, xbuf_ref, bbuf_ref, cbuf_ref, ybuf_ref):
    first = pl.program_id(0) == 0
    q = z_ref.shape[0]
    groups = range(SGROUPS)

    @pl.when(first)
    def _():
        prev_ref[...] = jnp.zeros_like(prev_ref)

    xs = _ssd_conv(x_ref, xh_ref, cwx_ref, xbuf_ref, first)
    bmb = _ssd_conv(b_ref, bhalo_ref, cwb_ref, bbuf_ref, first).astype(BF16)
    cmb = _ssd_conv(c_ref, chalo_ref, cwc_ref, cbuf_ref, first).astype(BF16)

    dt = _softplus(dt_ref[...] + hp_ref[0:1, :])
    a_neg = -jnp.exp(hp_ref[1:2, :])
    acs = _seg_cumsum(dt * a_neg, q)
    acs_t = acs.T
    acs_last = acs[q - 1:q, :]

    expand = jnp.where(_iota((LANE, SW), 0) == _idiv(_iota((LANE, SW), 1), SHEAD) + S_DT_LANE0,
                       1.0, 0.0).astype(BF16)
    th, tl = _split2(jnp.concatenate([dt, jnp.exp(acs), jnp.exp(acs_last - acs)], axis=0))
    ex = _dg(jnp.concatenate([th, tl], axis=0), expand)
    ex = ex[:3 * q] + ex[3 * q:]
    dt_e, eacs_e, ds_e = ex[:q], ex[q:2 * q], ex[2 * q:]
    xc = xs * dt_e
    xcd = (xc * ds_e).astype(BF16)

    gl = [slice(g * SSTATE, (g + 1) * SSTATE) for g in groups]
    gc = [slice(g * SGW, (g + 1) * SGW) for g in groups]
    scores = [_dg(cmb[:, gl[g]], bmb[:, gl[g]], NT) for g in groups]
    prev = prev_ref[...]
    prevb = prev.astype(BF16)
    y_off = [_dg(cmb[:, gl[g]], prevb[:, gc[g]]) for g in groups]
    states = [_dg(bmb[:, gl[g]], xcd[:, gc[g]], TN) for g in groups]
    for g in groups:
        prev_ref[:, gc[g]] = prev[:, gc[g]] * eacs_e[q - 1:q, gc[g]] + states[g]

    causal = _iota((q, q), 1) <= _iota((q, q), 0)
    for h in range(SW // SHEAD):
        lane = S_DT_LANE0 + h
        hc = slice(h * SHEAD, (h + 1) * SHEAD)
        diff = acs[:, lane:lane + 1] - acs_t[lane:lane + 1, :]
        lm = jnp.exp(jnp.where(causal, diff, -jnp.inf))
        ybuf_ref[:, hc] = _bdot(scores[h // SHG] * lm, xc[:, hc])
    for g in groups:
        y = ybuf_ref[:, gc[g]] + y_off[g] * eacs_e[:, gc[g]] + dx_ref[0:1, gc[g]] * xs[:, gc[g]]
        y = y * _silu(z_ref[:, gc[g]])
        ms = jnp.mean(y * y, axis=-1, keepdims=True)
        o_ref[:, gc[g]] = (y * lax.rsqrt(ms + EPS) * nw_ref[0:1, gc[g]]).astype(BF16)


def _ssd(p, cw8, hp, dx8, nw8):
    l = p.shape[0]
    q = SCHUNK
    hb = q // SUB
    gn = SGROUPS * SSTATE

    def halo_idx(i):
        return jnp.maximum(i * hb - 1, 0)

    zb, xb = P_Z // SW, P_X // SW
    bb, cb, db = P_B // gn, P_C // gn, P_DT // LANE
    in_specs = [
        pl.BlockSpec((q, SW), lambda i: (i, zb)),
        pl.BlockSpec((q, SW), lambda i: (i, xb)),
        pl.BlockSpec((q, gn), lambda i: (i, bb)),
        pl.BlockSpec((q, gn), lambda i: (i, cb)),
        pl.BlockSpec((q, LANE), lambda i: (i, db)),
        pl.BlockSpec((SUB, SW), lambda i: (halo_idx(i), xb)),
        pl.BlockSpec((SUB, gn), lambda i: (halo_idx(i), bb)),
        pl.BlockSpec((SUB, gn), lambda i: (halo_idx(i), cb)),
        pl.BlockSpec((SUB, SW), lambda i: (0, 0)),
        pl.BlockSpec((SUB, gn), lambda i: (0, SW // gn)),
        pl.BlockSpec((SUB, gn), lambda i: (0, SW // gn + 1)),
        pl.BlockSpec((SUB, LANE), lambda i: (0, 0)),
        pl.BlockSpec((SUB, SW), lambda i: (0, 0)),
        pl.BlockSpec((SUB, SW), lambda i: (0, 0)),
    ]
    return pl.pallas_call(
        _ssd_kernel,
        grid=(l // q,),
        in_specs=in_specs,
        out_specs=pl.BlockSpec((q, SW), lambda i: (i, 0)),
        out_shape=jax.ShapeDtypeStruct((l, SW), BF16),
        scratch_shapes=[pltpu.VMEM((SSTATE, SW), F32),
                        pltpu.VMEM((q + SUB, SW), F32),
                        pltpu.VMEM((q + SUB, gn), F32),
                        pltpu.VMEM((q + SUB, gn), F32),
                        pltpu.VMEM((q, SW), F32)],
        compiler_params=_params(dimension_semantics=("arbitrary",)),
        name="ssd",
    )(p, p, p, p, p, p, p, p, cw8, cw8, cw8, hp, dx8, nw8)


GT = 256


def _gla_kernel(q_ref, k_ref, v_ref, g_ref, gd_ref, gkw_ref, gkb_ref, nw_ref, o_ref, st_ref, oi_ref):
    @pl.when(pl.program_id(0) == 0)
    def _():
        st_ref[...] = jnp.zeros_like(st_ref)

    t = q_ref.shape[0]
    nchunk = t // GCHUNK
    heads = range(GHEADS)
    kl = [slice(h * GHK, (h + 1) * GHK) for h in heads]
    vl = [slice(h * GHV, (h + 1) * GHV) for h in heads]
    gk = _log_sigmoid(_dot3(gd_ref[...], gkw_ref[...]) + gkb_ref[0:1, :]) * (1.0 / 16.0)
    bc = _seg_cumsum(gk, GCHUNK)
    mid = _chunk_row(bc, GCHUNK, GCHUNK // 2)
    last = _chunk_row(bc, GCHUNK, GCHUNK - 1)
    qs = q_ref[...] * (GHK ** -0.5)
    k = k_ref[...]
    vb = v_ref[...].astype(BF16)
    qm = (qs * jnp.exp(bc - mid)).astype(BF16)
    km = (k * jnp.exp(mid - bc)).astype(BF16)
    q_in = (qs * jnp.exp(bc)).astype(BF16)
    k_st = (k * jnp.exp(last - bc)).astype(BF16)
    dlast = jnp.exp(last)

    ri = _iota((t, t), 0)
    ci = _iota((t, t), 1)
    incl = (_idiv(ri, GCHUNK) == _idiv(ci, GCHUNK)) & (ci <= ri)
    attn = [jnp.where(incl, _dg(qm[:, kl[h]], km[:, kl[h]], NT), 0.0).astype(BF16) for h in heads]
    o_intra = [_dg(attn[h], vb[:, vl[h]]) for h in heads]

    for c in range(nchunk):
        rows = slice(c * GCHUNK, (c + 1) * GCHUNK)
        st = [st_ref[h] for h in heads]
        for h in heads:
            oi_ref[rows, vl[h]] = _dg(q_in[rows, kl[h]], st[h].astype(BF16), NT)
        for h in heads:
            st_ref[h] = (st[h] * dlast[c * GCHUNK:c * GCHUNK + 1, kl[h]]
                         + _dg(vb[rows, vl[h]], k_st[rows, kl[h]], TN))
    for h in heads:
        o = o_intra[h] + oi_ref[:, vl[h]]
        ms = jnp.mean(o * o, axis=-1, keepdims=True)
        o = o * lax.rsqrt(ms + EPS) * nw_ref[0:1, :]
        o_ref[:, vl[h]] = (o * _silu(g_ref[:, vl[h]])).astype(BF16)


def _gla(p, gkwp, gkb8, nw8):
    l = p.shape[0]
    t = min(GT, l)
    return pl.pallas_call(
        _gla_kernel,
        grid=(l // t,),
        in_specs=[pl.BlockSpec((t, GKEY), lambda i: (i, O_Q // GKEY)),
                  pl.BlockSpec((t, GKEY), lambda i: (i, O_K // GKEY)),
                  pl.BlockSpec((t, GVAL), lambda i: (i, O_V // GVAL)),
                  pl.BlockSpec((t, GVAL), lambda i: (i, O_G // GVAL)),
                  pl.BlockSpec((t, LANE), lambda i: (i, OP_GD // LANE)),
                  pl.BlockSpec((LANE, GKEY), lambda i: (0, 0)),
                  pl.BlockSpec((SUB, GKEY), lambda i: (0, 0)),
                  pl.BlockSpec((SUB, GHV), lambda i: (0, 0))],
        out_specs=pl.BlockSpec((t, GVAL), lambda i: (i, 0)),
        out_shape=jax.ShapeDtypeStruct((l, GVAL), BF16),
        scratch_shapes=[pltpu.VMEM((GHEADS, GHV, GHK), F32), pltpu.VMEM((t, GVAL), F32)],
        compiler_params=_params(dimension_semantics=("arbitrary",)),
        name="gla",
    )(p, p, p, p, p, gkwp, gkb8, nw8)


def _rows8(*rows):
    n = rows[0].shape[0]
    parts = [r.astype(F32)[None, :] for r in rows]
    if len(rows) < SUB:
        parts.append(jnp.zeros((SUB - len(rows), n), F32))
    return jnp.concatenate(parts, axis=0)


def _pad_cols(w, n):
    return jnp.concatenate([w, jnp.zeros((w.shape[0], n - w.shape[1]), w.dtype)], axis=1)


def _pad_rows(w, n):
    return jnp.concatenate([w, jnp.zeros((n - w.shape[0], w.shape[1]), w.dtype)], axis=0)


N_MAIN_BLK = E_MAIN // PTN
N_ZXBC_BLK = ZXBC_COLS // PTN


def _even_col0(j):
    main = j * PTN
    zxbc = E_ZXBC + (j - N_MAIN_BLK) * PTN
    tail = jnp.where(j == N_MAIN_BLK + N_ZXBC_BLK, E_WD, E_COLS - PTN)
    return jnp.where(j < N_MAIN_BLK, main, jnp.where(j < N_MAIN_BLK + N_ZXBC_BLK, zxbc, tail))


def _even_shifted(j):
    return (j < N_MAIN_BLK) | (j == N_MAIN_BLK + N_ZXBC_BLK)


def _odd_col0(j):
    return jnp.where(j < O_MAIN // PTN, j * PTN, O_COLS - PTN)


def _lora_rows(w, row0):
    return jnp.concatenate([jnp.zeros((row0, w.shape[1]), w.dtype), w,
                            jnp.zeros((S_BLK - row0 - w.shape[0], w.shape[1]), w.dtype)], axis=0)


def kernel(x, norm_w, final_norm_w, w_in_even, w_out_even, rwkv_mu, rwkv_w0, rwkv_w2, rwkv_a0, rwkv_a2, rwkv_k_k, rwkv_k_a, rwkv_r_k, rwkv_ln_w, rwkv_ln_b, rwkv_v0, rwkv_v1, rwkv_v2, ssm_conv_w, ssm_conv_b, ssm_dt_bias, ssm_A_log, ssm_D, ssm_norm_w, w_in_odd, w_out_odd, gla_gk_w, gla_gk_b, gla_norm_w):
    bsz, l, d = x.shape
    depth = norm_w.shape[0]
    w_out_even_bf = w_out_even.astype(BF16)
    w_out_odd_bf = w_out_odd.astype(BF16)
    wt_even = jnp.swapaxes(w_in_even, 1, 2)
    wt_odd = jnp.swapaxes(w_in_odd, 1, 2)
    outs = []
    for b in range(bsz):
        res = x[b]
        h = _rmsnorm(res, _rows8(norm_w[0]))
        p_first = None
        for layer in range(depth):
            i = layer // 2
            last = layer == depth - 1
            nw_next = _rows8(final_norm_w if last else norm_w[layer + 1])
            if layer % 2 == 0:
                mu = rwkv_mu[i]
                mu_all = jnp.concatenate([mu[:E_MAIN], jnp.zeros((ZXBC_COLS,), F32), mu[E_WD:E_ZXBC],
                                          jnp.zeros((P_COLS - P_LORA - (E_ZXBC - E_WD),), F32)])
                p = _inproj(h, wt_even, i, _even_col0, P_COLS // PTN, PTN, mu8=_rows8(mu_all),
                            shifted_block=_even_shifted)
                v0 = rwkv_v0[i - 1] if i > 0 else jnp.zeros((RW,), F32)
                pa = _rows8(rwkv_w0[i], rwkv_a0[i], rwkv_k_k[i], rwkv_k_a[i],
                            rwkv_r_k[i].reshape(RW), rwkv_ln_w[i], rwkv_ln_b[i], v0)
                w2p = _lora_rows(rwkv_w2[i], 0)
                a2p = _lora_rows(rwkv_a2[i], E_AD - E_WD)
                if i == 0:
                    vmix = None
                    p_first = p
                else:
                    tl = _vlora(p, _pad_cols(rwkv_v1[i - 1], LANE))
                    vmix = (tl, _pad_rows(rwkv_v2[i - 1], LANE), p_first)
                (atp, rp, bh, kh, vb, u0, y0, bonus, gs, dl) = _rwkv_prep(p, pa, w2p, a2p, vmix)
                y_a = _rwkv_scan(atp, rp, bh, kh, vb, u0, y0, bonus, gs, dl, pa)

                cw8 = jnp.concatenate([ssm_conv_w[i], ssm_conv_b[i][None, :],
                                       jnp.zeros((SUB - SCONV - 1, ssm_conv_w.shape[2]), F32)], axis=0)
                lead = jnp.zeros((S_DT_LANE0,), F32)
                hp = _rows8(jnp.concatenate([lead, ssm_dt_bias[i]]), jnp.concatenate([lead, ssm_A_log[i]]))
                dx8 = _rows8(jnp.repeat(ssm_D[i], SHEAD))
                y_b = _ssd(p, cw8, hp, dx8, _rows8(ssm_norm_w[i]))
                o = _outproj([y_a, y_b], w_out_even_bf, i, res, nw_next, last)
            else:
                p = _inproj(h, wt_odd, i, _odd_col0, OP_COLS // PTN, PTN)
                gkwp = jnp.concatenate([jnp.zeros((O_GD_LANE0, GKEY), F32), gla_gk_w[i]], axis=0)
                y = _gla(p, gkwp, _rows8(gla_gk_b[i]), _rows8(gla_norm_w[i]))
                o = _outproj([y], w_out_odd_bf, i, res, nw_next, last)
            if last:
                res = o[0]
            else:
                res, h = o
        outs.append(res)
    return jnp.stack(outs).astype(x.dtype)
```

```python
import functools
import math

import jax
import jax.numpy as jnp
from jax import lax
from jax.experimental import pallas as pl
from jax.experimental.pallas import tpu as pltpu

F32 = jnp.float32
BF16 = jnp.bfloat16

EPS = 1e-5
RW = 2048
RH = 64
RWKV_GN_EPS = 64e-5
RCHUNK = 64
SW = 2048
SHEAD = 64
SGROUPS = 4
SSTATE = 128
SCONV = 4
SCHUNK = 128
SGW = SW // SGROUPS
SHG = SGW // SHEAD
GHEADS = 4
GKEY = 1024
GVAL = 2048
GHK = 256
GHV = 512
GRANK = 16
GCHUNK = 64

LANE = 128
SUB = 8
VMEM_LIMIT = 56 * 1024 * 1024

PTN = 512
E_R, E_K, E_V, E_G = 0, 2048, 4096, 6144
E_MAIN = 8192
E_WD = 8192
E_AD = 8288
E_ZXBC = 8384
E_COLS = 13536
T_Z, T_X = 0, 2048
T_B, T_C = 4096, 4608
T_COLS = 5120
S_BLK = 256
S_LORA = 0
S_COLS = 2 * S_BLK
S_DT = S_COLS - LANE
S_DT_COL0 = E_COLS - S_BLK
S_DT_LANE0 = LANE - SW // SHEAD
O_COLS = 6160
O_GD_COL0 = O_COLS - LANE
O_GD_LANE0 = LANE - GRANK
O_Q, O_K, O_V, O_G = 0, 1024, 2048, 4096
O_MAIN = 6144

NN = (((1,), (0,)), ((), ()))
NT = (((1,), (1,)), ((), ()))
TN = (((0,), (0,)), ((), ()))


def _dg(a, b, dims=NN):
    return lax.dot_general(a, b, dims, preferred_element_type=F32)


def _bdot(a, b, dims=NN):
    return _dg(a.astype(BF16), b.astype(BF16), dims)


def _split2(x):
    hi = x.astype(BF16)
    lo = (x - hi.astype(F32)).astype(BF16)
    return hi, lo


def _dot3(a, b, dims=NN):
    ah, al = _split2(a)
    bh, bl = _split2(b)
    return _dg(ah, bh, dims) + (_dg(ah, bl, dims) + _dg(al, bh, dims))


EXP_NEG_HALF = math.exp(-0.5)


def _sigmoid(x):
    return 0.5 * jnp.tanh(0.5 * x) + 0.5


def _silu(x):
    return x * _sigmoid(x)


def _log_sigmoid(x):
    return jnp.minimum(x, 0.0) - jnp.log(1.0 + jnp.exp(-jnp.abs(x)))


def _softplus(x):
    return jnp.maximum(x, 0.0) + jnp.log(1.0 + jnp.exp(-jnp.abs(x)))


def _iota(shape, axis):
    return lax.broadcasted_iota(jnp.int32, shape, axis)


def _idiv(x, n):
    return lax.shift_right_logical(x, n.bit_length() - 1)


def _seg_cumsum(x, chunk):
    rin = _iota(x.shape, 0) & (chunk - 1)
    s = 1
    while s < chunk:
        x = x + jnp.where(rin >= s, pltpu.roll(x, s, 0), 0.0)
        s *= 2
    return x


def _chunk_row(x, chunk, r):
    t, w = x.shape
    x3 = x.reshape(t // chunk, chunk, w)
    return jnp.broadcast_to(x3[:, r:r + 1, :], x3.shape).reshape(t, w)


def _head_ones(n, head, scale=1.0):
    r = _idiv(_iota((n, n), 0), head)
    c = _idiv(_iota((n, n), 1), head)
    return jnp.where(r == c, scale, 0.0).astype(BF16)


def _params(**kw):
    return pltpu.CompilerParams(vmem_limit_bytes=VMEM_LIMIT, **kw)


def _rmsnorm_kernel(x_ref, w_ref, o_ref):
    x = x_ref[...]
    ms = jnp.mean(x * x, axis=-1, keepdims=True)
    o_ref[...] = (x * lax.rsqrt(ms + EPS) * w_ref[0:1, :]).astype(BF16)


def _rmsnorm(x, w8, tm=512):
    l, d = x.shape
    return pl.pallas_call(
        _rmsnorm_kernel,
        grid=(l // tm,),
        in_specs=[pl.BlockSpec((tm, d), lambda i: (i, 0)),
                  pl.BlockSpec((SUB, d), lambda i: (0, 0))],
        out_specs=pl.BlockSpec((tm, d), lambda i: (i, 0)),
        out_shape=jax.ShapeDtypeStruct((l, d), BF16),
        compiler_params=_params(dimension_semantics=("parallel",)),
        name="rmsnorm",
    )(x, w8)


HALO = 16


def _inproj_kernel(*refs, shift):
    if shift:
        h_ref, hh_ref, w_ref, mu_ref, o_ref = refs
    else:
        h_ref, w_ref, o_ref = refs
    w = w_ref[0].astype(BF16)
    p = _dg(h_ref[...], w, NT)
    if shift:
        prev = _dg(hh_ref[...], w, NT)[HALO - 1:HALO, :]
        prev = jnp.where(pl.program_id(0) == 0, 0.0, prev)
        pprev = jnp.where(_iota(p.shape, 0) == 0, prev, pltpu.roll(p, 1, 0))
        p = p + (pprev - p) * mu_ref[0:1, :]
    o_ref[...] = p


def _inproj(h, wt, layer, col0, nblk, tn, mu8=None, tm=2048):
    l, d = h.shape
    tm = min(tm, l)
    shift = mu8 is not None
    in_specs = [pl.BlockSpec((tm, d), lambda i, j: (i, 0))]
    args = [h]
    if shift:
        hb = tm // HALO
        in_specs.append(pl.BlockSpec((HALO, d), lambda i, j: (jnp.maximum(i * hb - 1, 0), 0)))
        args.append(h)
    def w_map(i, j):
        c = col0(j)
        return (layer, c if isinstance(c, int) else pl.multiple_of(c, SUB), 0)

    in_specs.append(pl.BlockSpec((pl.Element(1), pl.Element(tn), pl.Element(d)), w_map))
    args.append(wt)
    if shift:
        in_specs.append(pl.BlockSpec((SUB, tn), lambda i, j: (0, j)))
        args.append(mu8)
    return pl.pallas_call(
        functools.partial(_inproj_kernel, shift=shift),
        grid=(l // tm, nblk),
        in_specs=in_specs,
        out_specs=pl.BlockSpec((tm, tn), lambda i, j: (i, j)),
        out_shape=jax.ShapeDtypeStruct((l, nblk * tn), F32),
        compiler_params=_params(dimension_semantics=("parallel", "parallel")),
        name="inproj",
    )(*args)


def _outproj_kernel(*refs, n_in, final):
    y_refs = refs[:n_in]
    w_ref, res_ref, nw_ref = refs[n_in:n_in + 3]
    outs = refs[n_in + 3:]
    r = res_ref[...]
    for s in range(n_in):
        kw = y_refs[s].shape[1]
        r = r + jnp.dot(y_refs[s][...], w_ref[s * kw:(s + 1) * kw, :], preferred_element_type=F32)
    ms = jnp.mean(r * r, axis=-1, keepdims=True)
    hn = r * lax.rsqrt(ms + EPS) * nw_ref[0:1, :]
    if final:
        outs[0][...] = hn
    else:
        outs[0][...] = r
        outs[1][...] = hn.astype(BF16)


def _outproj(ys, w, layer, res, nw8, final, tm=512):
    l, d = res.shape
    n_in = len(ys)
    kdim = w.shape[1]
    tm = min(tm, l)
    in_specs = [pl.BlockSpec((tm, y.shape[1]), lambda i: (i, 0)) for y in ys]
    in_specs += [pl.BlockSpec((None, kdim, d), lambda i: (layer, 0, 0), pipeline_mode=pl.Buffered(1)),
                 pl.BlockSpec((tm, d), lambda i: (i, 0)),
                 pl.BlockSpec((SUB, d), lambda i: (0, 0))]
    row_spec = pl.BlockSpec((tm, d), lambda i: (i, 0))
    if final:
        out_specs = [row_spec]
        out_shape = [jax.ShapeDtypeStruct((l, d), F32)]
    else:
        out_specs = [row_spec, row_spec]
        out_shape = [jax.ShapeDtypeStruct((l, d), F32), jax.ShapeDtypeStruct((l, d), BF16)]
    return pl.pallas_call(
        functools.partial(_outproj_kernel, n_in=n_in, final=final),
        grid=(l // tm,),
        in_specs=in_specs,
        out_specs=out_specs,
        out_shape=out_shape,
        compiler_params=_params(dimension_semantics=("parallel",)),
        name="outproj",
    )(*ys, w, res, nw8)


def _vlora_kernel(v_ref, v1_ref, o_ref):
    o_ref[...] = _dot3(v_ref[...], v1_ref[...])


def _vlora(p, v1p, t=512):
    l = p.shape[0]
    t = min(t, l)
    cb = E_V // RW
    return pl.pallas_call(
        _vlora_kernel,
        grid=(l // t,),
        in_specs=[pl.BlockSpec((t, RW), lambda i: (i, cb)),
                  pl.BlockSpec((RW, LANE), lambda i: (0, 0))],
        out_specs=pl.BlockSpec((t, LANE), lambda i: (i, 0)),
        out_shape=jax.ShapeDtypeStruct((l, LANE), F32),
        compiler_params=_params(dimension_semantics=("parallel",)),
        name="rwkv_vlora",
    )(p, v1p)


RT = 1024
RHALF = 1024
REW = 1024
RST = 256


def _rwkv_prep_kernel(*refs, has_vmix):
    (r_ref, k_ref, v_ref, g_ref, lo_ref, pa_ref, w2_ref, a2_ref) = refs[:8]
    pos = 8
    if has_vmix:
        tl_ref, v2_ref, vf_ref = refs[pos:pos + 3]
        pos += 3
    (atp_ref, rp_ref, bh_ref, kh2_ref, vb_ref, u0_ref, y0_ref, bonus_ref, gs_ref, dl_ref) = refs[pos:]
    t = r_ref.shape[0]
    half = min(RHALF, t)
    blk = min(REW, t)
    cpb = blk // RCHUNK

    w0, a0, k_k, k_a = (pa_ref[i:i + 1, :] for i in range(4))
    r_k, v0 = pa_ref[4:5, :], pa_ref[7:8, :]
    hm = _head_ones(LANE, RH)
    first_row = (_iota((blk, LANE), 0) & (RCHUNK - 1)) == 0

    def front(r0):
        rs = slice(r0, r0 + blk)
        r, k, v, g, lo = r_ref[rs, :], k_ref[rs, :], v_ref[rs, :], g_ref[rs, :], lo_ref[rs, :]
        logd = -EXP_NEG_HALF * _sigmoid(w0 + _dot3(jnp.tanh(lo), w2_ref[...]))
        a = _sigmoid(a0 + _dot3(lo, a2_ref[...]))
        if has_vmix:
            v = v + (vf_ref[rs, :] - v) * _sigmoid(v0 + _dot3(tl_ref[rs, :], v2_ref[...]))

        kk = k * k_k
        k2h, k2l = _split2(kk * kk)
        ss = _dg(k2h, hm) + _dg(k2l, hm)
        kk = kk * lax.rsqrt(jnp.maximum(ss, 1e-24))
        kh = k * (1.0 + (a - 1.0) * k_a)
        av = -kk
        bv = kk * a

        rkh, rkl = _split2(r * kh * r_k)
        bonus_ref[rs, :] = (_dg(rkh, hm) + _dg(rkl, hm)) * v
        gs_ref[rs, :] = _silu(g)

        cum = _seg_cumsum(logd, RCHUNK)
        e_c = jnp.exp(cum)
        e_n = jnp.exp(-cum)
        e_p = jnp.where(first_row, 1.0, pltpu.roll(e_c, 1, 0))
        p_end = jnp.exp(cum.reshape(cpb, RCHUNK, LANE)[:, RCHUNK - 1:RCHUNK, :])
        e_l = e_n * jnp.broadcast_to(p_end, (cpb, RCHUNK, LANE)).reshape(blk, LANE)
        at = av * e_p
        rt = r * e_c
        bt = bv * e_n
        kt = kh * e_n
        bh_ref[rs, :] = (bv * e_l).astype(BF16)
        kh2_ref[rs, :] = (kh * e_l).astype(BF16)
        vb_ref[rs, :] = v.astype(BF16)
        c0 = r0 // RCHUNK
        dl_ref[c0 * SUB:(c0 + cpb) * SUB, :] = (
            jnp.broadcast_to(p_end, (cpb, SUB, LANE)).reshape(cpb * SUB, LANE))
        out = []
        for c in range(cpb):
            cr = slice(c * RCHUNK, (c + 1) * RCHUNK)
            out.append((r0 + c * RCHUNK, at[cr], rt[cr], bt[cr], kt[cr], v[cr]))
        return out

    lane = _iota((RCHUNK, LANE), 1)
    si = lane & (RH - 1)
    ti = _iota((RCHUNK, LANE), 0)
    strict = si < ti
    incl = si <= ti
    eye = jnp.where(si == ti, 1.0, 0.0)
    head0 = lane < RH

    def stack(xc):
        xc = xc.astype(BF16)
        zero = jnp.zeros_like(xc)
        return jnp.concatenate([jnp.where(head0, xc, zero), jnp.where(head0, zero, xc)], axis=0)

    def chains(ops):
        chunks = range(len(ops))
        ats = [stack(o[1]) for o in ops]
        vs = [stack(o[5]) for o in ops]
        amat = []
        for (_, at, rt, bt, kt, _) in ops:
            lhs = jnp.concatenate([at, rt], axis=0).astype(BF16)
            rhs = jnp.concatenate([stack(bt), stack(kt)], axis=0)
            amat.append(_dg(lhs, rhs, NT))
        yield
        a_ab = [jnp.where(strict, m[:RCHUNK, :LANE], 0.0) for m in amat]
        a_ak = [jnp.where(strict, m[:RCHUNK, LANE:], 0.0) for m in amat]
        a_rb = [jnp.where(incl, m[RCHUNK:, :LANE], 0.0).astype(BF16) for m in amat]
        a_rk = [jnp.where(incl, m[RCHUNK:, LANE:], 0.0) for m in amat]
        akv2 = [_dg(jnp.concatenate([a_ak[c], a_rk[c]], axis=0).astype(BF16), vs[c]) for c in chunks]
        akv = [m[:RCHUNK] for m in akv2]
        arkv = [m[RCHUNK:] for m in akv2]
        yield
        inv = [eye + m for m in a_ab]
        pw = [_dg(m.astype(BF16), stack(m)) for m in a_ab]
        yield
        s = 2
        while s < RCHUNK // 2:
            both = [_dg(pw[c].astype(BF16), jnp.concatenate([stack(pw[c]), stack(inv[c])], axis=1))
                    for c in chunks]
            pw = [m[:, :LANE] for m in both]
            inv = [inv[c] + both[c][:, LANE:] for c in chunks]
            s *= 2
            yield
        inv = [inv[c] + _dg(pw[c].astype(BF16), stack(inv[c])) for c in chunks]
        yield
        x = [_dg(inv[c].astype(BF16), jnp.concatenate([ats[c], stack(akv[c])], axis=1)) for c in chunks]
        yield
        z = [_dg(a_rb[c], jnp.concatenate([stack(x[c][:, :LANE]), stack(x[c][:, LANE:])], axis=1))
             for c in chunks]
        for c in chunks:
            orow = slice(ops[c][0], ops[c][0] + RCHUNK)
            atp_ref[orow, :] = x[c][:, :LANE].astype(BF16)
            u0_ref[orow, :] = x[c][:, LANE:]
            rp_ref[orow, :] = (ops[c][2] + z[c][:, :LANE]).astype(BF16)
            y0_ref[orow, :] = z[c][:, LANE:] + arkv[c]

    halves = [list(range(h0, h0 + half, blk)) for h0 in range(0, t, half)]
    ops = [o for r0 in halves[0] for o in front(r0)]
    for hi in range(len(halves)):
        pending = list(halves[hi + 1]) if hi + 1 < len(halves) else []
        nxt = []
        for _ in chains(ops):
            if pending:
                nxt += front(pending.pop(0))
        while pending:
            nxt += front(pending.pop(0))
        ops = nxt


def _rwkv_prep(p, ps, pa, w2p, a2p, vmix):
    l = p.shape[0]
    t = min(RT, l)
    npair = RW // LANE
    has_vmix = vmix is not None

    def main(off):
        cb = off // LANE
        return pl.BlockSpec((t, LANE), lambda q, i: (i, cb + q))

    par = pl.BlockSpec((SUB, LANE), lambda q, i: (0, q))
    lora = pl.BlockSpec((LANE, LANE), lambda q, i: (0, q))
    lora_in = pl.BlockSpec((S_BLK, LANE), lambda q, i: (0, q))
    row = pl.BlockSpec((t, LANE), lambda q, i: (i, q))

    in_specs = [main(E_R), main(E_K), main(E_V), main(E_G),
                pl.BlockSpec((t, S_BLK), lambda q, i: (i, S_LORA // S_BLK)), par, lora_in, lora_in]
    args = [p] * 4 + [ps, pa, w2p, a2p]
    if has_vmix:
        tl, v2p, p_first = vmix
        in_specs += [pl.BlockSpec((t, LANE), lambda q, i: (i, 0)), lora, main(E_V)]
        args += [tl, v2p, p_first]
    dl_rows = (t // RCHUNK) * SUB
    out_specs = [row] * 9 + [pl.BlockSpec((dl_rows, LANE), lambda q, i: (i, q))]
    sd = jax.ShapeDtypeStruct
    out_shape = ([sd((l, RW), BF16)] * 5 + [sd((l, RW), F32)] * 4
                 + [sd((l // RCHUNK * SUB, RW), F32)])
    return pl.pallas_call(
        functools.partial(_rwkv_prep_kernel, has_vmix=has_vmix),
        grid=(npair, l // t),
        in_specs=in_specs,
        out_specs=out_specs,
        out_shape=out_shape,
        compiler_params=_params(dimension_semantics=("parallel", "parallel")),
        name="rwkv_prep",
    )(*args)


def _rwkv_scan_kernel(atp_ref, rp_ref, bh_ref, kh_ref, vb_ref, u0_ref, y0_ref,
                      bonus_ref, gs_ref, dl_ref, pa_ref, o_ref, s_ref, y_ref):
    @pl.when(pl.program_id(0) == 0)
    def _():
        s_ref[...] = jnp.zeros_like(s_ref)

    t = atp_ref.shape[0]
    npair = RW // LANE
    bd = _idiv(_iota((LANE, LANE), 0), RH) == _idiv(_iota((LANE, LANE), 1), RH)
    hmean = _head_ones(LANE, RH, 1.0 / RH)

    pairs = range(npair)
    lanes = [slice(q * LANE, (q + 1) * LANE) for q in pairs]
    for c in range(t // RCHUNK):
        rows = slice(c * RCHUNK, (c + 1) * RCHUNK)
        s_old = [s_ref[q] for q in pairs]
        o = [_dg(jnp.concatenate([atp_ref[rows, lanes[q]], rp_ref[rows, lanes[q]]], axis=0),
                 s_old[q].astype(BF16), NT) for q in pairs]
        ds = []
        for q in pairs:
            u = o[q][:RCHUNK] + u0_ref[rows, lanes[q]]
            y_ref[rows, lanes[q]] = o[q][RCHUNK:] + y0_ref[rows, lanes[q]]
            uv = jnp.concatenate([u.astype(BF16), vb_ref[rows, lanes[q]]], axis=0)
            bk = jnp.concatenate([bh_ref[rows, lanes[q]], kh_ref[rows, lanes[q]]], axis=0)
            ds.append(_dg(uv, bk, TN))
        for q in pairs:
            dl = dl_ref[c * SUB:c * SUB + 1, lanes[q]]
            s_ref[q] = s_old[q] * dl + jnp.where(bd, ds[q], 0.0)

    splits = [_split2(y_ref[:, lanes[q]]) for q in pairs]
    means = [_dg(yh, hmean) + _dg(yl, hmean) for (yh, yl) in splits]
    devs = [y_ref[:, lanes[q]] - means[q] for q in pairs]
    varis = [_bdot(devs[q] * devs[q], hmean) for q in pairs]
    for q in pairs:
        sl = lanes[q]
        yn = devs[q] * lax.rsqrt(varis[q] + RWKV_GN_EPS) * pa_ref[5:6, sl] + pa_ref[6:7, sl]
        o_ref[:, sl] = ((yn + bonus_ref[:, sl]) * gs_ref[:, sl]).astype(BF16)


def _rwkv_scan(atp, rp, bh, kh, vb, u0, y0, bonus, gs, dl, pa):
    l = atp.shape[0]
    t = min(RST, l)
    row = pl.BlockSpec((t, RW), lambda i: (i, 0))
    return pl.pallas_call(
        _rwkv_scan_kernel,
        grid=(l // t,),
        in_specs=[row] * 9 + [pl.BlockSpec(((t // RCHUNK) * SUB, RW), lambda i: (i, 0)),
                              pl.BlockSpec((SUB, RW), lambda i: (0, 0))],
        out_specs=row,
        out_shape=jax.ShapeDtypeStruct((l, RW), BF16),
        scratch_shapes=[pltpu.VMEM((RW // LANE, LANE, LANE), F32), pltpu.VMEM((t, RW), F32)],
        compiler_params=_params(dimension_semantics=("arbitrary",)),
        name="rwkv_scan",
    )(atp, rp, bh, kh, vb, u0, y0, bonus, gs, dl, pa)


def _ssd_conv(main_ref, halo_ref, wb_ref, buf_ref, first):
    q = main_ref.shape[0]
    buf_ref[0:SUB, :] = jnp.where(first, 0.0, halo_ref[...])
    buf_ref[SUB:SUB + q, :] = main_ref[...]
    xb = buf_ref[...]
    acc = wb_ref[SCONV:SCONV + 1, :] + xb[SUB:, :] * wb_ref[SCONV - 1:SCONV, :]
    for s in range(1, SCONV):
        acc = acc + pltpu.roll(xb, s, 0)[SUB:, :] * wb_ref[SCONV - 1 - s:SCONV - s, :]
    return _silu(acc)


def _ssd_kernel(z_ref, x_ref, b_ref, c_ref, dt_ref, xh_ref, bhalo_ref, chalo_ref,
                cwx_ref, cwb_ref, cwc_ref, hp_ref, dx_ref, nw_ref,
                o_ref, prev_ref, xbuf_ref, bbuf_ref, cbuf_ref, ybuf_ref):
    first = pl.program_id(0) == 0
    q = z_ref.shape[0]
    groups = range(SGROUPS)

    @pl.when(first)
    def _():
        prev_ref[...] = jnp.zeros_like(prev_ref)

    xs = _ssd_conv(x_ref, xh_ref, cwx_ref, xbuf_ref, first)
    bmb = _ssd_conv(b_ref, bhalo_ref, cwb_ref, bbuf_ref, first).astype(BF16)
    cmb = _ssd_conv(c_ref, chalo_ref, cwc_ref, cbuf_ref, first).astype(BF16)

    dt = _softplus(dt_ref[...] + hp_ref[0:1, :])
    a_neg = -jnp.exp(hp_ref[1:2, :])
    acs = _seg_cumsum(dt * a_neg, q)
    acs_t = acs.T
    acs_last = acs[q - 1:q, :]

    expand = jnp.where(_iota((LANE, SW), 0) == _idiv(_iota((LANE, SW), 1), SHEAD) + S_DT_LANE0,
                       1.0, 0.0).astype(BF16)
    th, tl = _split2(jnp.concatenate([dt, jnp.exp(acs), jnp.exp(acs_last - acs)], axis=0))
    ex = _dg(jnp.concatenate([th, tl], axis=0), expand)
    ex = ex[:3 * q] + ex[3 * q:]
    dt_e, eacs_e, ds_e = ex[:q], ex[q:2 * q], ex[2 * q:]
    xc = xs * dt_e
    xcd = (xc * ds_e).astype(BF16)

    gl = [slice(g * SSTATE, (g + 1) * SSTATE) for g in groups]
    gc = [slice(g * SGW, (g + 1) * SGW) for g in groups]
    scores = [_dg(cmb[:, gl[g]], bmb[:, gl[g]], NT) for g in groups]
    prev = prev_ref[...]
    prevb = prev.astype(BF16)
    y_off = [_dg(cmb[:, gl[g]], prevb[:, gc[g]]) for g in groups]
    states = [_dg(bmb[:, gl[g]], xcd[:, gc[g]], TN) for g in groups]
    for g in groups:
        prev_ref[:, gc[g]] = prev[:, gc[g]] * eacs_e[q - 1:q, gc[g]] + states[g]

    causal = _iota((q, q), 1) <= _iota((q, q), 0)
    for h in range(SW // SHEAD):
        lane = S_DT_LANE0 + h
        hc = slice(h * SHEAD, (h + 1) * SHEAD)
        diff = acs[:, lane:lane + 1] - acs_t[lane:lane + 1, :]
        lm = jnp.exp(jnp.where(causal, diff, -jnp.inf))
        ybuf_ref[:, hc] = _bdot(scores[h // SHG] * lm, xc[:, hc])
    for g in groups:
        y = ybuf_ref[:, gc[g]] + y_off[g] * eacs_e[:, gc[g]] + dx_ref[0:1, gc[g]] * xs[:, gc[g]]
        y = y * _silu(z_ref[:, gc[g]])
        ms = jnp.mean(y * y, axis=-1, keepdims=True)
        o_ref[:, gc[g]] = (y * lax.rsqrt(ms + EPS) * nw_ref[0:1, gc[g]]).astype(BF16)


def _ssd(p, ps, cw8, hp, dx8, nw8):
    l = p.shape[0]
    q = SCHUNK
    hb = q // SUB
    gn = SGROUPS * SSTATE

    def halo_idx(i):
        return jnp.maximum(i * hb - 1, 0)

    zb, xb = T_Z // SW, T_X // SW
    bb, cb, db = T_B // gn, T_C // gn, S_DT // LANE
    in_specs = [
        pl.BlockSpec((q, SW), lambda i: (i, zb)),
        pl.BlockSpec((q, SW), lambda i: (i, xb)),
        pl.BlockSpec((q, gn), lambda i: (i, bb)),
        pl.BlockSpec((q, gn), lambda i: (i, cb)),
        pl.BlockSpec((q, LANE), lambda i: (i, db)),
        pl.BlockSpec((SUB, SW), lambda i: (halo_idx(i), xb)),
        pl.BlockSpec((SUB, gn), lambda i: (halo_idx(i), bb)),
        pl.BlockSpec((SUB, gn), lambda i: (halo_idx(i), cb)),
        pl.BlockSpec((SUB, SW), lambda i: (0, 0)),
        pl.BlockSpec((SUB, gn), lambda i: (0, SW // gn)),
        pl.BlockSpec((SUB, gn), lambda i: (0, SW // gn + 1)),
        pl.BlockSpec((SUB, LANE), lambda i: (0, 0)),
        pl.BlockSpec((SUB, SW), lambda i: (0, 0)),
        pl.BlockSpec((SUB, SW), lambda i: (0, 0)),
    ]
    return pl.pallas_call(
        _ssd_kernel,
        grid=(l // q,),
        in_specs=in_specs,
        out_specs=pl.BlockSpec((q, SW), lambda i: (i, 0)),
        out_shape=jax.ShapeDtypeStruct((l, SW), BF16),
        scratch_shapes=[pltpu.VMEM((SSTATE, SW), F32),
                        pltpu.VMEM((q + SUB, SW), F32),
                        pltpu.VMEM((q + SUB, gn), F32),
                        pltpu.VMEM((q + SUB, gn), F32),
                        pltpu.VMEM((q, SW), F32)],
        compiler_params=_params(dimension_semantics=("arbitrary",)),
        name="ssd",
    )(p, p, p, p, ps, p, p, p, cw8, cw8, cw8, hp, dx8, nw8)


GT = 256


def _gla_kernel(q_ref, k_ref, v_ref, g_ref, gd_ref, gkw_ref, gkb_ref, nw_ref, o_ref, st_ref, oi_ref):
    @pl.when(pl.program_id(0) == 0)
    def _():
        st_ref[...] = jnp.zeros_like(st_ref)

    t = q_ref.shape[0]
    nchunk = t // GCHUNK
    heads = range(GHEADS)
    kl = [slice(h * GHK, (h + 1) * GHK) for h in heads]
    vl = [slice(h * GHV, (h + 1) * GHV) for h in heads]
    gk = _log_sigmoid(_dot3(gd_ref[...], gkw_ref[...]) + gkb_ref[0:1, :]) * (1.0 / 16.0)
    bc = _seg_cumsum(gk, GCHUNK)
    mid = _chunk_row(bc, GCHUNK, GCHUNK // 2)
    last = _chunk_row(bc, GCHUNK, GCHUNK - 1)
    qs = q_ref[...] * (GHK ** -0.5)
    k = k_ref[...]
    vb = v_ref[...].astype(BF16)
    qm = (qs * jnp.exp(bc - mid)).astype(BF16)
    km = (k * jnp.exp(mid - bc)).astype(BF16)
    q_in = (qs * jnp.exp(bc)).astype(BF16)
    k_st = (k * jnp.exp(last - bc)).astype(BF16)
    dlast = jnp.exp(last)

    ri = _iota((t, t), 0)
    ci = _iota((t, t), 1)
    incl = (_idiv(ri, GCHUNK) == _idiv(ci, GCHUNK)) & (ci <= ri)
    attn = [jnp.where(incl, _dg(qm[:, kl[h]], km[:, kl[h]], NT), 0.0).astype(BF16) for h in heads]
    o_intra = [_dg(attn[h], vb[:, vl[h]]) for h in heads]

    for c in range(nchunk):
        rows = slice(c * GCHUNK, (c + 1) * GCHUNK)
        st = [st_ref[h] for h in heads]
        for h in heads:
            oi_ref[rows, vl[h]] = _dg(q_in[rows, kl[h]], st[h].astype(BF16), NT)
        for h in heads:
            st_ref[h] = (st[h] * dlast[c * GCHUNK:c * GCHUNK + 1, kl[h]]
                         + _dg(vb[rows, vl[h]], k_st[rows, kl[h]], TN))
    for h in heads:
        o = o_intra[h] + oi_ref[:, vl[h]]
        ms = jnp.mean(o * o, axis=-1, keepdims=True)
        o = o * lax.rsqrt(ms + EPS) * nw_ref[0:1, :]
        o_ref[:, vl[h]] = (o * _silu(g_ref[:, vl[h]])).astype(BF16)


def _gla(p, pgd, gkwp, gkb8, nw8):
    l = p.shape[0]
    t = min(GT, l)
    return pl.pallas_call(
        _gla_kernel,
        grid=(l // t,),
        in_specs=[pl.BlockSpec((t, GKEY), lambda i: (i, O_Q // GKEY)),
                  pl.BlockSpec((t, GKEY), lambda i: (i, O_K // GKEY)),
                  pl.BlockSpec((t, GVAL), lambda i: (i, O_V // GVAL)),
                  pl.BlockSpec((t, GVAL), lambda i: (i, O_G // GVAL)),
                  pl.BlockSpec((t, LANE), lambda i: (i, 0)),
                  pl.BlockSpec((LANE, GKEY), lambda i: (0, 0)),
                  pl.BlockSpec((SUB, GKEY), lambda i: (0, 0)),
                  pl.BlockSpec((SUB, GHV), lambda i: (0, 0))],
        out_specs=pl.BlockSpec((t, GVAL), lambda i: (i, 0)),
        out_shape=jax.ShapeDtypeStruct((l, GVAL), BF16),
        scratch_shapes=[pltpu.VMEM((GHEADS, GHV, GHK), F32), pltpu.VMEM((t, GVAL), F32)],
        compiler_params=_params(dimension_semantics=("arbitrary",)),
        name="gla",
    )(p, p, p, p, pgd, gkwp, gkb8, nw8)


def _rows8(*rows):
    n = rows[0].shape[0]
    parts = [r.astype(F32)[None, :] for r in rows]
    if len(rows) < SUB:
        parts.append(jnp.zeros((SUB - len(rows), n), F32))
    return jnp.concatenate(parts, axis=0)


def _pad_cols(w, n):
    return jnp.concatenate([w, jnp.zeros((w.shape[0], n - w.shape[1]), w.dtype)], axis=1)


def _pad_rows(w, n):
    return jnp.concatenate([w, jnp.zeros((n - w.shape[0], w.shape[1]), w.dtype)], axis=0)


def _small_col0(j):
    return jnp.where(j == 0, E_WD, S_DT_COL0)


def _lora_rows(w, row0):
    return jnp.concatenate([jnp.zeros((row0, w.shape[1]), w.dtype), w,
                            jnp.zeros((S_BLK - row0 - w.shape[0], w.shape[1]), w.dtype)], axis=0)


def kernel(x, norm_w, final_norm_w, w_in_even, w_out_even, rwkv_mu, rwkv_w0, rwkv_w2, rwkv_a0, rwkv_a2, rwkv_k_k, rwkv_k_a, rwkv_r_k, rwkv_ln_w, rwkv_ln_b, rwkv_v0, rwkv_v1, rwkv_v2, ssm_conv_w, ssm_conv_b, ssm_dt_bias, ssm_A_log, ssm_D, ssm_norm_w, w_in_odd, w_out_odd, gla_gk_w, gla_gk_b, gla_norm_w):
    bsz, l, d = x.shape
    depth = norm_w.shape[0]
    w_out_even_bf = w_out_even.astype(BF16)
    w_out_odd_bf = w_out_odd.astype(BF16)
    wt_even = jnp.swapaxes(w_in_even, 1, 2)
    wt_odd = jnp.swapaxes(w_in_odd, 1, 2)
    outs = []
    for b in range(bsz):
        res = x[b]
        h = _rmsnorm(res, _rows8(norm_w[0]))
        p_first = None
        for layer in range(depth):
            i = layer // 2
            last = layer == depth - 1
            nw_next = _rows8(final_norm_w if last else norm_w[layer + 1])
            if layer % 2 == 0:
                mu = rwkv_mu[i]
                mu_small = jnp.concatenate([mu[E_WD:E_ZXBC], jnp.zeros((S_COLS - (E_ZXBC - E_WD),), F32)])
                p = _inproj(h, wt_even, i, lambda j: j * PTN, E_MAIN // PTN, PTN, mu8=_rows8(mu[:E_MAIN]))
                pt = _inproj(h, wt_even, i, lambda j: E_ZXBC + j * PTN, T_COLS // PTN, PTN)
                ps = _inproj(h, wt_even, i, _small_col0, S_COLS // S_BLK, S_BLK, mu8=_rows8(mu_small))
                v0 = rwkv_v0[i - 1] if i > 0 else jnp.zeros((RW,), F32)
                pa = _rows8(rwkv_w0[i], rwkv_a0[i], rwkv_k_k[i], rwkv_k_a[i],
                            rwkv_r_k[i].reshape(RW), rwkv_ln_w[i], rwkv_ln_b[i], v0)
                w2p = _lora_rows(rwkv_w2[i], 0)
                a2p = _lora_rows(rwkv_a2[i], E_AD - E_WD)
                if i == 0:
                    vmix = None
                    p_first = p
                else:
                    tl = _vlora(p, _pad_cols(rwkv_v1[i - 1], LANE))
                    vmix = (tl, _pad_rows(rwkv_v2[i - 1], LANE), p_first)
                (atp, rp, bh, kh, vb, u0, y0, bonus, gs, dl) = _rwkv_prep(p, ps, pa, w2p, a2p, vmix)
                y_a = _rwkv_scan(atp, rp, bh, kh, vb, u0, y0, bonus, gs, dl, pa)

                cw8 = jnp.concatenate([ssm_conv_w[i], ssm_conv_b[i][None, :],
                                       jnp.zeros((SUB - SCONV - 1, ssm_conv_w.shape[2]), F32)], axis=0)
                lead = jnp.zeros((S_DT_LANE0,), F32)
                hp = _rows8(jnp.concatenate([lead, ssm_dt_bias[i]]), jnp.concatenate([lead, ssm_A_log[i]]))
                dx8 = _rows8(jnp.repeat(ssm_D[i], SHEAD))
                y_b = _ssd(pt, ps, cw8, hp, dx8, _rows8(ssm_norm_w[i]))
                o = _outproj([y_a, y_b], w_out_even_bf, i, res, nw_next, last)
            else:
                p = _inproj(h, wt_odd, i, lambda j: j * PTN, O_MAIN // PTN, PTN)
                pgd = _inproj(h, wt_odd, i, lambda j: O_GD_COL0, 1, LANE)
                gkwp = jnp.concatenate([jnp.zeros((O_GD_LANE0, GKEY), F32), gla_gk_w[i]], axis=0)
                y = _gla(p, pgd, gkwp, _rows8(gla_gk_b[i]), _rows8(gla_norm_w[i]))
                o = _outproj([y], w_out_odd_bf, i, res, nw_next, last)
            if last:
                res = o[0]
            else:
                res, h = o
        outs.append(res)
    return jnp.stack(outs).astype(x.dtype)
```

```python
import functools
import math

import jax
import jax.numpy as jnp
from jax import lax
from jax.experimental import pallas as pl
from jax.experimental.pallas import tpu as pltpu

F32 = jnp.float32
BF16 = jnp.bfloat16

EPS = 1e-5
RW = 2048
RH = 64
RWKV_GN_EPS = 64e-5
RCHUNK = 64
SW = 2048
SHEAD = 64
SGROUPS = 4
SSTATE = 128
SCONV = 4
SCHUNK = 128
SGW = SW // SGROUPS
SHG = SGW // SHEAD
GHEADS = 4
GKEY = 1024
GVAL = 2048
GHK = 256
GHV = 512
GRANK = 16
GCHUNK = 64

LANE = 128
SUB = 8
VMEM_LIMIT = 56 * 1024 * 1024

PTN = 512
E_R, E_K, E_V, E_G = 0, 2048, 4096, 6144
E_MAIN = 8192
E_WD = 8192
E_AD = 8288
E_ZXBC = 8384
E_COLS = 13536
T_Z, T_X = 0, 2048
T_B, T_C = 4096, 4608
T_COLS = 5120
S_BLK = 256
S_LORA = 0
S_COLS = 2 * S_BLK
S_DT = S_COLS - LANE
S_DT_COL0 = E_COLS - S_BLK
S_DT_LANE0 = LANE - SW // SHEAD
O_COLS = 6160
O_GD_COL0 = O_COLS - LANE
O_GD_LANE0 = LANE - GRANK
O_Q, O_K, O_V, O_G = 0, 1024, 2048, 4096
O_MAIN = 6144

NN = (((1,), (0,)), ((), ()))
NT = (((1,), (1,)), ((), ()))
TN = (((0,), (0,)), ((), ()))


def _dg(a, b, dims=NN):
    return lax.dot_general(a, b, dims, preferred_element_type=F32)


def _bdot(a, b, dims=NN):
    return _dg(a.astype(BF16), b.astype(BF16), dims)


def _split2(x):
    hi = x.astype(BF16)
    lo = (x - hi.astype(F32)).astype(BF16)
    return hi, lo


def _dot3(a, b, dims=NN):
    ah, al = _split2(a)
    bh, bl = _split2(b)
    return _dg(ah, bh, dims) + (_dg(ah, bl, dims) + _dg(al, bh, dims))


EXP_NEG_HALF = math.exp(-0.5)


def _sigmoid(x):
    return 0.5 * jnp.tanh(0.5 * x) + 0.5


def _silu(x):
    return x * _sigmoid(x)


def _log_sigmoid(x):
    return jnp.minimum(x, 0.0) - jnp.log(1.0 + jnp.exp(-jnp.abs(x)))


def _softplus(x):
    return jnp.maximum(x, 0.0) + jnp.log(1.0 + jnp.exp(-jnp.abs(x)))


def _iota(shape, axis):
    return lax.broadcasted_iota(jnp.int32, shape, axis)


def _idiv(x, n):
    return lax.shift_right_logical(x, n.bit_length() - 1)


def _seg_cumsum(x, chunk):
    rin = _iota(x.shape, 0) & (chunk - 1)
    s = 1
    while s < chunk:
        x = x + jnp.where(rin >= s, pltpu.roll(x, s, 0), 0.0)
        s *= 2
    return x


def _chunk_row(x, chunk, r):
    t, w = x.shape
    x3 = x.reshape(t // chunk, chunk, w)
    return jnp.broadcast_to(x3[:, r:r + 1, :], x3.shape).reshape(t, w)


def _head_ones(n, head, scale=1.0):
    r = _idiv(_iota((n, n), 0), head)
    c = _idiv(_iota((n, n), 1), head)
    return jnp.where(r == c, scale, 0.0).astype(BF16)


def _params(**kw):
    return pltpu.CompilerParams(vmem_limit_bytes=VMEM_LIMIT, **kw)


def _rmsnorm_kernel(x_ref, w_ref, o_ref):
    x = x_ref[...]
    ms = jnp.mean(x * x, axis=-1, keepdims=True)
    o_ref[...] = (x * lax.rsqrt(ms + EPS) * w_ref[0:1, :]).astype(BF16)


def _rmsnorm(x, w8, tm=512):
    l, d = x.shape
    return pl.pallas_call(
        _rmsnorm_kernel,
        grid=(l // tm,),
        in_specs=[pl.BlockSpec((tm, d), lambda i: (i, 0)),
                  pl.BlockSpec((SUB, d), lambda i: (0, 0))],
        out_specs=pl.BlockSpec((tm, d), lambda i: (i, 0)),
        out_shape=jax.ShapeDtypeStruct((l, d), BF16),
        compiler_params=_params(dimension_semantics=("parallel",)),
        name="rmsnorm",
    )(x, w8)


HALO = 16


def _inproj_kernel(*refs, shift):
    if shift:
        h_ref, hh_ref, w_ref, mu_ref, o_ref = refs
    else:
        h_ref, w_ref, o_ref = refs
    w = w_ref[0].astype(BF16)
    p = _dg(h_ref[...], w, NT)
    if shift:
        prev = _dg(hh_ref[...], w, NT)[HALO - 1:HALO, :]
        prev = jnp.where(pl.program_id(0) == 0, 0.0, prev)
        pprev = jnp.where(_iota(p.shape, 0) == 0, prev, pltpu.roll(p, 1, 0))
        p = p + (pprev - p) * mu_ref[0:1, :]
    o_ref[...] = p


def _inproj(h, wt, layer, col0, nblk, tn, mu8=None, tm=2048):
    l, d = h.shape
    tm = min(tm, l)
    shift = mu8 is not None
    in_specs = [pl.BlockSpec((tm, d), lambda i, j: (i, 0))]
    args = [h]
    if shift:
        hb = tm // HALO
        in_specs.append(pl.BlockSpec((HALO, d), lambda i, j: (jnp.maximum(i * hb - 1, 0), 0)))
        args.append(h)
    def w_map(i, j):
        c = col0(j)
        return (layer, c if isinstance(c, int) else pl.multiple_of(c, SUB), 0)

    in_specs.append(pl.BlockSpec((pl.Element(1), pl.Element(tn), pl.Element(d)), w_map))
    args.append(wt)
    if shift:
        in_specs.append(pl.BlockSpec((SUB, tn), lambda i, j: (0, j)))
        args.append(mu8)
    return pl.pallas_call(
        functools.partial(_inproj_kernel, shift=shift),
        grid=(l // tm, nblk),
        in_specs=in_specs,
        out_specs=pl.BlockSpec((tm, tn), lambda i, j: (i, j)),
        out_shape=jax.ShapeDtypeStruct((l, nblk * tn), F32),
        compiler_params=_params(dimension_semantics=("parallel", "parallel")),
        name="inproj",
    )(*args)


def _outproj_kernel(*refs, n_in, final):
    y_refs = refs[:n_in]
    w_ref, res_ref, nw_ref = refs[n_in:n_in + 3]
    outs = refs[n_in + 3:]
    r = res_ref[...]
    for s in range(n_in):
        kw = y_refs[s].shape[1]
        r = r + jnp.dot(y_refs[s][...], w_ref[s * kw:(s + 1) * kw, :], preferred_element_type=F32)
    ms = jnp.mean(r * r, axis=-1, keepdims=True)
    hn = r * lax.rsqrt(ms + EPS) * nw_ref[0:1, :]
    if final:
        outs[0][...] = hn
    else:
        outs[0][...] = r
        outs[1][...] = hn.astype(BF16)


def _outproj(ys, w, layer, res, nw8, final, tm=512):
    l, d = res.shape
    n_in = len(ys)
    kdim = w.shape[1]
    tm = min(tm, l)
    in_specs = [pl.BlockSpec((tm, y.shape[1]), lambda i: (i, 0)) for y in ys]
    in_specs += [pl.BlockSpec((None, kdim, d), lambda i: (layer, 0, 0), pipeline_mode=pl.Buffered(1)),
                 pl.BlockSpec((tm, d), lambda i: (i, 0)),
                 pl.BlockSpec((SUB, d), lambda i: (0, 0))]
    row_spec = pl.BlockSpec((tm, d), lambda i: (i, 0))
    if final:
        out_specs = [row_spec]
        out_shape = [jax.ShapeDtypeStruct((l, d), F32)]
    else:
        out_specs = [row_spec, row_spec]
        out_shape = [jax.ShapeDtypeStruct((l, d), F32), jax.ShapeDtypeStruct((l, d), BF16)]
    return pl.pallas_call(
        functools.partial(_outproj_kernel, n_in=n_in, final=final),
        grid=(l // tm,),
        in_specs=in_specs,
        out_specs=out_specs,
        out_shape=out_shape,
        compiler_params=_params(dimension_semantics=("parallel",)),
        name="outproj",
    )(*ys, w, res, nw8)


def _vlora_kernel(v_ref, v1_ref, o_ref):
    o_ref[...] = _dot3(v_ref[...], v1_ref[...])


def _vlora(p, v1p, t=512):
    l = p.shape[0]
    t = min(t, l)
    cb = E_V // RW
    return pl.pallas_call(
        _vlora_kernel,
        grid=(l // t,),
        in_specs=[pl.BlockSpec((t, RW), lambda i: (i, cb)),
                  pl.BlockSpec((RW, LANE), lambda i: (0, 0))],
        out_specs=pl.BlockSpec((t, LANE), lambda i: (i, 0)),
        out_shape=jax.ShapeDtypeStruct((l, LANE), F32),
        compiler_params=_params(dimension_semantics=("parallel",)),
        name="rwkv_vlora",
    )(p, v1p)


RT = 1024
RHALF = 1024
REW = 1024
RST = 256


def _rwkv_prep_kernel(*refs, has_vmix):
    (r_ref, k_ref, v_ref, g_ref, lo_ref, pa_ref, w2_ref, a2_ref) = refs[:8]
    pos = 8
    if has_vmix:
        tl_ref, v2_ref, vf_ref = refs[pos:pos + 3]
        pos += 3
    (atp_ref, rp_ref, bh_ref, kh2_ref, vb_ref, u0_ref, y0_ref, bonus_ref, gs_ref, dl_ref) = refs[pos:]
    t = r_ref.shape[0]
    half = min(RHALF, t)
    blk = min(REW, t)
    cpb = blk // RCHUNK

    w0, a0, k_k, k_a = (pa_ref[i:i + 1, :] for i in range(4))
    r_k, v0 = pa_ref[4:5, :], pa_ref[7:8, :]
    hm = _head_ones(LANE, RH)
    first_row = (_iota((blk, LANE), 0) & (RCHUNK - 1)) == 0

    def front(r0):
        rs = slice(r0, r0 + blk)
        r, k, v, g, lo = r_ref[rs, :], k_ref[rs, :], v_ref[rs, :], g_ref[rs, :], lo_ref[rs, :]
        logd = -EXP_NEG_HALF * _sigmoid(w0 + _dot3(jnp.tanh(lo), w2_ref[...]))
        a = _sigmoid(a0 + _dot3(lo, a2_ref[...]))
        if has_vmix:
            v = v + (vf_ref[rs, :] - v) * _sigmoid(v0 + _dot3(tl_ref[rs, :], v2_ref[...]))

        kk = k * k_k
        k2h, k2l = _split2(kk * kk)
        ss = _dg(k2h, hm) + _dg(k2l, hm)
        kk = kk * lax.rsqrt(jnp.maximum(ss, 1e-24))
        kh = k * (1.0 + (a - 1.0) * k_a)
        av = -kk
        bv = kk * a

        rkh, rkl = _split2(r * kh * r_k)
        bonus_ref[rs, :] = ((_dg(rkh, hm) + _dg(rkl, hm)) * v).astype(BF16)
        gs_ref[rs, :] = _silu(g).astype(BF16)

        cum = _seg_cumsum(logd, RCHUNK)
        e_c = jnp.exp(cum)
        e_n = jnp.exp(-cum)
        e_p = jnp.where(first_row, 1.0, pltpu.roll(e_c, 1, 0))
        p_end = jnp.exp(cum.reshape(cpb, RCHUNK, LANE)[:, RCHUNK - 1:RCHUNK, :])
        e_l = e_n * jnp.broadcast_to(p_end, (cpb, RCHUNK, LANE)).reshape(blk, LANE)
        at = av * e_p
        rt = r * e_c
        bt = bv * e_n
        kt = kh * e_n
        bh_ref[rs, :] = (bv * e_l).astype(BF16)
        kh2_ref[rs, :] = (kh * e_l).astype(BF16)
        vb_ref[rs, :] = v.astype(BF16)
        c0 = r0 // RCHUNK
        dl_ref[c0 * SUB:(c0 + cpb) * SUB, :] = (
            jnp.broadcast_to(p_end, (cpb, SUB, LANE)).reshape(cpb * SUB, LANE))
        out = []
        for c in range(cpb):
            cr = slice(c * RCHUNK, (c + 1) * RCHUNK)
            out.append((r0 + c * RCHUNK, at[cr], rt[cr], bt[cr], kt[cr], v[cr]))
        return out

    lane = _iota((RCHUNK, LANE), 1)
    si = lane & (RH - 1)
    ti = _iota((RCHUNK, LANE), 0)
    strict = si < ti
    incl = si <= ti
    eye = jnp.where(si == ti, 1.0, 0.0)
    head0 = lane < RH

    def stack(xc):
        xc = xc.astype(BF16)
        zero = jnp.zeros_like(xc)
        return jnp.concatenate([jnp.where(head0, xc, zero), jnp.where(head0, zero, xc)], axis=0)

    def chains(ops):
        chunks = range(len(ops))
        ats = [stack(o[1]) for o in ops]
        vs = [stack(o[5]) for o in ops]
        amat = []
        for (_, at, rt, bt, kt, _) in ops:
            lhs = jnp.concatenate([at, rt], axis=0).astype(BF16)
            rhs = jnp.concatenate([stack(bt), stack(kt)], axis=0)
            amat.append(_dg(lhs, rhs, NT))
        yield
        a_ab = [jnp.where(strict, m[:RCHUNK, :LANE], 0.0) for m in amat]
        a_ak = [jnp.where(strict, m[:RCHUNK, LANE:], 0.0) for m in amat]
        a_rb = [jnp.where(incl, m[RCHUNK:, :LANE], 0.0).astype(BF16) for m in amat]
        a_rk = [jnp.where(incl, m[RCHUNK:, LANE:], 0.0) for m in amat]
        akv2 = [_dg(jnp.concatenate([a_ak[c], a_rk[c]], axis=0).astype(BF16), vs[c]) for c in chunks]
        akv = [m[:RCHUNK] for m in akv2]
        arkv = [m[RCHUNK:] for m in akv2]
        yield
        inv = [eye + m for m in a_ab]
        pw = [_dg(m.astype(BF16), stack(m)) for m in a_ab]
        yield
        s = 2
        while s < RCHUNK // 2:
            both = [_dg(pw[c].astype(BF16), jnp.concatenate([stack(pw[c]), stack(inv[c])], axis=1))
                    for c in chunks]
            pw = [m[:, :LANE] for m in both]
            inv = [inv[c] + both[c][:, LANE:] for c in chunks]
            s *= 2
            yield
        inv = [inv[c] + _dg(pw[c].astype(BF16), stack(inv[c])) for c in chunks]
        yield
        x = [_dg(inv[c].astype(BF16), jnp.concatenate([ats[c], stack(akv[c])], axis=1)) for c in chunks]
        yield
        z = [_dg(a_rb[c], jnp.concatenate([stack(x[c][:, :LANE]), stack(x[c][:, LANE:])], axis=1))
             for c in chunks]
        for c in chunks:
            orow = slice(ops[c][0], ops[c][0] + RCHUNK)
            atp_ref[orow, :] = x[c][:, :LANE].astype(BF16)
            u0_ref[orow, :] = x[c][:, LANE:].astype(BF16)
            rp_ref[orow, :] = (ops[c][2] + z[c][:, :LANE]).astype(BF16)
            y0_ref[orow, :] = (z[c][:, LANE:] + arkv[c]).astype(BF16)

    halves = [list(range(h0, h0 + half, blk)) for h0 in range(0, t, half)]
    ops = [o for r0 in halves[0] for o in front(r0)]
    for hi in range(len(halves)):
        pending = list(halves[hi + 1]) if hi + 1 < len(halves) else []
        nxt = []
        for _ in chains(ops):
            if pending:
                nxt += front(pending.pop(0))
        while pending:
            nxt += front(pending.pop(0))
        ops = nxt


def _rwkv_prep(p, ps, pa, w2p, a2p, vmix):
    l = p.shape[0]
    t = min(RT, l)
    npair = RW // LANE
    has_vmix = vmix is not None

    def main(off):
        cb = off // LANE
        return pl.BlockSpec((t, LANE), lambda q, i: (i, cb + q))

    par = pl.BlockSpec((SUB, LANE), lambda q, i: (0, q))
    lora = pl.BlockSpec((LANE, LANE), lambda q, i: (0, q))
    lora_in = pl.BlockSpec((S_BLK, LANE), lambda q, i: (0, q))
    row = pl.BlockSpec((t, LANE), lambda q, i: (i, q))

    in_specs = [main(E_R), main(E_K), main(E_V), main(E_G),
                pl.BlockSpec((t, S_BLK), lambda q, i: (i, S_LORA // S_BLK)), par, lora_in, lora_in]
    args = [p] * 4 + [ps, pa, w2p, a2p]
    if has_vmix:
        tl, v2p, p_first = vmix
        in_specs += [pl.BlockSpec((t, LANE), lambda q, i: (i, 0)), lora, main(E_V)]
        args += [tl, v2p, p_first]
    dl_rows = (t // RCHUNK) * SUB
    out_specs = [row] * 9 + [pl.BlockSpec((dl_rows, LANE), lambda q, i: (i, q))]
    sd = jax.ShapeDtypeStruct
    out_shape = ([sd((l, RW), BF16)] * 9
                 + [sd((l // RCHUNK * SUB, RW), F32)])
    return pl.pallas_call(
        functools.partial(_rwkv_prep_kernel, has_vmix=has_vmix),
        grid=(npair, l // t),
        in_specs=in_specs,
        out_specs=out_specs,
        out_shape=out_shape,
        compiler_params=_params(dimension_semantics=("parallel", "parallel")),
        name="rwkv_prep",
    )(*args)


def _rwkv_scan_kernel(atp_ref, rp_ref, bh_ref, kh_ref, vb_ref, u0_ref, y0_ref,
                      bonus_ref, gs_ref, dl_ref, pa_ref, o_ref, s_ref, y_ref):
    @pl.when(pl.program_id(0) == 0)
    def _():
        s_ref[...] = jnp.zeros_like(s_ref)

    t = atp_ref.shape[0]
    npair = RW // LANE
    bd = _idiv(_iota((LANE, LANE), 0), RH) == _idiv(_iota((LANE, LANE), 1), RH)
    hmean = _head_ones(LANE, RH, 1.0 / RH)

    pairs = range(npair)
    lanes = [slice(q * LANE, (q + 1) * LANE) for q in pairs]
    for c in range(t // RCHUNK):
        rows = slice(c * RCHUNK, (c + 1) * RCHUNK)
        s_old = [s_ref[q] for q in pairs]
        o = [_dg(jnp.concatenate([atp_ref[rows, lanes[q]], rp_ref[rows, lanes[q]]], axis=0),
                 s_old[q].astype(BF16), NT) for q in pairs]
        ds = []
        for q in pairs:
            u = o[q][:RCHUNK] + u0_ref[rows, lanes[q]].astype(F32)
            y_ref[rows, lanes[q]] = o[q][RCHUNK:] + y0_ref[rows, lanes[q]].astype(F32)
            uv = jnp.concatenate([u.astype(BF16), vb_ref[rows, lanes[q]]], axis=0)
            bk = jnp.concatenate([bh_ref[rows, lanes[q]], kh_ref[rows, lanes[q]]], axis=0)
            ds.append(_dg(uv, bk, TN))
        for q in pairs:
            dl = dl_ref[c * SUB:c * SUB + 1, lanes[q]]
            s_ref[q] = s_old[q] * dl + jnp.where(bd, ds[q], 0.0)

    splits = [_split2(y_ref[:, lanes[q]]) for q in pairs]
    means = [_dg(yh, hmean) + _dg(yl, hmean) for (yh, yl) in splits]
    devs = [y_ref[:, lanes[q]] - means[q] for q in pairs]
    varis = [_bdot(devs[q] * devs[q], hmean) for q in pairs]
    for q in pairs:
        sl = lanes[q]
        yn = devs[q] * lax.rsqrt(varis[q] + RWKV_GN_EPS) * pa_ref[5:6, sl] + pa_ref[6:7, sl]
        o_ref[:, sl] = ((yn + bonus_ref[:, sl].astype(F32)) * gs_ref[:, sl].astype(F32)).astype(BF16)


def _rwkv_scan(atp, rp, bh, kh, vb, u0, y0, bonus, gs, dl, pa):
    l = atp.shape[0]
    t = min(RST, l)
    row = pl.BlockSpec((t, RW), lambda i: (i, 0))
    return pl.pallas_call(
        _rwkv_scan_kernel,
        grid=(l // t,),
        in_specs=[row] * 9 + [pl.BlockSpec(((t // RCHUNK) * SUB, RW), lambda i: (i, 0)),
                              pl.BlockSpec((SUB, RW), lambda i: (0, 0))],
        out_specs=row,
        out_shape=jax.ShapeDtypeStruct((l, RW), BF16),
        scratch_shapes=[pltpu.VMEM((RW // LANE, LANE, LANE), F32), pltpu.VMEM((t, RW), F32)],
        compiler_params=_params(dimension_semantics=("arbitrary",)),
        name="rwkv_scan",
    )(atp, rp, bh, kh, vb, u0, y0, bonus, gs, dl, pa)


def _ssd_conv(main_ref, halo_ref, wb_ref, buf_ref, first):
    q = main_ref.shape[0]
    buf_ref[0:SUB, :] = jnp.where(first, 0.0, halo_ref[...])
    buf_ref[SUB:SUB + q, :] = main_ref[...]
    xb = buf_ref[...]
    acc = wb_ref[SCONV:SCONV + 1, :] + xb[SUB:, :] * wb_ref[SCONV - 1:SCONV, :]
    for s in range(1, SCONV):
        acc = acc + pltpu.roll(xb, s, 0)[SUB:, :] * wb_ref[SCONV - 1 - s:SCONV - s, :]
    return _silu(acc)


def _ssd_kernel(z_ref, x_ref, b_ref, c_ref, dt_ref, xh_ref, bhalo_ref, chalo_ref,
                cwx_ref, cwb_ref, cwc_ref, hp_ref, dx_ref, nw_ref,
                o_ref, prev_ref, xbuf_ref, bbuf_ref, cbuf_ref, ybuf_ref):
    first = pl.program_id(0) == 0
    q = z_ref.shape[0]
    groups = range(SGROUPS)

    @pl.when(first)
    def _():
        prev_ref[...] = jnp.zeros_like(prev_ref)

    xs = _ssd_conv(x_ref, xh_ref, cwx_ref, xbuf_ref, first)
    bmb = _ssd_conv(b_ref, bhalo_ref, cwb_ref, bbuf_ref, first).astype(BF16)
    cmb = _ssd_conv(c_ref, chalo_ref, cwc_ref, cbuf_ref, first).astype(BF16)

    dt = _softplus(dt_ref[...] + hp_ref[0:1, :])
    a_neg = -jnp.exp(hp_ref[1:2, :])
    acs = _seg_cumsum(dt * a_neg, q)
    acs_t = acs.T
    acs_last = acs[q - 1:q, :]

    expand = jnp.where(_iota((LANE, SW), 0) == _idiv(_iota((LANE, SW), 1), SHEAD) + S_DT_LANE0,
                       1.0, 0.0).astype(BF16)
    th, tl = _split2(jnp.concatenate([dt, jnp.exp(acs), jnp.exp(acs_last - acs)], axis=0))
    ex = _dg(jnp.concatenate([th, tl], axis=0), expand)
    ex = ex[:3 * q] + ex[3 * q:]
    dt_e, eacs_e, ds_e = ex[:q], ex[q:2 * q], ex[2 * q:]
    xc = xs * dt_e
    xcd = (xc * ds_e).astype(BF16)

    gl = [slice(g * SSTATE, (g + 1) * SSTATE) for g in groups]
    gc = [slice(g * SGW, (g + 1) * SGW) for g in groups]
    scores = [_dg(cmb[:, gl[g]], bmb[:, gl[g]], NT) for g in groups]
    prev = prev_ref[...]
    prevb = prev.astype(BF16)
    y_off = [_dg(cmb[:, gl[g]], prevb[:, gc[g]]) for g in groups]
    states = [_dg(bmb[:, gl[g]], xcd[:, gc[g]], TN) for g in groups]
    for g in groups:
        prev_ref[:, gc[g]] = prev[:, gc[g]] * eacs_e[q - 1:q, gc[g]] + states[g]

    causal = _iota((q, q), 1) <= _iota((q, q), 0)
    for h in range(SW // SHEAD):
        lane = S_DT_LANE0 + h
        hc = slice(h * SHEAD, (h + 1) * SHEAD)
        diff = acs[:, lane:lane + 1] - acs_t[lane:lane + 1, :]
        lm = jnp.exp(jnp.where(causal, diff, -jnp.inf))
        ybuf_ref[:, hc] = _bdot(scores[h // SHG] * lm, xc[:, hc])
    for g in groups:
        y = ybuf_ref[:, gc[g]] + y_off[g] * eacs_e[:, gc[g]] + dx_ref[0:1, gc[g]] * xs[:, gc[g]]
        y = y * _silu(z_ref[:, gc[g]])
        ms = jnp.mean(y * y, axis=-1, keepdims=True)
        o_ref[:, gc[g]] = (y * lax.rsqrt(ms + EPS) * nw_ref[0:1, gc[g]]).astype(BF16)


def _ssd(p, ps, cw8, hp, dx8, nw8):
    l = p.shape[0]
    q = SCHUNK
    hb = q // SUB
    gn = SGROUPS * SSTATE

    def halo_idx(i):
        return jnp.maximum(i * hb - 1, 0)

    zb, xb = T_Z // SW, T_X // SW
    bb, cb, db = T_B // gn, T_C // gn, S_DT // LANE
    in_specs = [
        pl.BlockSpec((q, SW), lambda i: (i, zb)),
        pl.BlockSpec((q, SW), lambda i: (i, xb)),
        pl.BlockSpec((q, gn), lambda i: (i, bb)),
        pl.BlockSpec((q, gn), lambda i: (i, cb)),
        pl.BlockSpec((q, LANE), lambda i: (i, db)),
        pl.BlockSpec((SUB, SW), lambda i: (halo_idx(i), xb)),
        pl.BlockSpec((SUB, gn), lambda i: (halo_idx(i), bb)),
        pl.BlockSpec((SUB, gn), lambda i: (halo_idx(i), cb)),
        pl.BlockSpec((SUB, SW), lambda i: (0, 0)),
        pl.BlockSpec((SUB, gn), lambda i: (0, SW // gn)),
        pl.BlockSpec((SUB, gn), lambda i: (0, SW // gn + 1)),
        pl.BlockSpec((SUB, LANE), lambda i: (0, 0)),
        pl.BlockSpec((SUB, SW), lambda i: (0, 0)),
        pl.BlockSpec((SUB, SW), lambda i: (0, 0)),
    ]
    return pl.pallas_call(
        _ssd_kernel,
        grid=(l // q,),
        in_specs=in_specs,
        out_specs=pl.BlockSpec((q, SW), lambda i: (i, 0)),
        out_shape=jax.ShapeDtypeStruct((l, SW), BF16),
        scratch_shapes=[pltpu.VMEM((SSTATE, SW), F32),
                        pltpu.VMEM((q + SUB, SW), F32),
                        pltpu.VMEM((q + SUB, gn), F32),
                        pltpu.VMEM((q + SUB, gn), F32),
                        pltpu.VMEM((q, SW), F32)],
        compiler_params=_params(dimension_semantics=("arbitrary",)),
        name="ssd",
    )(p, p, p, p, ps, p, p, p, cw8, cw8, cw8, hp, dx8, nw8)


GT = 256


def _gla_kernel(q_ref, k_ref, v_ref, g_ref, gd_ref, gkw_ref, gkb_ref, nw_ref, o_ref, st_ref, oi_ref):
    @pl.when(pl.program_id(0) == 0)
    def _():
        st_ref[...] = jnp.zeros_like(st_ref)

    t = q_ref.shape[0]
    nchunk = t // GCHUNK
    heads = range(GHEADS)
    kl = [slice(h * GHK, (h + 1) * GHK) for h in heads]
    vl = [slice(h * GHV, (h + 1) * GHV) for h in heads]
    gk = _log_sigmoid(_dot3(gd_ref[...], gkw_ref[...]) + gkb_ref[0:1, :]) * (1.0 / 16.0)
    bc = _seg_cumsum(gk, GCHUNK)
    mid = _chunk_row(bc, GCHUNK, GCHUNK // 2)
    last = _chunk_row(bc, GCHUNK, GCHUNK - 1)
    qs = q_ref[...] * (GHK ** -0.5)
    k = k_ref[...]
    vb = v_ref[...].astype(BF16)
    qm = (qs * jnp.exp(bc - mid)).astype(BF16)
    km = (k * jnp.exp(mid - bc)).astype(BF16)
    q_in = (qs * jnp.exp(bc)).astype(BF16)
    k_st = (k * jnp.exp(last - bc)).astype(BF16)
    dlast = jnp.exp(last)

    ri = _iota((t, t), 0)
    ci = _iota((t, t), 1)
    incl = (_idiv(ri, GCHUNK) == _idiv(ci, GCHUNK)) & (ci <= ri)
    attn = [jnp.where(incl, _dg(qm[:, kl[h]], km[:, kl[h]], NT), 0.0).astype(BF16) for h in heads]
    o_intra = [_dg(attn[h], vb[:, vl[h]]) for h in heads]

    for c in range(nchunk):
        rows = slice(c * GCHUNK, (c + 1) * GCHUNK)
        st = [st_ref[h] for h in heads]
        for h in heads:
            oi_ref[rows, vl[h]] = _dg(q_in[rows, kl[h]], st[h].astype(BF16), NT)
        for h in heads:
            st_ref[h] = (st[h] * dlast[c * GCHUNK:c * GCHUNK + 1, kl[h]]
                         + _dg(vb[rows, vl[h]], k_st[rows, kl[h]], TN))
    for h in heads:
        o = o_intra[h] + oi_ref[:, vl[h]]
        ms = jnp.mean(o * o, axis=-1, keepdims=True)
        o = o * lax.rsqrt(ms + EPS) * nw_ref[0:1, :]
        o_ref[:, vl[h]] = (o * _silu(g_ref[:, vl[h]])).astype(BF16)


def _gla(p, pgd, gkwp, gkb8, nw8):
    l = p.shape[0]
    t = min(GT, l)
    return pl.pallas_call(
        _gla_kernel,
        grid=(l // t,),
        in_specs=[pl.BlockSpec((t, GKEY), lambda i: (i, O_Q // GKEY)),
                  pl.BlockSpec((t, GKEY), lambda i: (i, O_K // GKEY)),
                  pl.BlockSpec((t, GVAL), lambda i: (i, O_V // GVAL)),
                  pl.BlockSpec((t, GVAL), lambda i: (i, O_G // GVAL)),
                  pl.BlockSpec((t, LANE), lambda i: (i, 0)),
                  pl.BlockSpec((LANE, GKEY), lambda i: (0, 0)),
                  pl.BlockSpec((SUB, GKEY), lambda i: (0, 0)),
                  pl.BlockSpec((SUB, GHV), lambda i: (0, 0))],
        out_specs=pl.BlockSpec((t, GVAL), lambda i: (i, 0)),
        out_shape=jax.ShapeDtypeStruct((l, GVAL), BF16),
        scratch_shapes=[pltpu.VMEM((GHEADS, GHV, GHK), F32), pltpu.VMEM((t, GVAL), F32)],
        compiler_params=_params(dimension_semantics=("arbitrary",)),
        name="gla",
    )(p, p, p, p, pgd, gkwp, gkb8, nw8)


def _rows8(*rows):
    n = rows[0].shape[0]
    parts = [r.astype(F32)[None, :] for r in rows]
    if len(rows) < SUB:
        parts.append(jnp.zeros((SUB - len(rows), n), F32))
    return jnp.concatenate(parts, axis=0)


def _pad_cols(w, n):
    return jnp.concatenate([w, jnp.zeros((w.shape[0], n - w.shape[1]), w.dtype)], axis=1)


def _pad_rows(w, n):
    return jnp.concatenate([w, jnp.zeros((n - w.shape[0], w.shape[1]), w.dtype)], axis=0)


def _small_col0(j):
    return jnp.where(j == 0, E_WD, S_DT_COL0)


def _lora_rows(w, row0):
    return jnp.concatenate([jnp.zeros((row0, w.shape[1]), w.dtype), w,
                            jnp.zeros((S_BLK - row0 - w.shape[0], w.shape[1]), w.dtype)], axis=0)


def kernel(x, norm_w, final_norm_w, w_in_even, w_out_even, rwkv_mu, rwkv_w0, rwkv_w2, rwkv_a0, rwkv_a2, rwkv_k_k, rwkv_k_a, rwkv_r_k, rwkv_ln_w, rwkv_ln_b, rwkv_v0, rwkv_v1, rwkv_v2, ssm_conv_w, ssm_conv_b, ssm_dt_bias, ssm_A_log, ssm_D, ssm_norm_w, w_in_odd, w_out_odd, gla_gk_w, gla_gk_b, gla_norm_w):
    bsz, l, d = x.shape
    depth = norm_w.shape[0]
    w_out_even_bf = w_out_even.astype(BF16)
    w_out_odd_bf = w_out_odd.astype(BF16)
    wt_even = jnp.swapaxes(w_in_even, 1, 2)
    wt_odd = jnp.swapaxes(w_in_odd, 1, 2)
    outs = []
    for b in range(bsz):
        res = x[b]
        h = _rmsnorm(res, _rows8(norm_w[0]))
        p_first = None
        for layer in range(depth):
            i = layer // 2
            last = layer == depth - 1
            nw_next = _rows8(final_norm_w if last else norm_w[layer + 1])
            if layer % 2 == 0:
                mu = rwkv_mu[i]
                mu_small = jnp.concatenate([mu[E_WD:E_ZXBC], jnp.zeros((S_COLS - (E_ZXBC - E_WD),), F32)])
                p = _inproj(h, wt_even, i, lambda j: j * PTN, E_MAIN // PTN, PTN, mu8=_rows8(mu[:E_MAIN]))
                pt = _inproj(h, wt_even, i, lambda j: E_ZXBC + j * PTN, T_COLS // PTN, PTN)
                ps = _inproj(h, wt_even, i, _small_col0, S_COLS // S_BLK, S_BLK, mu8=_rows8(mu_small))
                v0 = rwkv_v0[i - 1] if i > 0 else jnp.zeros((RW,), F32)
                pa = _rows8(rwkv_w0[i], rwkv_a0[i], rwkv_k_k[i], rwkv_k_a[i],
                            rwkv_r_k[i].reshape(RW), rwkv_ln_w[i], rwkv_ln_b[i], v0)
                w2p = _lora_rows(rwkv_w2[i], 0)
                a2p = _lora_rows(rwkv_a2[i], E_AD - E_WD)
                if i == 0:
                    vmix = None
                    p_first = p
                else:
                    tl = _vlora(p, _pad_cols(rwkv_v1[i - 1], LANE))
                    vmix = (tl, _pad_rows(rwkv_v2[i - 1], LANE), p_first)
                (atp, rp, bh, kh, vb, u0, y0, bonus, gs, dl) = _rwkv_prep(p, ps, pa, w2p, a2p, vmix)
                y_a = _rwkv_scan(atp, rp, bh, kh, vb, u0, y0, bonus, gs, dl, pa)

                cw8 = jnp.concatenate([ssm_conv_w[i], ssm_conv_b[i][None, :],
                                       jnp.zeros((SUB - SCONV - 1, ssm_conv_w.shape[2]), F32)], axis=0)
                lead = jnp.zeros((S_DT_LANE0,), F32)
                hp = _rows8(jnp.concatenate([lead, ssm_dt_bias[i]]), jnp.concatenate([lead, ssm_A_log[i]]))
                dx8 = _rows8(jnp.repeat(ssm_D[i], SHEAD))
                y_b = _ssd(pt, ps, cw8, hp, dx8, _rows8(ssm_norm_w[i]))
                o = _outproj([y_a, y_b], w_out_even_bf, i, res, nw_next, last)
            else:
                p = _inproj(h, wt_odd, i, lambda j: j * PTN, O_MAIN // PTN, PTN)
                pgd = _inproj(h, wt_odd, i, lambda j: O_GD_COL0, 1, LANE)
                gkwp = jnp.concatenate([jnp.zeros((O_GD_LANE0, GKEY), F32), gla_gk_w[i]], axis=0)
                y = _gla(p, pgd, gkwp, _rows8(gla_gk_b[i]), _rows8(gla_norm_w[i]))
                o = _outproj([y], w_out_odd_bf, i, res, nw_next, last)
            if last:
                res = o[0]
            else:
                res, h = o
        outs.append(res)
    return jnp.stack(outs).astype(x.dtype)
```

```python
import functools
import math

import jax
import jax.numpy as jnp
from jax import lax
from jax.experimental import pallas as pl
from jax.experimental.pallas import tpu as pltpu

F32 = jnp.float32
BF16 = jnp.bfloat16

EPS = 1e-5
RW = 2048
RH = 64
RWKV_GN_EPS = 64e-5
RCHUNK = 64
SW = 2048
SHEAD = 64
SGROUPS = 4
SSTATE = 128
SCONV = 4
SCHUNK = 128
SGW = SW // SGROUPS
SHG = SGW // SHEAD
GHEADS = 4
GKEY = 1024
GVAL = 2048
GHK = 256
GHV = 512
GRANK = 16
GCHUNK = 64

LANE = 128
SUB = 8
VMEM_LIMIT = 56 * 1024 * 1024

PTN = 512
E_R, E_K, E_V, E_G = 0, 2048, 4096, 6144
E_MAIN = 8192
E_WD = 8192
E_AD = 8288
E_ZXBC = 8384
E_COLS = 13536
T_Z, T_X = 0, 2048
T_B, T_C = 4096, 4608
T_COLS = 5120
S_BLK = 256
S_LORA = 0
S_COLS = 2 * S_BLK
S_DT = S_COLS - LANE
S_DT_COL0 = E_COLS - S_BLK
S_DT_LANE0 = LANE - SW // SHEAD
O_COLS = 6160
O_GD_COL0 = O_COLS - LANE
O_GD_LANE0 = LANE - GRANK
O_Q, O_K, O_V, O_G = 0, 1024, 2048, 4096
O_MAIN = 6144

NN = (((1,), (0,)), ((), ()))
NT = (((1,), (1,)), ((), ()))
TN = (((0,), (0,)), ((), ()))


def _dg(a, b, dims=NN):
    return lax.dot_general(a, b, dims, preferred_element_type=F32)


def _bdot(a, b, dims=NN):
    return _dg(a.astype(BF16), b.astype(BF16), dims)


def _split2(x):
    hi = x.astype(BF16)
    lo = (x - hi.astype(F32)).astype(BF16)
    return hi, lo


def _dot3(a, b, dims=NN):
    ah, al = _split2(a)
    bh, bl = _split2(b)
    return _dg(ah, bh, dims) + (_dg(ah, bl, dims) + _dg(al, bh, dims))


EXP_NEG_HALF = math.exp(-0.5)


def _sigmoid(x):
    return 0.5 * jnp.tanh(0.5 * x) + 0.5


def _silu(x):
    return x * _sigmoid(x)


def _log_sigmoid(x):
    return jnp.minimum(x, 0.0) - jnp.log(1.0 + jnp.exp(-jnp.abs(x)))


def _softplus(x):
    return jnp.maximum(x, 0.0) + jnp.log(1.0 + jnp.exp(-jnp.abs(x)))


def _iota(shape, axis):
    return lax.broadcasted_iota(jnp.int32, shape, axis)


def _idiv(x, n):
    return lax.shift_right_logical(x, n.bit_length() - 1)


def _seg_cumsum(x, chunk):
    rin = _iota(x.shape, 0) & (chunk - 1)
    s = 1
    while s < chunk:
        x = x + jnp.where(rin >= s, pltpu.roll(x, s, 0), 0.0)
        s *= 2
    return x


def _chunk_row(x, chunk, r):
    t, w = x.shape
    x3 = x.reshape(t // chunk, chunk, w)
    return jnp.broadcast_to(x3[:, r:r + 1, :], x3.shape).reshape(t, w)


def _head_ones(n, head, scale=1.0):
    r = _idiv(_iota((n, n), 0), head)
    c = _idiv(_iota((n, n), 1), head)
    return jnp.where(r == c, scale, 0.0).astype(BF16)


def _params(**kw):
    return pltpu.CompilerParams(vmem_limit_bytes=VMEM_LIMIT, **kw)


def _rmsnorm_kernel(x_ref, w_ref, o_ref):
    x = x_ref[...]
    ms = jnp.mean(x * x, axis=-1, keepdims=True)
    o_ref[...] = (x * lax.rsqrt(ms + EPS) * w_ref[0:1, :]).astype(BF16)


def _rmsnorm(x, w8, tm=512):
    l, d = x.shape
    return pl.pallas_call(
        _rmsnorm_kernel,
        grid=(l // tm,),
        in_specs=[pl.BlockSpec((tm, d), lambda i: (i, 0)),
                  pl.BlockSpec((SUB, d), lambda i: (0, 0))],
        out_specs=pl.BlockSpec((tm, d), lambda i: (i, 0)),
        out_shape=jax.ShapeDtypeStruct((l, d), BF16),
        compiler_params=_params(dimension_semantics=("parallel",)),
        name="rmsnorm",
    )(x, w8)


HALO = 16


def _inproj_kernel(*refs, shift):
    if shift:
        h_ref, hh_ref, w_ref, mu_ref, o_ref = refs
    else:
        h_ref, w_ref, o_ref = refs
    w = w_ref[0].astype(BF16)
    p = _dg(h_ref[...], w, NT)
    if shift:
        prev = _dg(hh_ref[...], w, NT)[HALO - 1:HALO, :]
        prev = jnp.where(pl.program_id(0) == 0, 0.0, prev)
        pprev = jnp.where(_iota(p.shape, 0) == 0, prev, pltpu.roll(p, 1, 0))
        p = p + (pprev - p) * mu_ref[0:1, :]
    o_ref[...] = p


def _inproj(h, wt, layer, col0, nblk, tn, mu8=None, tm=2048):
    l, d = h.shape
    tm = min(tm, l)
    shift = mu8 is not None
    in_specs = [pl.BlockSpec((tm, d), lambda i, j: (i, 0))]
    args = [h]
    if shift:
        hb = tm // HALO
        in_specs.append(pl.BlockSpec((HALO, d), lambda i, j: (jnp.maximum(i * hb - 1, 0), 0)))
        args.append(h)
    def w_map(i, j):
        c = col0(j)
        return (layer, c if isinstance(c, int) else pl.multiple_of(c, SUB), 0)

    in_specs.append(pl.BlockSpec((pl.Element(1), pl.Element(tn), pl.Element(d)), w_map))
    args.append(wt)
    if shift:
        in_specs.append(pl.BlockSpec((SUB, tn), lambda i, j: (0, j)))
        args.append(mu8)
    return pl.pallas_call(
        functools.partial(_inproj_kernel, shift=shift),
        grid=(l // tm, nblk),
        in_specs=in_specs,
        out_specs=pl.BlockSpec((tm, tn), lambda i, j: (i, j)),
        out_shape=jax.ShapeDtypeStruct((l, nblk * tn), F32),
        compiler_params=_params(dimension_semantics=("parallel", "parallel")),
        name="inproj",
    )(*args)


def _outproj_kernel(*refs, n_in, final):
    y_refs = refs[:n_in]
    w_ref, res_ref, nw_ref = refs[n_in:n_in + 3]
    outs = refs[n_in + 3:]
    r = res_ref[...]
    for s in range(n_in):
        kw = y_refs[s].shape[1]
        r = r + jnp.dot(y_refs[s][...], w_ref[s * kw:(s + 1) * kw, :], preferred_element_type=F32)
    ms = jnp.mean(r * r, axis=-1, keepdims=True)
    hn = r * lax.rsqrt(ms + EPS) * nw_ref[0:1, :]
    if final:
        outs[0][...] = hn
    else:
        outs[0][...] = r
        outs[1][...] = hn.astype(BF16)


def _outproj(ys, w, layer, res, nw8, final, tm=512):
    l, d = res.shape
    n_in = len(ys)
    kdim = w.shape[1]
    tm = min(tm, l)
    in_specs = [pl.BlockSpec((tm, y.shape[1]), lambda i: (i, 0)) for y in ys]
    in_specs += [pl.BlockSpec((None, kdim, d), lambda i: (layer, 0, 0), pipeline_mode=pl.Buffered(1)),
                 pl.BlockSpec((tm, d), lambda i: (i, 0)),
                 pl.BlockSpec((SUB, d), lambda i: (0, 0))]
    row_spec = pl.BlockSpec((tm, d), lambda i: (i, 0))
    if final:
        out_specs = [row_spec]
        out_shape = [jax.ShapeDtypeStruct((l, d), F32)]
    else:
        out_specs = [row_spec, row_spec]
        out_shape = [jax.ShapeDtypeStruct((l, d), F32), jax.ShapeDtypeStruct((l, d), BF16)]
    return pl.pallas_call(
        functools.partial(_outproj_kernel, n_in=n_in, final=final),
        grid=(l // tm,),
        in_specs=in_specs,
        out_specs=out_specs,
        out_shape=out_shape,
        compiler_params=_params(dimension_semantics=("parallel",)),
        name="outproj",
    )(*ys, w, res, nw8)


def _vlora_kernel(v_ref, v1_ref, o_ref):
    o_ref[...] = _bdot(v_ref[...], v1_ref[...])


def _vlora(p, v1p, t=512):
    l = p.shape[0]
    t = min(t, l)
    cb = E_V // RW
    return pl.pallas_call(
        _vlora_kernel,
        grid=(l // t,),
        in_specs=[pl.BlockSpec((t, RW), lambda i: (i, cb)),
                  pl.BlockSpec((RW, LANE), lambda i: (0, 0))],
        out_specs=pl.BlockSpec((t, LANE), lambda i: (i, 0)),
        out_shape=jax.ShapeDtypeStruct((l, LANE), F32),
        compiler_params=_params(dimension_semantics=("parallel",)),
        name="rwkv_vlora",
    )(p, v1p)


RT = 2048
RHALF = 2048
REW = 2048
RST = 256


def _rwkv_prep_kernel(*refs, has_vmix):
    (r_ref, k_ref, v_ref, g_ref, lo_ref, pa_ref, w2_ref, a2_ref) = refs[:8]
    pos = 8
    if has_vmix:
        tl_ref, v2_ref, vf_ref = refs[pos:pos + 3]
        pos += 3
    (atp_ref, rp_ref, bh_ref, kh2_ref, vb_ref, u0_ref, y0_ref, bonus_ref, gs_ref, dl_ref) = refs[pos:]
    t = r_ref.shape[0]
    half = min(RHALF, t)
    blk = min(REW, t)
    cpb = blk // RCHUNK

    w0, a0, k_k, k_a = (pa_ref[i:i + 1, :] for i in range(4))
    r_k, v0 = pa_ref[4:5, :], pa_ref[7:8, :]
    hm = _head_ones(LANE, RH)
    first_row = (_iota((blk, LANE), 0) & (RCHUNK - 1)) == 0

    def front(r0):
        rs = slice(r0, r0 + blk)
        r, k, v, g, lo = r_ref[rs, :], k_ref[rs, :], v_ref[rs, :], g_ref[rs, :], lo_ref[rs, :]
        logd = -EXP_NEG_HALF * _sigmoid(w0 + _dot3(jnp.tanh(lo), w2_ref[...]))
        a = _sigmoid(a0 + _bdot(lo, a2_ref[...]))
        if has_vmix:
            v = v + (vf_ref[rs, :] - v) * _sigmoid(v0 + _bdot(tl_ref[rs, :], v2_ref[...]))

        kk = k * k_k
        k2h, k2l = _split2(kk * kk)
        ss = _dg(k2h, hm) + _dg(k2l, hm)
        kk = kk * lax.rsqrt(jnp.maximum(ss, 1e-24))
        kh = k * (1.0 + (a - 1.0) * k_a)
        av = -kk
        bv = kk * a

        rkh, rkl = _split2(r * kh * r_k)
        bonus_ref[rs, :] = ((_dg(rkh, hm) + _dg(rkl, hm)) * v).astype(BF16)
        gs_ref[rs, :] = _silu(g).astype(BF16)

        cum = _seg_cumsum(logd, RCHUNK)
        e_c = jnp.exp(cum)
        e_n = jnp.exp(-cum)
        e_p = jnp.where(first_row, 1.0, pltpu.roll(e_c, 1, 0))
        p_end = jnp.exp(cum.reshape(cpb, RCHUNK, LANE)[:, RCHUNK - 1:RCHUNK, :])
        e_l = e_n * jnp.broadcast_to(p_end, (cpb, RCHUNK, LANE)).reshape(blk, LANE)
        at = av * e_p
        rt = r * e_c
        bt = bv * e_n
        kt = kh * e_n
        bh_ref[rs, :] = (bv * e_l).astype(BF16)
        kh2_ref[rs, :] = (kh * e_l).astype(BF16)
        vb_ref[rs, :] = v.astype(BF16)
        c0 = r0 // RCHUNK
        dl_ref[c0 * SUB:(c0 + cpb) * SUB, :] = (
            jnp.broadcast_to(p_end, (cpb, SUB, LANE)).reshape(cpb * SUB, LANE))
        out = []
        for c in range(cpb):
            cr = slice(c * RCHUNK, (c + 1) * RCHUNK)
            out.append((r0 + c * RCHUNK, at[cr], rt[cr], bt[cr], kt[cr], v[cr]))
        return out

    lane = _iota((RCHUNK, LANE), 1)
    si = lane & (RH - 1)
    ti = _iota((RCHUNK, LANE), 0)
    strict = si < ti
    incl = si <= ti
    eye = jnp.where(si == ti, 1.0, 0.0)
    head0 = lane < RH

    def stack(xc):
        xc = xc.astype(BF16)
        zero = jnp.zeros_like(xc)
        return jnp.concatenate([jnp.where(head0, xc, zero), jnp.where(head0, zero, xc)], axis=0)

    def chains(ops):
        chunks = range(len(ops))
        ats = [stack(o[1]) for o in ops]
        vs = [stack(o[5]) for o in ops]
        amat = []
        for (_, at, rt, bt, kt, _) in ops:
            lhs = jnp.concatenate([at, rt], axis=0).astype(BF16)
            rhs = jnp.concatenate([stack(bt), stack(kt)], axis=0)
            amat.append(_dg(lhs, rhs, NT))
        yield
        a_ab = [jnp.where(strict, m[:RCHUNK, :LANE], 0.0) for m in amat]
        a_ak = [jnp.where(strict, m[:RCHUNK, LANE:], 0.0) for m in amat]
        a_rb = [jnp.where(incl, m[RCHUNK:, :LANE], 0.0).astype(BF16) for m in amat]
        a_rk = [jnp.where(incl, m[RCHUNK:, LANE:], 0.0) for m in amat]
        akv2 = [_dg(jnp.concatenate([a_ak[c], a_rk[c]], axis=0).astype(BF16), vs[c]) for c in chunks]
        akv = [m[:RCHUNK] for m in akv2]
        arkv = [m[RCHUNK:] for m in akv2]
        yield
        inv = [eye + m for m in a_ab]
        pw = [_dg(m.astype(BF16), stack(m)) for m in a_ab]
        yield
        s = 2
        while s < RCHUNK // 2:
            both = [_dg(pw[c].astype(BF16), jnp.concatenate([stack(pw[c]), stack(inv[c])], axis=1))
                    for c in chunks]
            pw = [m[:, :LANE] for m in both]
            inv = [inv[c] + both[c][:, LANE:] for c in chunks]
            s *= 2
            yield
        inv = [inv[c] + _dg(pw[c].astype(BF16), stack(inv[c])) for c in chunks]
        yield
        x = [_dg(inv[c].astype(BF16), jnp.concatenate([ats[c], stack(akv[c])], axis=1)) for c in chunks]
        yield
        z = [_dg(a_rb[c], jnp.concatenate([stack(x[c][:, :LANE]), stack(x[c][:, LANE:])], axis=1))
             for c in chunks]
        for c in chunks:
            orow = slice(ops[c][0], ops[c][0] + RCHUNK)
            atp_ref[orow, :] = x[c][:, :LANE].astype(BF16)
            u0_ref[orow, :] = x[c][:, LANE:].astype(BF16)
            rp_ref[orow, :] = (ops[c][2] + z[c][:, :LANE]).astype(BF16)
            y0_ref[orow, :] = (z[c][:, LANE:] + arkv[c]).astype(BF16)

    halves = [list(range(h0, h0 + half, blk)) for h0 in range(0, t, half)]
    ops = [o for r0 in halves[0] for o in front(r0)]
    for hi in range(len(halves)):
        pending = list(halves[hi + 1]) if hi + 1 < len(halves) else []
        nxt = []
        for _ in chains(ops):
            if pending:
                nxt += front(pending.pop(0))
        while pending:
            nxt += front(pending.pop(0))
        ops = nxt


def _rwkv_prep(p, ps, pa, w2p, a2p, vmix):
    l = p.shape[0]
    t = min(RT, l)
    npair = RW // LANE
    has_vmix = vmix is not None

    def main(off):
        cb = off // LANE
        return pl.BlockSpec((t, LANE), lambda q, i: (i, cb + q))

    par = pl.BlockSpec((SUB, LANE), lambda q, i: (0, q))
    lora = pl.BlockSpec((LANE, LANE), lambda q, i: (0, q))
    lora_in = pl.BlockSpec((S_BLK, LANE), lambda q, i: (0, q))
    row = pl.BlockSpec((t, LANE), lambda q, i: (i, q))

    in_specs = [main(E_R), main(E_K), main(E_V), main(E_G),
                pl.BlockSpec((t, S_BLK), lambda q, i: (i, S_LORA // S_BLK)), par, lora_in, lora_in]
    args = [p] * 4 + [ps, pa, w2p, a2p]
    if has_vmix:
        tl, v2p, p_first = vmix
        in_specs += [pl.BlockSpec((t, LANE), lambda q, i: (i, 0)), lora, main(E_V)]
        args += [tl, v2p, p_first]
    dl_rows = (t // RCHUNK) * SUB
    out_specs = [row] * 9 + [pl.BlockSpec((dl_rows, LANE), lambda q, i: (i, q))]
    sd = jax.ShapeDtypeStruct
    out_shape = ([sd((l, RW), BF16)] * 9
                 + [sd((l // RCHUNK * SUB, RW), F32)])
    return pl.pallas_call(
        functools.partial(_rwkv_prep_kernel, has_vmix=has_vmix),
        grid=(npair, l // t),
        in_specs=in_specs,
        out_specs=out_specs,
        out_shape=out_shape,
        compiler_params=_params(dimension_semantics=("parallel", "parallel")),
        name="rwkv_prep",
    )(*args)


def _rwkv_scan_kernel(atp_ref, rp_ref, bh_ref, kh_ref, vb_ref, u0_ref, y0_ref,
                      bonus_ref, gs_ref, dl_ref, pa_ref, o_ref, s_ref, y_ref):
    @pl.when(pl.program_id(0) == 0)
    def _():
        s_ref[...] = jnp.zeros_like(s_ref)

    t = atp_ref.shape[0]
    npair = RW // LANE
    bd = _idiv(_iota((LANE, LANE), 0), RH) == _idiv(_iota((LANE, LANE), 1), RH)
    hmean = _head_ones(LANE, RH, 1.0 / RH)

    pairs = range(npair)
    lanes = [slice(q * LANE, (q + 1) * LANE) for q in pairs]
    for c in range(t // RCHUNK):
        rows = slice(c * RCHUNK, (c + 1) * RCHUNK)
        s_old = [s_ref[q] for q in pairs]
        o = [_dg(jnp.concatenate([atp_ref[rows, lanes[q]], rp_ref[rows, lanes[q]]], axis=0),
                 s_old[q].astype(BF16), NT) for q in pairs]
        ds = []
        for q in pairs:
            u = o[q][:RCHUNK] + u0_ref[rows, lanes[q]].astype(F32)
            y_ref[rows, lanes[q]] = o[q][RCHUNK:] + y0_ref[rows, lanes[q]].astype(F32)
            uv = jnp.concatenate([u.astype(BF16), vb_ref[rows, lanes[q]]], axis=0)
            bk = jnp.concatenate([bh_ref[rows, lanes[q]], kh_ref[rows, lanes[q]]], axis=0)
            ds.append(_dg(uv, bk, TN))
        for q in pairs:
            dl = dl_ref[c * SUB:c * SUB + 1, lanes[q]]
            s_ref[q] = s_old[q] * dl + jnp.where(bd, ds[q], 0.0)

    splits = [_split2(y_ref[:, lanes[q]]) for q in pairs]
    means = [_dg(yh, hmean) + _dg(yl, hmean) for (yh, yl) in splits]
    devs = [y_ref[:, lanes[q]] - means[q] for q in pairs]
    varis = [_bdot(devs[q] * devs[q], hmean) for q in pairs]
    for q in pairs:
        sl = lanes[q]
        yn = devs[q] * lax.rsqrt(varis[q] + RWKV_GN_EPS) * pa_ref[5:6, sl] + pa_ref[6:7, sl]
        o_ref[:, sl] = ((yn + bonus_ref[:, sl].astype(F32)) * gs_ref[:, sl].astype(F32)).astype(BF16)


def _rwkv_scan(atp, rp, bh, kh, vb, u0, y0, bonus, gs, dl, pa):
    l = atp.shape[0]
    t = min(RST, l)
    row = pl.BlockSpec((t, RW), lambda i: (i, 0))
    return pl.pallas_call(
        _rwkv_scan_kernel,
        grid=(l // t,),
        in_specs=[row] * 9 + [pl.BlockSpec(((t // RCHUNK) * SUB, RW), lambda i: (i, 0)),
                              pl.BlockSpec((SUB, RW), lambda i: (0, 0))],
        out_specs=row,
        out_shape=jax.ShapeDtypeStruct((l, RW), BF16),
        scratch_shapes=[pltpu.VMEM((RW // LANE, LANE, LANE), F32), pltpu.VMEM((t, RW), F32)],
        compiler_params=_params(dimension_semantics=("arbitrary",)),
        name="rwkv_scan",
    )(atp, rp, bh, kh, vb, u0, y0, bonus, gs, dl, pa)


def _ssd_conv(main_ref, halo_ref, wb_ref, buf_ref, first):
    q = main_ref.shape[0]
    buf_ref[0:SUB, :] = jnp.where(first, 0.0, halo_ref[...])
    buf_ref[SUB:SUB + q, :] = main_ref[...]
    xb = buf_ref[...]
    acc = wb_ref[SCONV:SCONV + 1, :] + xb[SUB:, :] * wb_ref[SCONV - 1:SCONV, :]
    for s in range(1, SCONV):
        acc = acc + pltpu.roll(xb, s, 0)[SUB:, :] * wb_ref[SCONV - 1 - s:SCONV - s, :]
    return _silu(acc)


def _ssd_kernel(z_ref, x_ref, b_ref, c_ref, dt_ref, xh_ref, bhalo_ref, chalo_ref,
                cwx_ref, cwb_ref, cwc_ref, hp_ref, dx_ref, nw_ref,
                o_ref, prev_ref, xbuf_ref, bbuf_ref, cbuf_ref, ybuf_ref):
    first = pl.program_id(0) == 0
    q = z_ref.shape[0]
    groups = range(SGROUPS)

    @pl.when(first)
    def _():
        prev_ref[...] = jnp.zeros_like(prev_ref)

    xs = _ssd_conv(x_ref, xh_ref, cwx_ref, xbuf_ref, first)
    bmb = _ssd_conv(b_ref, bhalo_ref, cwb_ref, bbuf_ref, first).astype(BF16)
    cmb = _ssd_conv(c_ref, chalo_ref, cwc_ref, cbuf_ref, first).astype(BF16)

    dt = _softplus(dt_ref[...] + hp_ref[0:1, :])
    a_neg = -jnp.exp(hp_ref[1:2, :])
    acs = _seg_cumsum(dt * a_neg, q)
    acs_t = acs.T
    acs_last = acs[q - 1:q, :]

    expand = jnp.where(_iota((LANE, SW), 0) == _idiv(_iota((LANE, SW), 1), SHEAD) + S_DT_LANE0,
                       1.0, 0.0).astype(BF16)
    th, tl = _split2(jnp.concatenate([dt, jnp.exp(acs), jnp.exp(acs_last - acs)], axis=0))
    ex = _dg(jnp.concatenate([th, tl], axis=0), expand)
    ex = ex[:3 * q] + ex[3 * q:]
    dt_e, eacs_e, ds_e = ex[:q], ex[q:2 * q], ex[2 * q:]
    xc = xs * dt_e
    xcd = (xc * ds_e).astype(BF16)

    gl = [slice(g * SSTATE, (g + 1) * SSTATE) for g in groups]
    gc = [slice(g * SGW, (g + 1) * SGW) for g in groups]
    scores = [_dg(cmb[:, gl[g]], bmb[:, gl[g]], NT) for g in groups]
    prev = prev_ref[...]
    prevb = prev.astype(BF16)
    y_off = [_dg(cmb[:, gl[g]], prevb[:, gc[g]]) for g in groups]
    states = [_dg(bmb[:, gl[g]], xcd[:, gc[g]], TN) for g in groups]
    for g in groups:
        prev_ref[:, gc[g]] = prev[:, gc[g]] * eacs_e[q - 1:q, gc[g]] + states[g]

    causal = _iota((q, q), 1) <= _iota((q, q), 0)
    for h in range(SW // SHEAD):
        lane = S_DT_LANE0 + h
        hc = slice(h * SHEAD, (h + 1) * SHEAD)
        diff = acs[:, lane:lane + 1] - acs_t[lane:lane + 1, :]
        lm = jnp.exp(jnp.where(causal, diff, -jnp.inf))
        ybuf_ref[:, hc] = _bdot(scores[h // SHG] * lm, xc[:, hc])
    for g in groups:
        y = ybuf_ref[:, gc[g]] + y_off[g] * eacs_e[:, gc[g]] + dx_ref[0:1, gc[g]] * xs[:, gc[g]]
        y = y * _silu(z_ref[:, gc[g]])
        ms = jnp.mean(y * y, axis=-1, keepdims=True)
        o_ref[:, gc[g]] = (y * lax.rsqrt(ms + EPS) * nw_ref[0:1, gc[g]]).astype(BF16)


def _ssd(p, ps, cw8, hp, dx8, nw8):
    l = p.shape[0]
    q = SCHUNK
    hb = q // SUB
    gn = SGROUPS * SSTATE

    def halo_idx(i):
        return jnp.maximum(i * hb - 1, 0)

    zb, xb = T_Z // SW, T_X // SW
    bb, cb, db = T_B // gn, T_C // gn, S_DT // LANE
    in_specs = [
        pl.BlockSpec((q, SW), lambda i: (i, zb)),
        pl.BlockSpec((q, SW), lambda i: (i, xb)),
        pl.BlockSpec((q, gn), lambda i: (i, bb)),
        pl.BlockSpec((q, gn), lambda i: (i, cb)),
        pl.BlockSpec((q, LANE), lambda i: (i, db)),
        pl.BlockSpec((SUB, SW), lambda i: (halo_idx(i), xb)),
        pl.BlockSpec((SUB, gn), lambda i: (halo_idx(i), bb)),
        pl.BlockSpec((SUB, gn), lambda i: (halo_idx(i), cb)),
        pl.BlockSpec((SUB, SW), lambda i: (0, 0)),
        pl.BlockSpec((SUB, gn), lambda i: (0, SW // gn)),
        pl.BlockSpec((SUB, gn), lambda i: (0, SW // gn + 1)),
        pl.BlockSpec((SUB, LANE), lambda i: (0, 0)),
        pl.BlockSpec((SUB, SW), lambda i: (0, 0)),
        pl.BlockSpec((SUB, SW), lambda i: (0, 0)),
    ]
    return pl.pallas_call(
        _ssd_kernel,
        grid=(l // q,),
        in_specs=in_specs,
        out_specs=pl.BlockSpec((q, SW), lambda i: (i, 0)),
        out_shape=jax.ShapeDtypeStruct((l, SW), BF16),
        scratch_shapes=[pltpu.VMEM((SSTATE, SW), F32),
                        pltpu.VMEM((q + SUB, SW), F32),
                        pltpu.VMEM((q + SUB, gn), F32),
                        pltpu.VMEM((q + SUB, gn), F32),
                        pltpu.VMEM((q, SW), F32)],
        compiler_params=_params(dimension_semantics=("arbitrary",)),
        name="ssd",
    )(p, p, p, p, ps, p, p, p, cw8, cw8, cw8, hp, dx8, nw8)


GT = 256


def _gla_kernel(q_ref, k_ref, v_ref, g_ref, gd_ref, gkw_ref, gkb_ref, nw_ref, o_ref, st_ref, oi_ref):
    @pl.when(pl.program_id(0) == 0)
    def _():
        st_ref[...] = jnp.zeros_like(st_ref)

    t = q_ref.shape[0]
    nchunk = t // GCHUNK
    heads = range(GHEADS)
    kl = [slice(h * GHK, (h + 1) * GHK) for h in heads]
    vl = [slice(h * GHV, (h + 1) * GHV) for h in heads]
    gk = _log_sigmoid(_dot3(gd_ref[...], gkw_ref[...]) + gkb_ref[0:1, :]) * (1.0 / 16.0)
    bc = _seg_cumsum(gk, GCHUNK)
    mid = _chunk_row(bc, GCHUNK, GCHUNK // 2)
    last = _chunk_row(bc, GCHUNK, GCHUNK - 1)
    qs = q_ref[...] * (GHK ** -0.5)
    k = k_ref[...]
    vb = v_ref[...].astype(BF16)
    qm = (qs * jnp.exp(bc - mid)).astype(BF16)
    km = (k * jnp.exp(mid - bc)).astype(BF16)
    q_in = (qs * jnp.exp(bc)).astype(BF16)
    k_st = (k * jnp.exp(last - bc)).astype(BF16)
    dlast = jnp.exp(last)

    ri = _iota((t, t), 0)
    ci = _iota((t, t), 1)
    incl = (_idiv(ri, GCHUNK) == _idiv(ci, GCHUNK)) & (ci <= ri)
    attn = [jnp.where(incl, _dg(qm[:, kl[h]], km[:, kl[h]], NT), 0.0).astype(BF16) for h in heads]
    o_intra = [_dg(attn[h], vb[:, vl[h]]) for h in heads]

    for c in range(nchunk):
        rows = slice(c * GCHUNK, (c + 1) * GCHUNK)
        st = [st_ref[h] for h in heads]
        for h in heads:
            oi_ref[rows, vl[h]] = _dg(q_in[rows, kl[h]], st[h].astype(BF16), NT)
        for h in heads:
            st_ref[h] = (st[h] * dlast[c * GCHUNK:c * GCHUNK + 1, kl[h]]
                         + _dg(vb[rows, vl[h]], k_st[rows, kl[h]], TN))
    for h in heads:
        o = o_intra[h] + oi_ref[:, vl[h]]
        ms = jnp.mean(o * o, axis=-1, keepdims=True)
        o = o * lax.rsqrt(ms + EPS) * nw_ref[0:1, :]
        o_ref[:, vl[h]] = (o * _silu(g_ref[:, vl[h]])).astype(BF16)


def _gla(p, pgd, gkwp, gkb8, nw8):
    l = p.shape[0]
    t = min(GT, l)
    return pl.pallas_call(
        _gla_kernel,
        grid=(l // t,),
        in_specs=[pl.BlockSpec((t, GKEY), lambda i: (i, O_Q // GKEY)),
                  pl.BlockSpec((t, GKEY), lambda i: (i, O_K // GKEY)),
                  pl.BlockSpec((t, GVAL), lambda i: (i, O_V // GVAL)),
                  pl.BlockSpec((t, GVAL), lambda i: (i, O_G // GVAL)),
                  pl.BlockSpec((t, LANE), lambda i: (i, 0)),
                  pl.BlockSpec((LANE, GKEY), lambda i: (0, 0)),
                  pl.BlockSpec((SUB, GKEY), lambda i: (0, 0)),
                  pl.BlockSpec((SUB, GHV), lambda i: (0, 0))],
        out_specs=pl.BlockSpec((t, GVAL), lambda i: (i, 0)),
        out_shape=jax.ShapeDtypeStruct((l, GVAL), BF16),
        scratch_shapes=[pltpu.VMEM((GHEADS, GHV, GHK), F32), pltpu.VMEM((t, GVAL), F32)],
        compiler_params=_params(dimension_semantics=("arbitrary",)),
        name="gla",
    )(p, p, p, p, pgd, gkwp, gkb8, nw8)


def _rows8(*rows):
    n = rows[0].shape[0]
    parts = [r.astype(F32)[None, :] for r in rows]
    if len(rows) < SUB:
        parts.append(jnp.zeros((SUB - len(rows), n), F32))
    return jnp.concatenate(parts, axis=0)


def _pad_cols(w, n):
    return jnp.concatenate([w, jnp.zeros((w.shape[0], n - w.shape[1]), w.dtype)], axis=1)


def _pad_rows(w, n):
    return jnp.concatenate([w, jnp.zeros((n - w.shape[0], w.shape[1]), w.dtype)], axis=0)


def _small_col0(j):
    return jnp.where(j == 0, E_WD, S_DT_COL0)


def _lora_rows(w, row0):
    return jnp.concatenate([jnp.zeros((row0, w.shape[1]), w.dtype), w,
                            jnp.zeros((S_BLK - row0 - w.shape[0], w.shape[1]), w.dtype)], axis=0)


def kernel(x, norm_w, final_norm_w, w_in_even, w_out_even, rwkv_mu, rwkv_w0, rwkv_w2, rwkv_a0, rwkv_a2, rwkv_k_k, rwkv_k_a, rwkv_r_k, rwkv_ln_w, rwkv_ln_b, rwkv_v0, rwkv_v1, rwkv_v2, ssm_conv_w, ssm_conv_b, ssm_dt_bias, ssm_A_log, ssm_D, ssm_norm_w, w_in_odd, w_out_odd, gla_gk_w, gla_gk_b, gla_norm_w):
    bsz, l, d = x.shape
    depth = norm_w.shape[0]
    w_out_even_bf = w_out_even.astype(BF16)
    w_out_odd_bf = w_out_odd.astype(BF16)
    wt_even = jnp.swapaxes(w_in_even, 1, 2)
    wt_odd = jnp.swapaxes(w_in_odd, 1, 2)
    outs = []
    for b in range(bsz):
        res = x[b]
        h = _rmsnorm(res, _rows8(norm_w[0]))
        p_first = None
        for layer in range(depth):
            i = layer // 2
            last = layer == depth - 1
            nw_next = _rows8(final_norm_w if last else norm_w[layer + 1])
            if layer % 2 == 0:
                mu = rwkv_mu[i]
                mu_small = jnp.concatenate([mu[E_WD:E_ZXBC], jnp.zeros((S_COLS - (E_ZXBC - E_WD),), F32)])
                p = _inproj(h, wt_even, i, lambda j: j * PTN, E_MAIN // PTN, PTN, mu8=_rows8(mu[:E_MAIN]))
                pt = _inproj(h, wt_even, i, lambda j: E_ZXBC + j * PTN, T_COLS // PTN, PTN)
                ps = _inproj(h, wt_even, i, _small_col0, S_COLS // S_BLK, S_BLK, mu8=_rows8(mu_small))
                v0 = rwkv_v0[i - 1] if i > 0 else jnp.zeros((RW,), F32)
                pa = _rows8(rwkv_w0[i], rwkv_a0[i], rwkv_k_k[i], rwkv_k_a[i],
                            rwkv_r_k[i].reshape(RW), rwkv_ln_w[i], rwkv_ln_b[i], v0)
                w2p = _lora_rows(rwkv_w2[i], 0)
                a2p = _lora_rows(rwkv_a2[i], E_AD - E_WD)
                if i == 0:
                    vmix = None
                    p_first = p
                else:
                    tl = _vlora(p, _pad_cols(rwkv_v1[i - 1], LANE))
                    vmix = (tl, _pad_rows(rwkv_v2[i - 1], LANE), p_first)
                (atp, rp, bh, kh, vb, u0, y0, bonus, gs, dl) = _rwkv_prep(p, ps, pa, w2p, a2p, vmix)
                y_a = _rwkv_scan(atp, rp, bh, kh, vb, u0, y0, bonus, gs, dl, pa)

                cw8 = jnp.concatenate([ssm_conv_w[i], ssm_conv_b[i][None, :],
                                       jnp.zeros((SUB - SCONV - 1, ssm_conv_w.shape[2]), F32)], axis=0)
                lead = jnp.zeros((S_DT_LANE0,), F32)
                hp = _rows8(jnp.concatenate([lead, ssm_dt_bias[i]]), jnp.concatenate([lead, ssm_A_log[i]]))
                dx8 = _rows8(jnp.repeat(ssm_D[i], SHEAD))
                y_b = _ssd(pt, ps, cw8, hp, dx8, _rows8(ssm_norm_w[i]))
                o = _outproj([y_a, y_b], w_out_even_bf, i, res, nw_next, last)
            else:
                p = _inproj(h, wt_odd, i, lambda j: j * PTN, O_MAIN // PTN, PTN)
                pgd = _inproj(h, wt_odd, i, lambda j: O_GD_COL0, 1, LANE)
                gkwp = jnp.concatenate([jnp.zeros((O_GD_LANE0, GKEY), F32), gla_gk_w[i]], axis=0)
                y = _gla(p, pgd, gkwp, _rows8(gla_gk_b[i]), _rows8(gla_norm_w[i]))
                o = _outproj([y], w_out_odd_bf, i, res, nw_next, last)
            if last:
                res = o[0]
            else:
                res, h = o
        outs.append(res)
    return jnp.stack(outs).astype(x.dtype)
```

```python
import functools
import math

import jax
import jax.numpy as jnp
from jax import lax
from jax.experimental import pallas as pl
from jax.experimental.pallas import tpu as pltpu

F32 = jnp.float32
BF16 = jnp.bfloat16

EPS = 1e-5
RW = 2048
RH = 64
RWKV_GN_EPS = 64e-5
RCHUNK = 64
SW = 2048
SHEAD = 64
SGROUPS = 4
SSTATE = 128
SCONV = 4
SCHUNK = 128
SGW = SW // SGROUPS
SHG = SGW // SHEAD
GHEADS = 4
GKEY = 1024
GVAL = 2048
GHK = 256
GHV = 512
GRANK = 16
GCHUNK = 64

LANE = 128
SUB = 8
VMEM_LIMIT = 56 * 1024 * 1024

PTN = 512
E_R, E_K, E_V, E_G = 0, 2048, 4096, 6144
E_MAIN = 8192
E_WD = 8192
E_AD = 8288
E_ZXBC = 8384
E_COLS = 13536
T_Z, T_X = 0, 2048
T_B, T_C = 4096, 4608
T_COLS = 5120
S_BLK = 256
S_LORA = 0
S_COLS = 2 * S_BLK
S_DT = S_COLS - LANE
S_DT_COL0 = E_COLS - S_BLK
S_DT_LANE0 = LANE - SW // SHEAD
O_COLS = 6160
O_GD_COL0 = O_COLS - LANE
O_GD_LANE0 = LANE - GRANK
O_Q, O_K, O_V, O_G = 0, 1024, 2048, 4096
O_MAIN = 6144

NN = (((1,), (0,)), ((), ()))
NT = (((1,), (1,)), ((), ()))
TN = (((0,), (0,)), ((), ()))


def _dg(a, b, dims=NN):
    return lax.dot_general(a, b, dims, preferred_element_type=F32)


def _bdot(a, b, dims=NN):
    return _dg(a.astype(BF16), b.astype(BF16), dims)


def _split2(x):
    hi = x.astype(BF16)
    lo = (x - hi.astype(F32)).astype(BF16)
    return hi, lo


def _dot3(a, b, dims=NN):
    ah, al = _split2(a)
    bh, bl = _split2(b)
    return _dg(ah, bh, dims) + (_dg(ah, bl, dims) + _dg(al, bh, dims))


EXP_NEG_HALF = math.exp(-0.5)


def _sigmoid(x):
    return 0.5 * jnp.tanh(0.5 * x) + 0.5


def _silu(x):
    return x * _sigmoid(x)


def _log_sigmoid(x):
    return jnp.minimum(x, 0.0) - jnp.log(1.0 + jnp.exp(-jnp.abs(x)))


def _softplus(x):
    return jnp.maximum(x, 0.0) + jnp.log(1.0 + jnp.exp(-jnp.abs(x)))


def _iota(shape, axis):
    return lax.broadcasted_iota(jnp.int32, shape, axis)


def _idiv(x, n):
    return lax.shift_right_logical(x, n.bit_length() - 1)


def _seg_cumsum(x, chunk):
    rin = _iota(x.shape, 0) & (chunk - 1)
    s = 1
    while s < chunk:
        x = x + jnp.where(rin >= s, pltpu.roll(x, s, 0), 0.0)
        s *= 2
    return x


def _chunk_row(x, chunk, r):
    t, w = x.shape
    x3 = x.reshape(t // chunk, chunk, w)
    return jnp.broadcast_to(x3[:, r:r + 1, :], x3.shape).reshape(t, w)


def _head_ones(n, head, scale=1.0):
    r = _idiv(_iota((n, n), 0), head)
    c = _idiv(_iota((n, n), 1), head)
    return jnp.where(r == c, scale, 0.0).astype(BF16)


def _params(**kw):
    return pltpu.CompilerParams(vmem_limit_bytes=VMEM_LIMIT, **kw)


def _rmsnorm_kernel(x_ref, w_ref, o_ref):
    x = x_ref[...]
    ms = jnp.mean(x * x, axis=-1, keepdims=True)
    o_ref[...] = (x * lax.rsqrt(ms + EPS) * w_ref[0:1, :]).astype(BF16)


def _rmsnorm(x, w8, tm=512):
    l, d = x.shape
    return pl.pallas_call(
        _rmsnorm_kernel,
        grid=(l // tm,),
        in_specs=[pl.BlockSpec((tm, d), lambda i: (i, 0)),
                  pl.BlockSpec((SUB, d), lambda i: (0, 0))],
        out_specs=pl.BlockSpec((tm, d), lambda i: (i, 0)),
        out_shape=jax.ShapeDtypeStruct((l, d), BF16),
        compiler_params=_params(dimension_semantics=("parallel",)),
        name="rmsnorm",
    )(x, w8)


HALO = 16


def _inproj_kernel(*refs, shift):
    if shift:
        h_ref, hh_ref, w_ref, mu_ref, o_ref = refs
    else:
        h_ref, w_ref, o_ref = refs
    w = w_ref[0].astype(BF16)
    p = _dg(h_ref[...], w, NT)
    if shift:
        prev = _dg(hh_ref[...], w, NT)[HALO - 1:HALO, :]
        prev = jnp.where(pl.program_id(0) == 0, 0.0, prev)
        pprev = jnp.where(_iota(p.shape, 0) == 0, prev, pltpu.roll(p, 1, 0))
        p = p + (pprev - p) * mu_ref[0:1, :]
    o_ref[...] = p


def _inproj(h, wt, layer, col0, nblk, tn, mu8=None, tm=2048):
    l, d = h.shape
    tm = min(tm, l)
    shift = mu8 is not None
    in_specs = [pl.BlockSpec((tm, d), lambda i, j: (i, 0))]
    args = [h]
    if shift:
        hb = tm // HALO
        in_specs.append(pl.BlockSpec((HALO, d), lambda i, j: (jnp.maximum(i * hb - 1, 0), 0)))
        args.append(h)
    def w_map(i, j):
        c = col0(j)
        return (layer, c if isinstance(c, int) else pl.multiple_of(c, SUB), 0)

    in_specs.append(pl.BlockSpec((pl.Element(1), pl.Element(tn), pl.Element(d)), w_map))
    args.append(wt)
    if shift:
        in_specs.append(pl.BlockSpec((SUB, tn), lambda i, j: (0, j)))
        args.append(mu8)
    return pl.pallas_call(
        functools.partial(_inproj_kernel, shift=shift),
        grid=(l // tm, nblk),
        in_specs=in_specs,
        out_specs=pl.BlockSpec((tm, tn), lambda i, j: (i, j)),
        out_shape=jax.ShapeDtypeStruct((l, nblk * tn), F32),
        compiler_params=_params(dimension_semantics=("parallel", "parallel")),
        name="inproj",
    )(*args)


def _outproj_kernel(*refs, n_in, final):
    y_refs = refs[:n_in]
    w_ref, res_ref, nw_ref = refs[n_in:n_in + 3]
    outs = refs[n_in + 3:]
    r = res_ref[...]
    for s in range(n_in):
        kw = y_refs[s].shape[1]
        r = r + jnp.dot(y_refs[s][...], w_ref[s * kw:(s + 1) * kw, :], preferred_element_type=F32)
    ms = jnp.mean(r * r, axis=-1, keepdims=True)
    hn = r * lax.rsqrt(ms + EPS) * nw_ref[0:1, :]
    if final:
        outs[0][...] = hn
    else:
        outs[0][...] = r
        outs[1][...] = hn.astype(BF16)


def _outproj(ys, w, layer, res, nw8, final, tm=512):
    l, d = res.shape
    n_in = len(ys)
    kdim = w.shape[1]
    tm = min(tm, l)
    in_specs = [pl.BlockSpec((tm, y.shape[1]), lambda i: (i, 0)) for y in ys]
    in_specs += [pl.BlockSpec((None, kdim, d), lambda i: (layer, 0, 0), pipeline_mode=pl.Buffered(1)),
                 pl.BlockSpec((tm, d), lambda i: (i, 0)),
                 pl.BlockSpec((SUB, d), lambda i: (0, 0))]
    row_spec = pl.BlockSpec((tm, d), lambda i: (i, 0))
    if final:
        out_specs = [row_spec]
        out_shape = [jax.ShapeDtypeStruct((l, d), F32)]
    else:
        out_specs = [row_spec, row_spec]
        out_shape = [jax.ShapeDtypeStruct((l, d), F32), jax.ShapeDtypeStruct((l, d), BF16)]
    return pl.pallas_call(
        functools.partial(_outproj_kernel, n_in=n_in, final=final),
        grid=(l // tm,),
        in_specs=in_specs,
        out_specs=out_specs,
        out_shape=out_shape,
        compiler_params=_params(dimension_semantics=("parallel",)),
        name="outproj",
    )(*ys, w, res, nw8)


def _vlora_kernel(v_ref, v1_ref, o_ref):
    o_ref[...] = _bdot(v_ref[...], v1_ref[...])


def _vlora(p, v1p, t=512):
    l = p.shape[0]
    t = min(t, l)
    cb = E_V // RW
    return pl.pallas_call(
        _vlora_kernel,
        grid=(l // t,),
        in_specs=[pl.BlockSpec((t, RW), lambda i: (i, cb)),
                  pl.BlockSpec((RW, LANE), lambda i: (0, 0))],
        out_specs=pl.BlockSpec((t, LANE), lambda i: (i, 0)),
        out_shape=jax.ShapeDtypeStruct((l, LANE), F32),
        compiler_params=_params(dimension_semantics=("parallel",)),
        name="rwkv_vlora",
    )(p, v1p)


RT = 2048
RHALF = 2048
REW = 2048
RST = 256
INV_BASE = 4
assert INV_BASE == 4


def _rwkv_prep_kernel(*refs, has_vmix):
    (r_ref, k_ref, v_ref, g_ref, lo_ref, pa_ref, w2_ref, a2_ref) = refs[:8]
    pos = 8
    if has_vmix:
        tl_ref, v2_ref, vf_ref = refs[pos:pos + 3]
        pos += 3
    (atp_ref, rp_ref, bh_ref, kh2_ref, vb_ref, u0_ref, y0_ref, bonus_ref, gs_ref, dl_ref) = refs[pos:]
    t = r_ref.shape[0]
    half = min(RHALF, t)
    blk = min(REW, t)
    cpb = blk // RCHUNK

    w0, a0, k_k, k_a = (pa_ref[i:i + 1, :] for i in range(4))
    r_k, v0 = pa_ref[4:5, :], pa_ref[7:8, :]
    hm = _head_ones(LANE, RH)
    first_row = (_iota((blk, LANE), 0) & (RCHUNK - 1)) == 0

    def front(r0):
        rs = slice(r0, r0 + blk)
        r, k, v, g, lo = r_ref[rs, :], k_ref[rs, :], v_ref[rs, :], g_ref[rs, :], lo_ref[rs, :]
        logd = -EXP_NEG_HALF * _sigmoid(w0 + _dot3(jnp.tanh(lo), w2_ref[...]))
        a = _sigmoid(a0 + _bdot(lo, a2_ref[...]))
        if has_vmix:
            v = v + (vf_ref[rs, :] - v) * _sigmoid(v0 + _bdot(tl_ref[rs, :], v2_ref[...]))

        kk = k * k_k
        k2h, k2l = _split2(kk * kk)
        ss = _dg(k2h, hm) + _dg(k2l, hm)
        kk = kk * lax.rsqrt(jnp.maximum(ss, 1e-24))
        kh = k * (1.0 + (a - 1.0) * k_a)
        av = -kk
        bv = kk * a

        bonus_ref[rs, :] = (_bdot(r * kh * r_k, hm) * v).astype(BF16)
        gs_ref[rs, :] = _silu(g).astype(BF16)

        cum = _seg_cumsum(logd, RCHUNK)
        e_c = jnp.exp(cum)
        e_n = jnp.exp(-cum)
        e_p = jnp.where(first_row, 1.0, pltpu.roll(e_c, 1, 0))
        p_end = jnp.exp(cum.reshape(cpb, RCHUNK, LANE)[:, RCHUNK - 1:RCHUNK, :])
        e_l = e_n * jnp.broadcast_to(p_end, (cpb, RCHUNK, LANE)).reshape(blk, LANE)
        at = av * e_p
        rt = r * e_c
        bt = bv * e_n
        kt = kh * e_n
        bh_ref[rs, :] = (bv * e_l).astype(BF16)
        kh2_ref[rs, :] = (kh * e_l).astype(BF16)
        vb_ref[rs, :] = v.astype(BF16)
        c0 = r0 // RCHUNK
        dl_ref[c0 * SUB:(c0 + cpb) * SUB, :] = (
            jnp.broadcast_to(p_end, (cpb, SUB, LANE)).reshape(cpb * SUB, LANE))
        out = []
        for c in range(cpb):
            cr = slice(c * RCHUNK, (c + 1) * RCHUNK)
            out.append((r0 + c * RCHUNK, at[cr], rt[cr], bt[cr], kt[cr], v[cr]))
        return out

    lane = _iota((RCHUNK, LANE), 1)
    si = lane & (RH - 1)
    ti = _iota((RCHUNK, LANE), 0)
    strict = si < ti
    incl = si <= ti
    eye = jnp.where(si == ti, 1.0, 0.0)
    head0 = lane < RH
    blk_same = {INV_BASE: _idiv(si, INV_BASE) == _idiv(ti, INV_BASE)}
    blk_lower = {}
    bs = INV_BASE
    while bs < RCHUNK:
        blk_lower[bs] = ((_idiv(si, 2 * bs) == _idiv(ti, 2 * bs))
                         & ((ti & bs) != 0) & ((si & bs) == 0))
        bs *= 2

    def stack(xc):
        xc = xc.astype(BF16)
        zero = jnp.zeros_like(xc)
        return jnp.concatenate([jnp.where(head0, xc, zero), jnp.where(head0, zero, xc)], axis=0)

    def chains(ops):
        chunks = range(len(ops))
        ats = [stack(o[1]) for o in ops]
        vs = [stack(o[5]) for o in ops]
        amat = []
        for (_, at, rt, bt, kt, _) in ops:
            lhs = jnp.concatenate([at, rt], axis=0).astype(BF16)
            rhs = jnp.concatenate([stack(bt), stack(kt)], axis=0)
            amat.append(_dg(lhs, rhs, NT))
        yield
        a_ab = [jnp.where(strict, m[:RCHUNK, :LANE], 0.0) for m in amat]
        a_ak = [jnp.where(strict, m[:RCHUNK, LANE:], 0.0) for m in amat]
        a_rb = [jnp.where(incl, m[RCHUNK:, :LANE], 0.0).astype(BF16) for m in amat]
        a_rk = [jnp.where(incl, m[RCHUNK:, LANE:], 0.0) for m in amat]
        akv2 = [_dg(jnp.concatenate([a_ak[c], a_rk[c]], axis=0).astype(BF16), vs[c]) for c in chunks]
        akv = [m[:RCHUNK] for m in akv2]
        arkv = [m[RCHUNK:] for m in akv2]
        yield
        nd = [jnp.where(blk_same[INV_BASE], m, 0.0) for m in a_ab]
        inv = [eye + m for m in nd]
        pw = [_dg(m.astype(BF16), stack(m)) for m in nd]
        yield
        inv = [inv[c] + _dg(pw[c].astype(BF16), stack(inv[c])) for c in chunks]
        yield
        bs = INV_BASE
        while bs < RCHUNK:
            n21 = [stack(jnp.where(blk_lower[bs], m, 0.0)) for m in a_ab]
            tn = [_dg(inv[c].astype(BF16), n21[c]) for c in chunks]
            yield
            inv = [inv[c] + _dg(tn[c].astype(BF16), stack(inv[c])) for c in chunks]
            bs *= 2
            yield
        x = [_dg(inv[c].astype(BF16), jnp.concatenate([ats[c], stack(akv[c])], axis=1)) for c in chunks]
        yield
        z = [_dg(a_rb[c], jnp.concatenate([stack(x[c][:, :LANE]), stack(x[c][:, LANE:])], axis=1))
             for c in chunks]
        for c in chunks:
            orow = slice(ops[c][0], ops[c][0] + RCHUNK)
            atp_ref[orow, :] = x[c][:, :LANE].astype(BF16)
            u0_ref[orow, :] = x[c][:, LANE:].astype(BF16)
            rp_ref[orow, :] = (ops[c][2] + z[c][:, :LANE]).astype(BF16)
            y0_ref[orow, :] = (z[c][:, LANE:] + arkv[c]).astype(BF16)

    halves = [list(range(h0, h0 + half, blk)) for h0 in range(0, t, half)]
    ops = [o for r0 in halves[0] for o in front(r0)]
    for hi in range(len(halves)):
        pending = list(halves[hi + 1]) if hi + 1 < len(halves) else []
        nxt = []
        for _ in chains(ops):
            if pending:
                nxt += front(pending.pop(0))
        while pending:
            nxt += front(pending.pop(0))
        ops = nxt


def _rwkv_prep(p, ps, pa, w2p, a2p, vmix):
    l = p.shape[0]
    t = min(RT, l)
    npair = RW // LANE
    has_vmix = vmix is not None

    def main(off):
        cb = off // LANE
        return pl.BlockSpec((t, LANE), lambda q, i: (i, cb + q))

    par = pl.BlockSpec((SUB, LANE), lambda q, i: (0, q))
    lora = pl.BlockSpec((LANE, LANE), lambda q, i: (0, q))
    lora_in = pl.BlockSpec((S_BLK, LANE), lambda q, i: (0, q))
    row = pl.BlockSpec((t, LANE), lambda q, i: (i, q))

    in_specs = [main(E_R), main(E_K), main(E_V), main(E_G),
                pl.BlockSpec((t, S_BLK), lambda q, i: (i, S_LORA // S_BLK)), par, lora_in, lora_in]
    args = [p] * 4 + [ps, pa, w2p, a2p]
    if has_vmix:
        tl, v2p, p_first = vmix
        in_specs += [pl.BlockSpec((t, LANE), lambda q, i: (i, 0)), lora, main(E_V)]
        args += [tl, v2p, p_first]
    dl_rows = (t // RCHUNK) * SUB
    out_specs = [row] * 9 + [pl.BlockSpec((dl_rows, LANE), lambda q, i: (i, q))]
    sd = jax.ShapeDtypeStruct
    out_shape = ([sd((l, RW), BF16)] * 9
                 + [sd((l // RCHUNK * SUB, RW), F32)])
    return pl.pallas_call(
        functools.partial(_rwkv_prep_kernel, has_vmix=has_vmix),
        grid=(npair, l // t),
        in_specs=in_specs,
        out_specs=out_specs,
        out_shape=out_shape,
        compiler_params=_params(dimension_semantics=("parallel", "parallel")),
        name="rwkv_prep",
    )(*args)


def _rwkv_scan_kernel(atp_ref, rp_ref, bh_ref, kh_ref, vb_ref, u0_ref, y0_ref,
                      bonus_ref, gs_ref, dl_ref, pa_ref, o_ref, s_ref, y_ref):
    @pl.when(pl.program_id(0) == 0)
    def _():
        s_ref[...] = jnp.zeros_like(s_ref)

    t = atp_ref.shape[0]
    npair = RW // LANE
    bd = _idiv(_iota((LANE, LANE), 0), RH) == _idiv(_iota((LANE, LANE), 1), RH)
    hmean = _head_ones(LANE, RH, 1.0 / RH)

    pairs = range(npair)
    lanes = [slice(q * LANE, (q + 1) * LANE) for q in pairs]
    for c in range(t // RCHUNK):
        rows = slice(c * RCHUNK, (c + 1) * RCHUNK)
        s_old = [s_ref[q] for q in pairs]
        o = [_dg(jnp.concatenate([atp_ref[rows, lanes[q]], rp_ref[rows, lanes[q]]], axis=0),
                 s_old[q].astype(BF16), NT) for q in pairs]
        ds = []
        for q in pairs:
            u = o[q][:RCHUNK] + u0_ref[rows, lanes[q]].astype(F32)
            y_ref[rows, lanes[q]] = o[q][RCHUNK:] + y0_ref[rows, lanes[q]].astype(F32)
            uv = jnp.concatenate([u.astype(BF16), vb_ref[rows, lanes[q]]], axis=0)
            bk = jnp.concatenate([bh_ref[rows, lanes[q]], kh_ref[rows, lanes[q]]], axis=0)
            ds.append(_dg(uv, bk, TN))
        for q in pairs:
            dl = dl_ref[c * SUB:c * SUB + 1, lanes[q]]
            s_ref[q] = s_old[q] * dl + jnp.where(bd, ds[q], 0.0)

    splits = [_split2(y_ref[:, lanes[q]]) for q in pairs]
    means = [_dg(yh, hmean) + _dg(yl, hmean) for (yh, yl) in splits]
    devs = [y_ref[:, lanes[q]] - means[q] for q in pairs]
    varis = [_bdot(devs[q] * devs[q], hmean) for q in pairs]
    for q in pairs:
        sl = lanes[q]
        yn = devs[q] * lax.rsqrt(varis[q] + RWKV_GN_EPS) * pa_ref[5:6, sl] + pa_ref[6:7, sl]
        o_ref[:, sl] = ((yn + bonus_ref[:, sl].astype(F32)) * gs_ref[:, sl].astype(F32)).astype(BF16)


def _rwkv_scan(atp, rp, bh, kh, vb, u0, y0, bonus, gs, dl, pa):
    l = atp.shape[0]
    t = min(RST, l)
    row = pl.BlockSpec((t, RW), lambda i: (i, 0))
    return pl.pallas_call(
        _rwkv_scan_kernel,
        grid=(l // t,),
        in_specs=[row] * 9 + [pl.BlockSpec(((t // RCHUNK) * SUB, RW), lambda i: (i, 0)),
                              pl.BlockSpec((SUB, RW), lambda i: (0, 0))],
        out_specs=row,
        out_shape=jax.ShapeDtypeStruct((l, RW), BF16),
        scratch_shapes=[pltpu.VMEM((RW // LANE, LANE, LANE), F32), pltpu.VMEM((t, RW), F32)],
        compiler_params=_params(dimension_semantics=("arbitrary",)),
        name="rwkv_scan",
    )(atp, rp, bh, kh, vb, u0, y0, bonus, gs, dl, pa)


def _ssd_conv(main_ref, halo_ref, wb_ref, buf_ref, first):
    q = main_ref.shape[0]
    buf_ref[0:SUB, :] = jnp.where(first, 0.0, halo_ref[...])
    buf_ref[SUB:SUB + q, :] = main_ref[...]
    xb = buf_ref[...]
    acc = wb_ref[SCONV:SCONV + 1, :] + xb[SUB:, :] * wb_ref[SCONV - 1:SCONV, :]
    for s in range(1, SCONV):
        acc = acc + pltpu.roll(xb, s, 0)[SUB:, :] * wb_ref[SCONV - 1 - s:SCONV - s, :]
    return _silu(acc)


def _ssd_kernel(z_ref, x_ref, b_ref, c_ref, dt_ref, xh_ref, bhalo_ref, chalo_ref,
                cwx_ref, cwb_ref, cwc_ref, hp_ref, dx_ref, nw_ref,
                o_ref, prev_ref, xbuf_ref, bbuf_ref, cbuf_ref, ybuf_ref):
    first = pl.program_id(0) == 0
    q = z_ref.shape[0]
    groups = range(SGROUPS)

    @pl.when(first)
    def _():
        prev_ref[...] = jnp.zeros_like(prev_ref)

    xs = _ssd_conv(x_ref, xh_ref, cwx_ref, xbuf_ref, first)
    bmb = _ssd_conv(b_ref, bhalo_ref, cwb_ref, bbuf_ref, first).astype(BF16)
    cmb = _ssd_conv(c_ref, chalo_ref, cwc_ref, cbuf_ref, first).astype(BF16)

    dt = _softplus(dt_ref[...] + hp_ref[0:1, :])
    a_neg = -jnp.exp(hp_ref[1:2, :])
    acs = _seg_cumsum(dt * a_neg, q)
    acs_t = acs.T
    acs_last = acs[q - 1:q, :]

    expand = jnp.where(_iota((LANE, SW), 0) == _idiv(_iota((LANE, SW), 1), SHEAD) + S_DT_LANE0,
                       1.0, 0.0).astype(BF16)
    th, tl = _split2(jnp.concatenate([dt, jnp.exp(acs), jnp.exp(acs_last - acs)], axis=0))
    ex = _dg(jnp.concatenate([th, tl], axis=0), expand)
    ex = ex[:3 * q] + ex[3 * q:]
    dt_e, eacs_e, ds_e = ex[:q], ex[q:2 * q], ex[2 * q:]
    xc = xs * dt_e
    xcd = (xc * ds_e).astype(BF16)

    gl = [slice(g * SSTATE, (g + 1) * SSTATE) for g in groups]
    gc = [slice(g * SGW, (g + 1) * SGW) for g in groups]
    scores = [_dg(cmb[:, gl[g]], bmb[:, gl[g]], NT) for g in groups]
    prev = prev_ref[...]
    prevb = prev.astype(BF16)
    y_off = [_dg(cmb[:, gl[g]], prevb[:, gc[g]]) for g in groups]
    states = [_dg(bmb[:, gl[g]], xcd[:, gc[g]], TN) for g in groups]
    for g in groups:
        prev_ref[:, gc[g]] = prev[:, gc[g]] * eacs_e[q - 1:q, gc[g]] + states[g]

    causal = _iota((q, q), 1) <= _iota((q, q), 0)
    for h in range(SW // SHEAD):
        lane = S_DT_LANE0 + h
        hc = slice(h * SHEAD, (h + 1) * SHEAD)
        diff = acs[:, lane:lane + 1] - acs_t[lane:lane + 1, :]
        lm = jnp.exp(jnp.where(causal, diff, -jnp.inf))
        ybuf_ref[:, hc] = _bdot(scores[h // SHG] * lm, xc[:, hc])
    for g in groups:
        y = ybuf_ref[:, gc[g]] + y_off[g] * eacs_e[:, gc[g]] + dx_ref[0:1, gc[g]] * xs[:, gc[g]]
        y = y * _silu(z_ref[:, gc[g]])
        ms = jnp.mean(y * y, axis=-1, keepdims=True)
        o_ref[:, gc[g]] = (y * lax.rsqrt(ms + EPS) * nw_ref[0:1, gc[g]]).astype(BF16)


def _ssd(p, ps, cw8, hp, dx8, nw8):
    l = p.shape[0]
    q = SCHUNK
    hb = q // SUB
    gn = SGROUPS * SSTATE

    def halo_idx(i):
        return jnp.maximum(i * hb - 1, 0)

    zb, xb = T_Z // SW, T_X // SW
    bb, cb, db = T_B // gn, T_C // gn, S_DT // LANE
    in_specs = [
        pl.BlockSpec((q, SW), lambda i: (i, zb)),
        pl.BlockSpec((q, SW), lambda i: (i, xb)),
        pl.BlockSpec((q, gn), lambda i: (i, bb)),
        pl.BlockSpec((q, gn), lambda i: (i, cb)),
        pl.BlockSpec((q, LANE), lambda i: (i, db)),
        pl.BlockSpec((SUB, SW), lambda i: (halo_idx(i), xb)),
        pl.BlockSpec((SUB, gn), lambda i: (halo_idx(i), bb)),
        pl.BlockSpec((SUB, gn), lambda i: (halo_idx(i), cb)),
        pl.BlockSpec((SUB, SW), lambda i: (0, 0)),
        pl.BlockSpec((SUB, gn), lambda i: (0, SW // gn)),
        pl.BlockSpec((SUB, gn), lambda i: (0, SW // gn + 1)),
        pl.BlockSpec((SUB, LANE), lambda i: (0, 0)),
        pl.BlockSpec((SUB, SW), lambda i: (0, 0)),
        pl.BlockSpec((SUB, SW), lambda i: (0, 0)),
    ]
    return pl.pallas_call(
        _ssd_kernel,
        grid=(l // q,),
        in_specs=in_specs,
        out_specs=pl.BlockSpec((q, SW), lambda i: (i, 0)),
        out_shape=jax.ShapeDtypeStruct((l, SW), BF16),
        scratch_shapes=[pltpu.VMEM((SSTATE, SW), F32),
                        pltpu.VMEM((q + SUB, SW), F32),
                        pltpu.VMEM((q + SUB, gn), F32),
                        pltpu.VMEM((q + SUB, gn), F32),
                        pltpu.VMEM((q, SW), F32)],
        compiler_params=_params(dimension_semantics=("arbitrary",)),
        name="ssd",
    )(p, p, p, p, ps, p, p, p, cw8, cw8, cw8, hp, dx8, nw8)


GT = 256


def _gla_kernel(q_ref, k_ref, v_ref, g_ref, gd_ref, gkw_ref, gkb_ref, nw_ref, o_ref, st_ref, oi_ref):
    @pl.when(pl.program_id(0) == 0)
    def _():
        st_ref[...] = jnp.zeros_like(st_ref)

    t = q_ref.shape[0]
    nchunk = t // GCHUNK
    heads = range(GHEADS)
    kl = [slice(h * GHK, (h + 1) * GHK) for h in heads]
    vl = [slice(h * GHV, (h + 1) * GHV) for h in heads]
    gk = _log_sigmoid(_dot3(gd_ref[...], gkw_ref[...]) + gkb_ref[0:1, :]) * (1.0 / 16.0)
    bc = _seg_cumsum(gk, GCHUNK)
    mid = _chunk_row(bc, GCHUNK, GCHUNK // 2)
    last = _chunk_row(bc, GCHUNK, GCHUNK - 1)
    qs = q_ref[...] * (GHK ** -0.5)
    k = k_ref[...]
    vb = v_ref[...].astype(BF16)
    qm = (qs * jnp.exp(bc - mid)).astype(BF16)
    km = (k * jnp.exp(mid - bc)).astype(BF16)
    q_in = (qs * jnp.exp(bc)).astype(BF16)
    k_st = (k * jnp.exp(last - bc)).astype(BF16)
    dlast = jnp.exp(last)

    ri = _iota((t, t), 0)
    ci = _iota((t, t), 1)
    incl = (_idiv(ri, GCHUNK) == _idiv(ci, GCHUNK)) & (ci <= ri)
    attn = [jnp.where(incl, _dg(qm[:, kl[h]], km[:, kl[h]], NT), 0.0).astype(BF16) for h in heads]
    o_intra = [_dg(attn[h], vb[:, vl[h]]) for h in heads]

    for c in range(nchunk):
        rows = slice(c * GCHUNK, (c + 1) * GCHUNK)
        st = [st_ref[h] for h in heads]
        for h in heads:
            oi_ref[rows, vl[h]] = _dg(q_in[rows, kl[h]], st[h].astype(BF16), NT)
        for h in heads:
            st_ref[h] = (st[h] * dlast[c * GCHUNK:c * GCHUNK + 1, kl[h]]
                         + _dg(vb[rows, vl[h]], k_st[rows, kl[h]], TN))
    for h in heads:
        o = o_intra[h] + oi_ref[:, vl[h]]
        ms = jnp.mean(o * o, axis=-1, keepdims=True)
        o = o * lax.rsqrt(ms + EPS) * nw_ref[0:1, :]
        o_ref[:, vl[h]] = (o * _silu(g_ref[:, vl[h]])).astype(BF16)


def _gla(p, pgd, gkwp, gkb8, nw8):
    l = p.shape[0]
    t = min(GT, l)
    return pl.pallas_call(
        _gla_kernel,
        grid=(l // t,),
        in_specs=[pl.BlockSpec((t, GKEY), lambda i: (i, O_Q // GKEY)),
                  pl.BlockSpec((t, GKEY), lambda i: (i, O_K // GKEY)),
                  pl.BlockSpec((t, GVAL), lambda i: (i, O_V // GVAL)),
                  pl.BlockSpec((t, GVAL), lambda i: (i, O_G // GVAL)),
                  pl.BlockSpec((t, LANE), lambda i: (i, 0)),
                  pl.BlockSpec((LANE, GKEY), lambda i: (0, 0)),
                  pl.BlockSpec((SUB, GKEY), lambda i: (0, 0)),
                  pl.BlockSpec((SUB, GHV), lambda i: (0, 0))],
        out_specs=pl.BlockSpec((t, GVAL), lambda i: (i, 0)),
        out_shape=jax.ShapeDtypeStruct((l, GVAL), BF16),
        scratch_shapes=[pltpu.VMEM((GHEADS, GHV, GHK), F32), pltpu.VMEM((t, GVAL), F32)],
        compiler_params=_params(dimension_semantics=("arbitrary",)),
        name="gla",
    )(p, p, p, p, pgd, gkwp, gkb8, nw8)


def _rows8(*rows):
    n = rows[0].shape[0]
    parts = [r.astype(F32)[None, :] for r in rows]
    if len(rows) < SUB:
        parts.append(jnp.zeros((SUB - len(rows), n), F32))
    return jnp.concatenate(parts, axis=0)


def _pad_cols(w, n):
    return jnp.concatenate([w, jnp.zeros((w.shape[0], n - w.shape[1]), w.dtype)], axis=1)


def _pad_rows(w, n):
    return jnp.concatenate([w, jnp.zeros((n - w.shape[0], w.shape[1]), w.dtype)], axis=0)


def _small_col0(j):
    return jnp.where(j == 0, E_WD, S_DT_COL0)


def _lora_rows(w, row0):
    return jnp.concatenate([jnp.zeros((row0, w.shape[1]), w.dtype), w,
                            jnp.zeros((S_BLK - row0 - w.shape[0], w.shape[1]), w.dtype)], axis=0)


def kernel(x, norm_w, final_norm_w, w_in_even, w_out_even, rwkv_mu, rwkv_w0, rwkv_w2, rwkv_a0, rwkv_a2, rwkv_k_k, rwkv_k_a, rwkv_r_k, rwkv_ln_w, rwkv_ln_b, rwkv_v0, rwkv_v1, rwkv_v2, ssm_conv_w, ssm_conv_b, ssm_dt_bias, ssm_A_log, ssm_D, ssm_norm_w, w_in_odd, w_out_odd, gla_gk_w, gla_gk_b, gla_norm_w):
    bsz, l, d = x.shape
    depth = norm_w.shape[0]
    w_out_even_bf = w_out_even.astype(BF16)
    w_out_odd_bf = w_out_odd.astype(BF16)
    wt_even = jnp.swapaxes(w_in_even, 1, 2)
    wt_odd = jnp.swapaxes(w_in_odd, 1, 2)
    outs = []
    for b in range(bsz):
        res = x[b]
        h = _rmsnorm(res, _rows8(norm_w[0]))
        p_first = None
        for layer in range(depth):
            i = layer // 2
            last = layer == depth - 1
            nw_next = _rows8(final_norm_w if last else norm_w[layer + 1])
            if layer % 2 == 0:
                mu = rwkv_mu[i]
                mu_small = jnp.concatenate([mu[E_WD:E_ZXBC], jnp.zeros((S_COLS - (E_ZXBC - E_WD),), F32)])
                p = _inproj(h, wt_even, i, lambda j: j * PTN, E_MAIN // PTN, PTN, mu8=_rows8(mu[:E_MAIN]))
                pt = _inproj(h, wt_even, i, lambda j: E_ZXBC + j * PTN, T_COLS // PTN, PTN)
                ps = _inproj(h, wt_even, i, _small_col0, S_COLS // S_BLK, S_BLK, mu8=_rows8(mu_small))
                v0 = rwkv_v0[i - 1] if i > 0 else jnp.zeros((RW,), F32)
                pa = _rows8(rwkv_w0[i], rwkv_a0[i], rwkv_k_k[i], rwkv_k_a[i],
                            rwkv_r_k[i].reshape(RW), rwkv_ln_w[i], rwkv_ln_b[i], v0)
                w2p = _lora_rows(rwkv_w2[i], 0)
                a2p = _lora_rows(rwkv_a2[i], E_AD - E_WD)
                if i == 0:
                    vmix = None
                    p_first = p
                else:
                    tl = _vlora(p, _pad_cols(rwkv_v1[i - 1], LANE))
                    vmix = (tl, _pad_rows(rwkv_v2[i - 1], LANE), p_first)
                (atp, rp, bh, kh, vb, u0, y0, bonus, gs, dl) = _rwkv_prep(p, ps, pa, w2p, a2p, vmix)
                y_a = _rwkv_scan(atp, rp, bh, kh, vb, u0, y0, bonus, gs, dl, pa)

                cw8 = jnp.concatenate([ssm_conv_w[i], ssm_conv_b[i][None, :],
                                       jnp.zeros((SUB - SCONV - 1, ssm_conv_w.shape[2]), F32)], axis=0)
                lead = jnp.zeros((S_DT_LANE0,), F32)
                hp = _rows8(jnp.concatenate([lead, ssm_dt_bias[i]]), jnp.concatenate([lead, ssm_A_log[i]]))
                dx8 = _rows8(jnp.repeat(ssm_D[i], SHEAD))
                y_b = _ssd(pt, ps, cw8, hp, dx8, _rows8(ssm_norm_w[i]))
                o = _outproj([y_a, y_b], w_out_even_bf, i, res, nw_next, last)
            else:
                p = _inproj(h, wt_odd, i, lambda j: j * PTN, O_MAIN // PTN, PTN)
                pgd = _inproj(h, wt_odd, i, lambda j: O_GD_COL0, 1, LANE)
                gkwp = jnp.concatenate([jnp.zeros((O_GD_LANE0, GKEY), F32), gla_gk_w[i]], axis=0)
                y = _gla(p, pgd, gkwp, _rows8(gla_gk_b[i]), _rows8(gla_norm_w[i]))
                o = _outproj([y], w_out_odd_bf, i, res, nw_next, last)
            if last:
                res = o[0]
            else:
                res, h = o
        outs.append(res)
    return jnp.stack(outs).astype(x.dtype)
```
